```python
import math
import jax, jax.numpy as jnp
from jax import lax
import numpy as np

D_MODEL = 2048
BATCH = 1
SEQ = 16384
DEPTH = 2

CHUNK = 64
Q_BLOCK = 128
HEAD_DIM = 128
N_H = (D_MODEL // 2) // HEAD_DIM
BRANCH_WIDTH = N_H * HEAD_DIM
A_LEFT_CHUNKS = 8
A_BAND = (A_LEFT_CHUNKS + 1) * CHUNK
REL_CLIP = 128
D_QK_DIM = HEAD_DIM // 2
D_V_DIM = HEAD_DIM
ROPE_THETA = 500000.0
ROPE_DIM = D_QK_DIM // 4
EPS = 1e-6
N_EVEN = (DEPTH + 1) // 2
N_ODD = DEPTH // 2
EVEN_SIZES = [BRANCH_WIDTH] * 8
ODD_SIZES = [BRANCH_WIDTH] * 4 + [N_H] + [N_H * D_QK_DIM] * 4 + [BRANCH_WIDTH] * 2
EVEN_IN = sum(EVEN_SIZES)
ODD_IN = sum(ODD_SIZES)

kernel_name = "hybrid_chunk_streaming_encoder"


def rms_norm(x, g):
    xf = x.astype(jnp.float32)
    y = xf * lax.rsqrt(jnp.mean(xf * xf, axis=-1, keepdims=True) + EPS)
    return (y * g.astype(jnp.float32)).astype(x.dtype)


def split_heads(t, d):
    b, s, _ = t.shape
    return t.reshape(b, s, -1, d).transpose(0, 2, 1, 3)


def merge_heads(t):
    b, h, s, d = t.shape
    return t.transpose(0, 2, 1, 3).reshape(b, s, h * d)


def split_cols(t, sizes):
    idx = [int(v) for v in np.cumsum(sizes)[:-1]]
    return jnp.split(t, idx, axis=-1)


def sweep_blocks(fn, block, *qs):
    b, h, s, _ = qs[0].shape
    nb = s // block
    blocks = tuple(q.reshape(b, h, nb, block, q.shape[-1]).transpose(2, 0, 1, 3, 4) for q in qs)
    out = lax.map(lambda a: fn(a[0], *a[1:]), (jnp.arange(nb), *blocks))
    return out.transpose(1, 2, 0, 3, 4).reshape(b, h, s, out.shape[-1])


def partial_rotary(x, positions):
    inv_freq = ROPE_THETA ** (-jnp.arange(0, ROPE_DIM, 2, dtype=jnp.float32) / ROPE_DIM)
    ang = positions.astype(jnp.float32)[:, None, :, None] * inv_freq
    cos, sin = jnp.cos(ang), jnp.sin(ang)
    xr = x[..., :ROPE_DIM].astype(jnp.float32)
    x1, x2 = xr[..., :ROPE_DIM // 2], xr[..., ROPE_DIM // 2:]
    rot = jnp.concatenate([x1 * cos - x2 * sin, x2 * cos + x1 * sin], axis=-1)
    return jnp.concatenate([rot.astype(x.dtype), x[..., ROPE_DIM:]], axis=-1)


def chunked_relpos_attention(q, k, v, rel_bias):
    d = q.shape[-1]
    pad = A_LEFT_CHUNKS * CHUNK
    k_pad = jnp.pad(k, ((0, 0), (0, 0), (pad, 0), (0, 0)))
    v_pad = jnp.pad(v, ((0, 0), (0, 0), (pad, 0), (0, 0)))
    dist = jnp.arange(CHUNK)[:, None] + pad - jnp.arange(A_BAND)[None, :]
    bias = rel_bias.astype(jnp.float32)[:, jnp.clip(dist, -REL_CLIP, REL_CLIP) + REL_CLIP]
    scale = d ** -0.5

    def one_chunk(c, qc):
        start = c * CHUNK
        kb = lax.dynamic_slice_in_dim(k_pad, start, A_BAND, axis=2)
        vb = lax.dynamic_slice_in_dim(v_pad, start, A_BAND, axis=2)
        s = jnp.einsum('bhqd,bhkd->bhqk', qc, kb).astype(jnp.float32) * scale + bias
        valid = (start - pad + jnp.arange(A_BAND)) >= 0
        p = jax.nn.softmax(jnp.where(valid, s, -jnp.inf), axis=-1)
        return jnp.einsum('bhqk,bhkd->bhqd', p.astype(v.dtype), vb)

    return sweep_blocks(one_chunk, CHUNK, q)


def stick_breaking_attention(q, k, v):
    s_len, d = q.shape[2], q.shape[-1]
    scale = d ** -0.5
    key_pos = jnp.arange(s_len)

    def block(b, qb):
        q_pos = b * Q_BLOCK + jnp.arange(Q_BLOCK)
        z = jnp.einsum('bhqd,bhkd->bhqk', qb, k).astype(jnp.float32) * scale
        strict = key_pos[None, :] < q_pos[:, None]
        log_keep = jnp.where(strict, jax.nn.log_sigmoid(-z), 0.0)
        later = lax.cumsum(log_keep, axis=3, reverse=True) - log_keep
        w = jnp.where(strict, jnp.exp(jax.nn.log_sigmoid(z) + later), 0.0)
        return jnp.einsum('bhqk,bhkd->bhqd', w.astype(v.dtype), v)

    return sweep_blocks(block, Q_BLOCK, q)


def forgetting_attention(q, k, v, log_f):
    s_len, d = q.shape[2], q.shape[-1]
    scale = d ** -0.5
    key_pos = jnp.arange(s_len)
    cum = lax.cumsum(log_f, axis=2)

    def block(b, qb):
        q_pos = b * Q_BLOCK + jnp.arange(Q_BLOCK)
        c_q = lax.dynamic_slice_in_dim(cum, b * Q_BLOCK, Q_BLOCK, axis=2)
        s = (jnp.einsum('bhqd,bhkd->bhqk', qb, k).astype(jnp.float32) * scale
             + c_q[..., :, None] - cum[..., None, :])
        causal = key_pos[None, :] <= q_pos[:, None]
        p = jax.nn.softmax(jnp.where(causal, s, -jnp.inf), axis=-1)
        return jnp.einsum('bhqk,bhkd->bhqd', p.astype(v.dtype), v)

    return sweep_blocks(block, Q_BLOCK, q)


def differential_attention(q1, q2, k1, k2, v, lam):
    s_len, d = q1.shape[2], q1.shape[-1]
    scale = d ** -0.5
    key_chunk = jnp.arange(s_len) // CHUNK

    def block(b, q1b, q2b):
        q_chunk = (b * Q_BLOCK + jnp.arange(Q_BLOCK)) // CHUNK
        mask = key_chunk[None, :] <= q_chunk[:, None]
        s1 = jnp.einsum('bhqd,bhkd->bhqk', q1b, k1).astype(jnp.float32) * scale
        s2 = jnp.einsum('bhqd,bhkd->bhqk', q2b, k2).astype(jnp.float32) * scale
        p1 = jax.nn.softmax(jnp.where(mask, s1, -jnp.inf), axis=-1)
        p2 = jax.nn.softmax(jnp.where(mask, s2, -jnp.inf), axis=-1)
        w = p1 - lam * p2
        return jnp.einsum('bhqk,bhkd->bhqd', w.astype(v.dtype), v)

    return sweep_blocks(block, Q_BLOCK, q1, q2)


def even_mixer(u, w_in, rel_bias, w_out):
    proj = u @ w_in
    qA, kA, vA, gA, qB, kB, vB, gB = split_cols(proj, EVEN_SIZES)
    oA = chunked_relpos_attention(split_heads(qA, HEAD_DIM), split_heads(kA, HEAD_DIM),
                                  split_heads(vA, HEAD_DIM), rel_bias)
    oB = stick_breaking_attention(split_heads(qB, HEAD_DIM), split_heads(kB, HEAD_DIM),
                                  split_heads(vB, HEAD_DIM))
    y = jnp.concatenate([merge_heads(oA) * jax.nn.silu(gA),
                         merge_heads(oB) * jax.nn.silu(gB)], axis=-1)
    return y @ w_out


def odd_mixer(u, positions, w_in, forget_bias, lq1, lk1, lq2, lk2, subln, w_out, lambda_init):
    proj = u @ w_in
    qC, kC, vC, gC, fC, q1, q2, k1, k2, vD, gD = split_cols(proj, ODD_SIZES)
    log_f = jax.nn.log_sigmoid(fC.astype(jnp.float32) + forget_bias.astype(jnp.float32))
    log_f = log_f.transpose(0, 2, 1)
    oC = forgetting_attention(split_heads(qC, HEAD_DIM), split_heads(kC, HEAD_DIM),
                              split_heads(vC, HEAD_DIM), log_f)
    lam = (jnp.exp(jnp.sum(lq1.astype(jnp.float32) * lk1.astype(jnp.float32)))
           - jnp.exp(jnp.sum(lq2.astype(jnp.float32) * lk2.astype(jnp.float32))) + lambda_init)
    rq1 = partial_rotary(split_heads(q1, D_QK_DIM), positions)
    rq2 = partial_rotary(split_heads(q2, D_QK_DIM), positions)
    rk1 = partial_rotary(split_heads(k1, D_QK_DIM), positions)
    rk2 = partial_rotary(split_heads(k2, D_QK_DIM), positions)
    oD = differential_attention(rq1, rq2, rk1, rk2, split_heads(vD, D_V_DIM), lam)
    oD = rms_norm(oD, subln) * (1.0 - lambda_init)
    y = jnp.concatenate([merge_heads(oC) * jax.nn.silu(gC),
                         merge_heads(oD) * jax.nn.silu(gD)], axis=-1)
    return y @ w_out


def setup_inputs(seed: int = 0) -> dict:
    key = jax.random.key(seed)
    ks = jax.random.split(key, 20)
    f32 = jnp.float32
    x = jax.random.normal(ks[0], (BATCH, SEQ, D_MODEL), f32)
    offset = jax.random.randint(ks[1], (BATCH, 1), 0, 64) * CHUNK
    positions = (offset + jnp.arange(SEQ, dtype=jnp.int32)[None, :]).astype(jnp.int32)
    return {
        "x": x,
        "positions": positions,
        "even_w_in": jax.random.normal(ks[2], (N_EVEN, D_MODEL, EVEN_IN), f32) * D_MODEL ** -0.5,
        "even_rel_bias": jax.random.normal(ks[3], (N_EVEN, N_H, 2 * REL_CLIP + 1), f32) * 0.5,
        "even_w_out": jax.random.normal(ks[4], (N_EVEN, 2 * BRANCH_WIDTH, D_MODEL), f32) * (2 * BRANCH_WIDTH) ** -0.5,
        "even_norm_pre": 1.0 + 0.05 * jax.random.normal(ks[5], (N_EVEN, D_MODEL), f32),
        "even_norm_post": 1.0 + 0.05 * jax.random.normal(ks[6], (N_EVEN, D_MODEL), f32),
        "odd_w_in": jax.random.normal(ks[7], (N_ODD, D_MODEL, ODD_IN), f32) * D_MODEL ** -0.5,
        "odd_forget_bias": jax.random.uniform(ks[8], (N_ODD, N_H), f32, 1.0, 5.0),
        "odd_lambda_q1": 0.1 * jax.random.normal(ks[9], (N_ODD, D_QK_DIM), f32),
        "odd_lambda_k1": 0.1 * jax.random.normal(ks[10], (N_ODD, D_QK_DIM), f32),
        "odd_lambda_q2": 0.1 * jax.random.normal(ks[11], (N_ODD, D_QK_DIM), f32),
        "odd_lambda_k2": 0.1 * jax.random.normal(ks[12], (N_ODD, D_QK_DIM), f32),
        "odd_subln": 1.0 + 0.05 * jax.random.normal(ks[13], (N_ODD, D_V_DIM), f32),
        "odd_w_out": jax.random.normal(ks[14], (N_ODD, 2 * BRANCH_WIDTH, D_MODEL), f32) * (2 * BRANCH_WIDTH) ** -0.5,
        "odd_norm_pre": 1.0 + 0.05 * jax.random.normal(ks[15], (N_ODD, D_MODEL), f32),
        "odd_norm_post": 1.0 + 0.05 * jax.random.normal(ks[16], (N_ODD, D_MODEL), f32),
    }


def reference(x, positions, even_w_in, even_rel_bias, even_w_out, even_norm_pre, even_norm_post,
              odd_w_in, odd_forget_bias, odd_lambda_q1, odd_lambda_k1, odd_lambda_q2, odd_lambda_k2,
              odd_subln, odd_w_out, odd_norm_pre, odd_norm_post):
    h = x
    for layer in range(DEPTH):
        i = layer // 2
        if layer % 2 == 0:
            u = rms_norm(h, even_norm_pre[i])
            y = even_mixer(u, even_w_in[i], even_rel_bias[i], even_w_out[i])
            h = h + rms_norm(y, even_norm_post[i])
        else:
            lambda_init = 0.8 - 0.6 * math.exp(-0.3 * layer)
            u = rms_norm(h, odd_norm_pre[i])
            y = odd_mixer(u, positions, odd_w_in[i], odd_forget_bias[i],
                          odd_lambda_q1[i], odd_lambda_k1[i], odd_lambda_q2[i], odd_lambda_k2[i],
                          odd_subln[i], odd_w_out[i], lambda_init)
            h = h + rms_norm(y, odd_norm_post[i])
    return h
```

```python
import functools
import math

import jax
import jax.numpy as jnp
from jax import lax
from jax.experimental import pallas as pl
from jax.experimental.pallas import tpu as pltpu

F32 = jnp.float32
BF16 = jnp.bfloat16

HEAD_DIM = 128
N_HEADS = 8
BRANCH = N_HEADS * HEAD_DIM
CHUNK = 64
CHUNK_SHIFT = 6
A_LEFT_CHUNKS = 8
A_PAD = A_LEFT_CHUNKS * CHUNK
REL_CLIP = 128
D_QK = 64
ROPE_THETA = 500000.0
ROPE_DIM = 16
EPS = 1e-6
NEG_INF = float("-inf")
VMEM_LIMIT_BYTES = 48 * 1024 * 1024

_NT = (((1,), (1,)), ((), ()))


def _params(*semantics):
    return pltpu.CompilerParams(dimension_semantics=semantics, vmem_limit_bytes=VMEM_LIMIT_BYTES)


def _silu(g):
    return g * jax.nn.sigmoid(g)


def _log_sigmoid(z):
    return jnp.minimum(z, 0.0) - jnp.log1p(jnp.exp(-jnp.abs(z)))


def _rmsnorm_kernel(x_ref, g_ref, o_ref):
    x = x_ref[...]
    ms = jnp.mean(x * x, axis=-1, keepdims=True)
    o_ref[...] = (x * lax.rsqrt(ms + EPS) * g_ref[...]).astype(o_ref.dtype)


def _rmsnorm(x, g):
    s, d = x.shape
    tm = min(512, s)
    return pl.pallas_call(
        _rmsnorm_kernel,
        out_shape=jax.ShapeDtypeStruct((s, d), BF16),
        grid=(s // tm,),
        in_specs=[pl.BlockSpec((tm, d), lambda i: (i, 0)),
                  pl.BlockSpec((1, d), lambda i: (0, 0))],
        out_specs=pl.BlockSpec((tm, d), lambda i: (i, 0)),
        compiler_params=_params("parallel"),
        name="rmsnorm",
    )(x, g.reshape(1, d))


def _matmul_kernel(x_ref, w_ref, o_ref):
    o_ref[...] = jnp.dot(x_ref[...], w_ref[...], preferred_element_type=F32).astype(o_ref.dtype)


def _matmul(x, w, out_dtype):
    m, k = x.shape
    n = w.shape[1]
    tm = min(512, m)
    tn = 1024 if n % 1024 == 0 else n
    return pl.pallas_call(
        _matmul_kernel,
        out_shape=jax.ShapeDtypeStruct((m, n), out_dtype),
        grid=(n // tn, m // tm),
        in_specs=[pl.BlockSpec((tm, k), lambda j, i: (i, 0)),
                  pl.BlockSpec((k, tn), lambda j, i: (0, j))],
        out_specs=pl.BlockSpec((tm, tn), lambda j, i: (i, j)),
        compiler_params=_params("parallel", "parallel"),
        name="in_proj",
    )(x, w)


def _mixer_a_kernel(q_ref, k_ref, v_ref, g_ref, b_ref, o_ref, *, tq, band, scale):
    i = pl.program_id(1)
    start = pl.multiple_of(i * tq, tq)
    k = k_ref[pl.ds(start, band), :]
    v = v_ref[pl.ds(start, band), :]
    s = lax.dot_general(q_ref[...], k, _NT, preferred_element_type=F32) * scale + b_ref[0]
    r = lax.broadcasted_iota(jnp.int32, (tq, band), 0)
    c = lax.broadcasted_iota(jnp.int32, (tq, band), 1)
    qc = r >> CHUNK_SHIFT
    kc = c >> CHUNK_SHIFT
    valid = (kc >= qc) & (kc <= qc + A_LEFT_CHUNKS) & (c + start >= A_PAD)
    s = jnp.where(valid, s, NEG_INF)
    m = jnp.max(s, axis=1, keepdims=True)
    p = jnp.exp(s - m)
    l = jnp.sum(p, axis=1, keepdims=True)
    o = jnp.dot(p.astype(BF16), v, preferred_element_type=F32) / l
    o_ref[...] = (o * _silu(g_ref[...])).astype(o_ref.dtype)


def _mixer_a(q_arr, q_col, kv_pad, k_col, v_col, g_arr, g_col, bias_tile, tq):
    s = q_arr.shape[0]
    sp = kv_pad.shape[0]
    band = A_PAD + tq
    kern = functools.partial(_mixer_a_kernel, tq=tq, band=band, scale=HEAD_DIM ** -0.5)
    return pl.pallas_call(
        kern,
        out_shape=jax.ShapeDtypeStruct((s, BRANCH), BF16),
        grid=(N_HEADS, s // tq),
        in_specs=[pl.BlockSpec((tq, HEAD_DIM), lambda h, i: (i, q_col + h)),
                  pl.BlockSpec((sp, HEAD_DIM), lambda h, i: (0, k_col + h)),
                  pl.BlockSpec((sp, HEAD_DIM), lambda h, i: (0, v_col + h)),
                  pl.BlockSpec((tq, HEAD_DIM), lambda h, i: (i, g_col + h)),
                  pl.BlockSpec((1, tq, band), lambda h, i: (h, 0, 0))],
        out_specs=pl.BlockSpec((tq, HEAD_DIM), lambda h, i: (i, h)),
        compiler_params=_params("parallel", "parallel"),
        name="mixer_a",
    )(q_arr, kv_pad, kv_pad, g_arr, bias_tile)


def _mixer_b_kernel(q_ref, k_ref, v_ref, g_ref, o_ref, acc_sc, run_sc, *, t, scale):
    i = pl.program_id(1)
    q = q_ref[...]
    rows = lax.broadcasted_iota(jnp.int32, (t, t), 0)
    cols = lax.broadcasted_iota(jnp.int32, (t, t), 1)
    later_keys = jnp.where(rows > cols, 1.0, 0.0).astype(BF16)
    strict = cols < rows
    acc_sc[...] = jnp.zeros_like(acc_sc)
    run_sc[...] = jnp.zeros_like(run_sc)

    def step(j, masked):
        off = pl.multiple_of(j * t, t)
        k = k_ref[pl.ds(off, t), :]
        v = v_ref[pl.ds(off, t), :]
        z = lax.dot_general(q, k, _NT, preferred_element_type=F32) * scale
        log_beta = _log_sigmoid(z)
        log_keep = log_beta - z
        if masked:
            log_keep = jnp.where(strict, log_keep, 0.0)
        hi = log_keep.astype(BF16)
        lo = (log_keep - hi.astype(F32)).astype(BF16)
        later = (jnp.dot(hi, later_keys, preferred_element_type=F32)
                 + jnp.dot(lo, later_keys, preferred_element_type=F32))
        w = jnp.exp(log_beta + later + run_sc[...])
        if masked:
            w = jnp.where(strict, w, 0.0)
        acc_sc[...] += jnp.dot(w.astype(BF16), v, preferred_element_type=F32)
        run_sc[...] += jnp.sum(log_keep, axis=1, keepdims=True)

    step(i, True)

    def body(n, carry):
        step(i - 1 - n, False)
        return carry

    lax.fori_loop(0, i, body, 0)
    o_ref[...] = (acc_sc[...] * _silu(g_ref[...])).astype(o_ref.dtype)


def _mixer_b(qkv, q_col, k_col, v_col, g_arr, g_col, t):
    s = qkv.shape[0]
    kern = functools.partial(_mixer_b_kernel, t=t, scale=HEAD_DIM ** -0.5)
    return pl.pallas_call(
        kern,
        out_shape=jax.ShapeDtypeStruct((s, BRANCH), BF16),
        grid=(N_HEADS, s // t),
        in_specs=[pl.BlockSpec((t, HEAD_DIM), lambda h, i: (i, q_col + h)),
                  pl.BlockSpec((s, HEAD_DIM), lambda h, i: (0, k_col + h)),
                  pl.BlockSpec((s, HEAD_DIM), lambda h, i: (0, v_col + h)),
                  pl.BlockSpec((t, HEAD_DIM), lambda h, i: (i, g_col + h))],
        out_specs=pl.BlockSpec((t, HEAD_DIM), lambda h, i: (i, h)),
        scratch_shapes=[pltpu.VMEM((t, HEAD_DIM), F32), pltpu.VMEM((t, 1), F32)],
        compiler_params=_params("parallel", "parallel"),
        name="mixer_b",
    )(qkv, qkv, qkv, g_arr)


def _forget_kernel(wf_ref, u_ref, b_ref, o_ref):
    f = lax.dot_general(wf_ref[...], u_ref[...], _NT, preferred_element_type=F32)
    o_ref[...] = _log_sigmoid(f[:N_HEADS] + b_ref[...])


def _forget_log_gates(wf_t, u, bias):
    s, d = u.shape
    tm = min(1024, s)
    return pl.pallas_call(
        _forget_kernel,
        out_shape=jax.ShapeDtypeStruct((N_HEADS, s), F32),
        grid=(s // tm,),
        in_specs=[pl.BlockSpec((2 * N_HEADS, d), lambda i: (0, 0)),
                  pl.BlockSpec((tm, d), lambda i: (i, 0)),
                  pl.BlockSpec((N_HEADS, 1), lambda i: (0, 0))],
        out_specs=pl.BlockSpec((N_HEADS, tm), lambda i: (0, i)),
        compiler_params=_params("parallel"),
        name="forget_gates",
    )(wf_t, u, bias.reshape(N_HEADS, 1))


def _cumsum_kernel(x_ref, o_ref, *, nb_shift):
    x = x_ref[...]
    nrows = x.shape[0]
    r = lax.broadcasted_iota(jnp.int32, (128, 128), 0)
    c = lax.broadcasted_iota(jnp.int32, (128, 128), 1)
    upto = jnp.where(r <= c, 1.0, 0.0).astype(F32)
    within = jnp.dot(x, upto, preferred_element_type=F32, precision=lax.Precision.HIGHEST)
    row_sum = jnp.broadcast_to(jnp.sum(x, axis=1, keepdims=True), (nrows, 128))
    rr = lax.broadcasted_iota(jnp.int32, (nrows, nrows), 0)
    cc = lax.broadcasted_iota(jnp.int32, (nrows, nrows), 1)
    earlier = jnp.where(((rr >> nb_shift) == (cc >> nb_shift)) & (cc < rr), 1.0, 0.0).astype(F32)
    o_ref[...] = within + jnp.dot(earlier, row_sum, preferred_element_type=F32,
                                  precision=lax.Precision.HIGHEST)


def _cumsum_seq(log_f):
    h, s = log_f.shape
    nb = s // 128
    nb_shift = nb.bit_length() - 1
    assert nb == 1 << nb_shift
    x = log_f.reshape(h * nb, 128)
    out = pl.pallas_call(
        functools.partial(_cumsum_kernel, nb_shift=nb_shift),
        out_shape=jax.ShapeDtypeStruct(x.shape, F32),
        compiler_params=pltpu.CompilerParams(vmem_limit_bytes=VMEM_LIMIT_BYTES),
        name="forget_cumsum",
    )(x)
    return out.reshape(h, s)


def _mixer_c_kernel(q_ref, k_ref, v_ref, g_ref, cq_ref, ck_ref, o_ref, m_sc, l_sc, acc_sc, *, t, scale):
    i = pl.program_id(1)
    q = q_ref[...]
    cq = cq_ref[0]
    c_first = cq[0:1, :]
    dq = cq - c_first
    rows = lax.broadcasted_iota(jnp.int32, (t, t), 0)
    cols = lax.broadcasted_iota(jnp.int32, (t, t), 1)
    causal = cols <= rows
    m_sc[...] = jnp.full_like(m_sc, NEG_INF)
    l_sc[...] = jnp.zeros_like(l_sc)
    acc_sc[...] = jnp.zeros_like(acc_sc)

    def step(j, masked):
        off = pl.multiple_of(j * t, t)
        k = k_ref[pl.ds(off, t), :]
        v = v_ref[pl.ds(off, t), :]
        dk = ck_ref[0, :, pl.ds(off, t)] - c_first
        s = lax.dot_general(q, k, _NT, preferred_element_type=F32) * scale + dq - dk
        if masked:
            s = jnp.where(causal, s, NEG_INF)
        m_old = m_sc[...]
        m_new = jnp.maximum(m_old, jnp.max(s, axis=1, keepdims=True))
        p = jnp.exp(s - m_new)
        alpha = jnp.exp(m_old - m_new)
        l_sc[...] = alpha * l_sc[...] + jnp.sum(p, axis=1, keepdims=True)
        acc_sc[...] = alpha * acc_sc[...] + jnp.dot(p.astype(BF16), v, preferred_element_type=F32)
        m_sc[...] = m_new

    def body(j, carry):
        step(j, False)
        return carry

    lax.fori_loop(0, i, body, 0)
    step(i, True)
    o = acc_sc[...] / l_sc[...]
    o_ref[...] = (o * _silu(g_ref[...])).astype(o_ref.dtype)


def _mixer_c(qkv, q_col, k_col, v_col, g_arr, g_col, cum_col, cum_row, t):
    s = qkv.shape[0]
    kern = functools.partial(_mixer_c_kernel, t=t, scale=HEAD_DIM ** -0.5)
    return pl.pallas_call(
        kern,
        out_shape=jax.ShapeDtypeStruct((s, BRANCH), BF16),
        grid=(N_HEADS, s // t),
        in_specs=[pl.BlockSpec((t, HEAD_DIM), lambda h, i: (i, q_col + h)),
                  pl.BlockSpec((s, HEAD_DIM), lambda h, i: (0, k_col + h)),
                  pl.BlockSpec((s, HEAD_DIM), lambda h, i: (0, v_col + h)),
                  pl.BlockSpec((t, HEAD_DIM), lambda h, i: (i, g_col + h)),
                  pl.BlockSpec((1, t, 1), lambda h, i: (h, i, 0)),
                  pl.BlockSpec((1, 1, s), lambda h, i: (h, 0, 0))],
        out_specs=pl.BlockSpec((t, HEAD_DIM), lambda h, i: (i, h)),
        scratch_shapes=[pltpu.VMEM((t, 1), F32), pltpu.VMEM((t, 1), F32),
                        pltpu.VMEM((t, HEAD_DIM), F32)],
        compiler_params=_params("parallel", "parallel"),
        name="mixer_c",
    )(qkv, qkv, qkv, g_arr, cum_col, cum_row)


def _rope_kernel(x_ref, pos_ref, invf_ref, o_ref, *, groups):
    ang = pos_ref[...].astype(F32) * invf_ref[...]
    cos = jnp.cos(ang)
    sin = jnp.sin(ang)
    lane = lax.broadcasted_iota(jnp.int32, ang.shape, 1)
    first_half = (lane & (D_QK - 1)) < ROPE_DIM // 2
    for gidx in range(groups):
        x = x_ref[:, gidx * 128:(gidx + 1) * 128]
        partner = jnp.where(first_half,
                            -pltpu.roll(x, 128 - ROPE_DIM // 2, 1),
                            pltpu.roll(x, ROPE_DIM // 2, 1))
        o_ref[:, gidx * 128:(gidx + 1) * 128] = (x * cos + partner * sin).astype(o_ref.dtype)


def _rope(x, positions, invf):
    s, n = x.shape
    tm = min(512, s)
    return pl.pallas_call(
        functools.partial(_rope_kernel, groups=n // 128),
        out_shape=jax.ShapeDtypeStruct((s, n), BF16),
        grid=(s // tm,),
        in_specs=[pl.BlockSpec((tm, n), lambda i: (i, 0)),
                  pl.BlockSpec((tm, 1), lambda i: (i, 0)),
                  pl.BlockSpec((1, 128), lambda i: (0, 0))],
        out_specs=pl.BlockSpec((tm, n), lambda i: (i, 0)),
        compiler_params=_params("parallel"),
        name="rope",
    )(x, positions, invf)


def _mixer_d_kernel(q_ref, k_ref, v_ref, g_ref, lam_ref, sub_ref, o_ref,
                    m1_sc, l1_sc, a1_sc, m2_sc, l2_sc, a2_sc, *, t, scale, lambda_init):
    i = pl.program_id(1)
    q = q_ref[...]
    lane = lax.broadcasted_iota(jnp.int32, q.shape, 1)
    zero = jnp.zeros_like(q)
    q1 = jnp.where(lane < D_QK, q, zero)
    q2 = jnp.where(lane >= D_QK, q, zero)
    rows = lax.broadcasted_iota(jnp.int32, (t, t), 0)
    cols = lax.broadcasted_iota(jnp.int32, (t, t), 1)
    chunk_causal = (cols >> CHUNK_SHIFT) <= (rows >> CHUNK_SHIFT)
    for m_sc, l_sc, a_sc in ((m1_sc, l1_sc, a1_sc), (m2_sc, l2_sc, a2_sc)):
        m_sc[...] = jnp.full_like(m_sc, NEG_INF)
        l_sc[...] = jnp.zeros_like(l_sc)
        a_sc[...] = jnp.zeros_like(a_sc)

    def step(j, masked):
        off = pl.multiple_of(j * t, t)
        k = k_ref[pl.ds(off, t), :]
        v = v_ref[pl.ds(off, t), :]
        for qm, m_sc, l_sc, a_sc in ((q1, m1_sc, l1_sc, a1_sc), (q2, m2_sc, l2_sc, a2_sc)):
            s = lax.dot_general(qm, k, _NT, preferred_element_type=F32) * scale
            if masked:
                s = jnp.where(chunk_causal, s, NEG_INF)
            m_old = m_sc[...]
            m_new = jnp.maximum(m_old, jnp.max(s, axis=1, keepdims=True))
            p = jnp.exp(s - m_new)
            alpha = jnp.exp(m_old - m_new)
            l_sc[...] = alpha * l_sc[...] + jnp.sum(p, axis=1, keepdims=True)
            a_sc[...] = alpha * a_sc[...] + jnp.dot(p.astype(BF16), v, preferred_element_type=F32)
            m_sc[...] = m_new

    def body(j, carry):
        step(j, False)
        return carry

    lax.fori_loop(0, i, body, 0)
    step(i, True)

    lv = lam_ref[...]
    lam = (jnp.exp(jnp.sum(lv[0:1] * lv[1:2], axis=1, keepdims=True))
           - jnp.exp(jnp.sum(lv[2:3] * lv[3:4], axis=1, keepdims=True)) + lambda_init)
    o = a1_sc[...] / l1_sc[...] - lam * (a2_sc[...] / l2_sc[...])
    ms = jnp.mean(o * o, axis=-1, keepdims=True)
    y = o * lax.rsqrt(ms + EPS) * sub_ref[...] * (1.0 - lambda_init)
    o_ref[...] = (y * _silu(g_ref[...])).astype(o_ref.dtype)


def _mixer_d(qk_arr, q_col, k_col, v_arr, v_col, g_arr, g_col, lam_vecs, subln, lambda_init, t):
    s = qk_arr.shape[0]
    kern = functools.partial(_mixer_d_kernel, t=t, scale=D_QK ** -0.5, lambda_init=lambda_init)
    row_stat = pltpu.VMEM((t, 1), F32)
    acc = pltpu.VMEM((t, HEAD_DIM), F32)
    return pl.pallas_call(
        kern,
        out_shape=jax.ShapeDtypeStruct((s, BRANCH), BF16),
        grid=(N_HEADS, s // t),
        in_specs=[pl.BlockSpec((t, HEAD_DIM), lambda h, i: (i, q_col + h)),
                  pl.BlockSpec((s, HEAD_DIM), lambda h, i: (0, k_col + h)),
                  pl.BlockSpec((s, HEAD_DIM), lambda h, i: (0, v_col + h)),
                  pl.BlockSpec((t, HEAD_DIM), lambda h, i: (i, g_col + h)),
                  pl.BlockSpec((4, D_QK), lambda h, i: (0, 0)),
                  pl.BlockSpec((1, HEAD_DIM), lambda h, i: (0, 0))],
        out_specs=pl.BlockSpec((t, HEAD_DIM), lambda h, i: (i, h)),
        scratch_shapes=[row_stat, row_stat, acc, row_stat, row_stat, acc],
        compiler_params=_params("parallel", "parallel"),
        name="mixer_d",
    )(qk_arr, qk_arr, v_arr, g_arr, lam_vecs, subln.reshape(1, HEAD_DIM))


def _out_proj_kernel(ya_ref, yb_ref, w_ref, h_ref, g_ref, o_ref):
    half = ya_ref.shape[1]
    y = (jnp.dot(ya_ref[...], w_ref[:half, :], preferred_element_type=F32)
         + jnp.dot(yb_ref[...], w_ref[half:, :], preferred_element_type=F32))
    ms = jnp.mean(y * y, axis=-1, keepdims=True)
    o_ref[...] = h_ref[...] + y * lax.rsqrt(ms + EPS) * g_ref[...]


def _out_proj(ya, yb, w, h, g):
    s, d = h.shape
    tm = min(256, s)
    return pl.pallas_call(
        _out_proj_kernel,
        out_shape=jax.ShapeDtypeStruct((s, d), F32),
        grid=(s // tm,),
        in_specs=[pl.BlockSpec((tm, BRANCH), lambda i: (i, 0)),
                  pl.BlockSpec((tm, BRANCH), lambda i: (i, 0)),
                  pl.BlockSpec((2 * BRANCH, d), lambda i: (0, 0)),
                  pl.BlockSpec((tm, d), lambda i: (i, 0)),
                  pl.BlockSpec((1, d), lambda i: (0, 0))],
        out_specs=pl.BlockSpec((tm, d), lambda i: (i, 0)),
        compiler_params=_params("parallel"),
        name="out_proj",
    )(ya, yb, w, h, g.reshape(1, d))


def _rel_bias_tile(rel_bias, tq):
    band = A_PAD + tq
    dist = jnp.arange(tq)[:, None] + A_PAD - jnp.arange(band)[None, :]
    return rel_bias.astype(F32)[:, jnp.clip(dist, -REL_CLIP, REL_CLIP) + REL_CLIP]


def _even_layer(h, w_in, rel_bias, w_out, norm_pre, norm_post):
    s = h.shape[0]
    u = _rmsnorm(h, norm_pre)
    b = BRANCH
    w_qkv = jnp.concatenate([w_in[:, :3 * b], w_in[:, 4 * b:7 * b]], axis=1).astype(BF16)
    w_g = jnp.concatenate([w_in[:, 3 * b:4 * b], w_in[:, 7 * b:]], axis=1).astype(BF16)
    qkv = _matmul(u, w_qkv, BF16)
    gates = _matmul(u, w_g, F32)
    tq_a = min(256, s)
    kv_pad = jnp.pad(qkv[:, b:3 * b], ((A_PAD, 0), (0, 0)))
    y_a = _mixer_a(qkv, 0, kv_pad, 0, N_HEADS, gates, 0, _rel_bias_tile(rel_bias, tq_a), tq_a)
    y_b = _mixer_b(qkv, 3 * N_HEADS, 4 * N_HEADS, 5 * N_HEADS, gates, N_HEADS, min(256, s))
    return _out_proj(y_a, y_b, w_out.astype(BF16), h, norm_post)


def _odd_layer(h, positions, w_in, forget_bias, lq1, lk1, lq2, lk2, subln, w_out, norm_pre, norm_post,
               lambda_init):
    s, d = h.shape
    u = _rmsnorm(h, norm_pre)
    b = BRANCH
    f0 = 4 * b
    d0 = f0 + N_HEADS
    v0 = d0 + 4 * N_HEADS * D_QK
    w_qkv = jnp.concatenate([w_in[:, :3 * b], w_in[:, v0:v0 + b]], axis=1).astype(BF16)
    w_g = jnp.concatenate([w_in[:, 3 * b:4 * b], w_in[:, v0 + b:]], axis=1).astype(BF16)
    wd = w_in[:, d0:v0].reshape(d, 4, N_HEADS, D_QK)
    w_qk = jnp.concatenate([
        jnp.concatenate([wd[:, 0], wd[:, 1]], axis=2).reshape(d, b),
        jnp.concatenate([wd[:, 2], wd[:, 3]], axis=2).reshape(d, b),
    ], axis=1).astype(BF16)
    wf_t = jnp.pad(w_in[:, f0:d0].T, ((0, N_HEADS), (0, 0))).astype(BF16)

    qkv = _matmul(u, w_qkv, BF16)
    gates = _matmul(u, w_g, F32)
    qk_d = _matmul(u, w_qk, F32)

    cum = _cumsum_seq(_forget_log_gates(wf_t, u, forget_bias))
    t = min(512, s)
    y_c = _mixer_c(qkv, 0, N_HEADS, 2 * N_HEADS, gates, 0,
                   cum.reshape(N_HEADS, s, 1), cum.reshape(N_HEADS, 1, s), t)

    lane = jnp.arange(128) % D_QK
    inv_freq = ROPE_THETA ** (-jnp.arange(0, ROPE_DIM, 2, dtype=F32) / ROPE_DIM)
    invf = jnp.where(lane < ROPE_DIM, inv_freq[lane % (ROPE_DIM // 2)], 0.0).astype(F32).reshape(1, 128)
    qk_rot = _rope(qk_d, positions.reshape(s, 1), invf)
    lam_vecs = jnp.stack([lq1, lk1, lq2, lk2]).astype(F32)
    y_d = _mixer_d(qk_rot, 0, N_HEADS, qkv, 3 * N_HEADS, gates, N_HEADS, lam_vecs, subln,
                   lambda_init, t)
    return _out_proj(y_c, y_d, w_out.astype(BF16), h, norm_post)


def kernel(x, positions, even_w_in, even_rel_bias, even_w_out, even_norm_pre, even_norm_post,
           odd_w_in, odd_forget_bias, odd_lambda_q1, odd_lambda_k1, odd_lambda_q2, odd_lambda_k2,
           odd_subln, odd_w_out, odd_norm_pre, odd_norm_post):
    assert x.shape[0] == 1
    h = x[0]
    depth = even_w_in.shape[0] + odd_w_in.shape[0]
    for layer in range(depth):
        i = layer // 2
        if layer % 2 == 0:
            h = _even_layer(h, even_w_in[i], even_rel_bias[i], even_w_out[i],
                            even_norm_pre[i], even_norm_post[i])
        else:
            lambda_init = 0.8 - 0.6 * math.exp(-0.3 * layer)
            h = _odd_layer(h, positions, odd_w_in[i], odd_forget_bias[i],
                           odd_lambda_q1[i], odd_lambda_k1[i], odd_lambda_q2[i], odd_lambda_k2[i],
                           odd_subln[i], odd_w_out[i], odd_norm_pre[i], odd_norm_post[i], lambda_init)
    return h[None]
```

```python
import functools
import math

import jax
import jax.numpy as jnp
from jax import lax
from jax.experimental import pallas as pl
from jax.experimental.pallas import tpu as pltpu

F32 = jnp.float32
BF16 = jnp.bfloat16

HEAD_DIM = 128
N_HEADS = 8
BRANCH = N_HEADS * HEAD_DIM
CHUNK = 64
CHUNK_SHIFT = 6
A_LEFT_CHUNKS = 8
A_PAD = A_LEFT_CHUNKS * CHUNK
REL_CLIP = 128
D_QK = 64
ROPE_THETA = 500000.0
ROPE_DIM = 16
EPS = 1e-6
NEG_INF = float("-inf")
LOG2E = math.log2(math.e)
BF16_SUBLANES = 16
VMEM_LIMIT_BYTES = 48 * 1024 * 1024

_NT = (((1,), (1,)), ((), ()))


def _params(*semantics):
    return pltpu.CompilerParams(dimension_semantics=semantics, vmem_limit_bytes=VMEM_LIMIT_BYTES)


def _silu(g):
    return g * jax.nn.sigmoid(g)


def _log_sigmoid(z):
    return jnp.minimum(z, 0.0) - jnp.log1p(jnp.exp(-jnp.abs(z)))


def _rmsnorm_kernel(x_ref, g_ref, o_ref):
    x = x_ref[...]
    ms = jnp.mean(x * x, axis=-1, keepdims=True)
    o_ref[...] = (x * lax.rsqrt(ms + EPS) * g_ref[...]).astype(o_ref.dtype)


def _rmsnorm(x, g):
    s, d = x.shape
    tm = min(512, s)
    return pl.pallas_call(
        _rmsnorm_kernel,
        out_shape=jax.ShapeDtypeStruct((s, d), BF16),
        grid=(s // tm,),
        in_specs=[pl.BlockSpec((tm, d), lambda i: (i, 0)),
                  pl.BlockSpec((1, d), lambda i: (0, 0))],
        out_specs=pl.BlockSpec((tm, d), lambda i: (i, 0)),
        compiler_params=_params("parallel"),
        name="rmsnorm",
    )(x, g.reshape(1, d))


def _matmul_kernel(x_ref, w_ref, o_ref):
    o_ref[...] = jnp.dot(x_ref[...], w_ref[...], preferred_element_type=F32).astype(o_ref.dtype)


def _matmul(x, w, out_dtype):
    m, k = x.shape
    n = w.shape[1]
    tm = min(512, m)
    tn = 1024 if n % 1024 == 0 else n
    return pl.pallas_call(
        _matmul_kernel,
        out_shape=jax.ShapeDtypeStruct((m, n), out_dtype),
        grid=(n // tn, m // tm),
        in_specs=[pl.BlockSpec((tm, k), lambda j, i: (i, 0)),
                  pl.BlockSpec((k, tn), lambda j, i: (0, j))],
        out_specs=pl.BlockSpec((tm, tn), lambda j, i: (i, j)),
        compiler_params=_params("parallel", "parallel"),
        name="in_proj",
    )(x, w)


def _matmul_t_kernel(w_ref, x_ref, o_ref):
    o_ref[...] = lax.dot_general(w_ref[...], x_ref[...], _NT,
                                 preferred_element_type=F32).astype(o_ref.dtype)


def _matmul_t(w_t, x, out_dtype):
    n, k = w_t.shape
    m = x.shape[0]
    tm = min(512, m)
    tn = 1024 if n % 1024 == 0 else n
    return pl.pallas_call(
        _matmul_t_kernel,
        out_shape=jax.ShapeDtypeStruct((n, m), out_dtype),
        grid=(n // tn, m // tm),
        in_specs=[pl.BlockSpec((tn, k), lambda j, i: (j, 0)),
                  pl.BlockSpec((tm, k), lambda j, i: (i, 0))],
        out_specs=pl.BlockSpec((tn, tm), lambda j, i: (j, i)),
        compiler_params=_params("parallel", "parallel"),
        name="in_proj_t",
    )(w_t, x)


def _mixer_a_kernel(q_ref, k_ref, v_ref, g_ref, b_ref, o_ref, *, tq, band, scale):
    i = pl.program_id(1)
    start = pl.multiple_of(i * tq, tq)
    k = k_ref[pl.ds(start, band), :]
    v = v_ref[pl.ds(start, band), :]
    s = lax.dot_general(q_ref[...], k, _NT, preferred_element_type=F32) * scale + b_ref[0]
    r = lax.broadcasted_iota(jnp.int32, (tq, band), 0)
    c = lax.broadcasted_iota(jnp.int32, (tq, band), 1)
    qc = r >> CHUNK_SHIFT
    kc = c >> CHUNK_SHIFT
    valid = (kc >= qc) & (kc <= qc + A_LEFT_CHUNKS) & (c + start >= A_PAD)
    s = jnp.where(valid, s, NEG_INF)
    m = jnp.max(s, axis=1, keepdims=True)
    p = jnp.exp(s - m)
    l = jnp.sum(p, axis=1, keepdims=True)
    o = jnp.dot(p.astype(BF16), v, preferred_element_type=F32) / l
    o_ref[...] = (o * _silu(g_ref[...])).astype(o_ref.dtype)


def _mixer_a(q_arr, q_col, kv_pad, k_col, v_col, g_arr, g_col, bias_tile, tq):
    s = q_arr.shape[0]
    sp = kv_pad.shape[0]
    band = A_PAD + tq
    kern = functools.partial(_mixer_a_kernel, tq=tq, band=band, scale=HEAD_DIM ** -0.5)
    return pl.pallas_call(
        kern,
        out_shape=jax.ShapeDtypeStruct((s, BRANCH), BF16),
        grid=(N_HEADS, s // tq),
        in_specs=[pl.BlockSpec((tq, HEAD_DIM), lambda h, i: (i, q_col + h)),
                  pl.BlockSpec((sp, HEAD_DIM), lambda h, i: (0, k_col + h)),
                  pl.BlockSpec((sp, HEAD_DIM), lambda h, i: (0, v_col + h)),
                  pl.BlockSpec((tq, HEAD_DIM), lambda h, i: (i, g_col + h)),
                  pl.BlockSpec((1, tq, band), lambda h, i: (h, 0, 0))],
        out_specs=pl.BlockSpec((tq, HEAD_DIM), lambda h, i: (i, h)),
        compiler_params=_params("parallel", "parallel"),
        name="mixer_a",
    )(q_arr, kv_pad, kv_pad, g_arr, bias_tile)


def _key_query_index(tk, tq, key_off):
    key = lax.broadcasted_iota(jnp.int32, (tk, tq), 0) + key_off
    query = lax.broadcasted_iota(jnp.int32, (tk, tq), 1)
    return key, query


def _softmax_step(s, masked_out, v_aug, m_sc, acc_sc):
    if masked_out is not None:
        s = jnp.where(masked_out, NEG_INF, s)
    m_old = m_sc[...]
    m_new = jnp.maximum(m_old, jnp.max(s, axis=0, keepdims=True))
    p = jnp.exp2(s - m_new).astype(BF16)
    alpha = jnp.exp2(m_old - m_new)
    acc_sc[...] = alpha * acc_sc[...] + jnp.dot(v_aug, p, preferred_element_type=F32)
    m_sc[...] = m_new


def _init_softmax_state(m_sc, acc_sc):
    m_sc[...] = jnp.full_like(m_sc, NEG_INF)
    acc_sc[...] = jnp.zeros_like(acc_sc)


def _normalised_rows(acc_sc):
    acc = acc_sc[...]
    return (acc[:HEAD_DIM] / acc[HEAD_DIM:HEAD_DIM + 1]).T


def _ones_rows(width):
    return jnp.ones((BF16_SUBLANES, width), BF16)


def _mixer_b_kernel(q_ref, k_ref, vt_ref, g_ref, o_ref, acc_sc, run_sc, *, tq, tk):
    i = pl.program_id(1)
    q = q_ref[...]
    per_q = tq // tk
    kk = lax.broadcasted_iota(jnp.int32, (tk + BF16_SUBLANES, tk), 0)
    jj = lax.broadcasted_iota(jnp.int32, (tk + BF16_SUBLANES, tk), 1)
    suffix = jnp.where((jj > kk) | (kk >= tk), 1.0, 0.0).astype(BF16)
    acc_sc[...] = jnp.zeros_like(acc_sc)
    run_sc[...] = jnp.zeros_like(run_sc)

    def step(j, key_off):
        off = pl.multiple_of(j * tk, tk)
        z = lax.dot_general(k_ref[pl.ds(off, tk), :], q, _NT, preferred_element_type=F32)
        log_beta = jnp.minimum(z, 0.0) - jnp.log2(1.0 + jnp.exp2(-jnp.abs(z)))
        log_keep = log_beta - z
        if key_off is not None:
            key, query = _key_query_index(tk, tq, key_off)
            strict = key < query
            log_keep = jnp.where(strict, log_keep, 0.0)
        hi = log_keep.astype(BF16)
        lo = (log_keep - hi.astype(F32)).astype(BF16)
        later = (jnp.dot(suffix, hi, preferred_element_type=F32)
                 + jnp.dot(suffix, lo, preferred_element_type=F32))
        w = jnp.exp2(log_beta + later[:tk] + run_sc[...])
        if key_off is not None:
            w = jnp.where(strict, w, 0.0)
        acc_sc[...] += jnp.dot(vt_ref[:, pl.ds(off, tk)], w.astype(BF16), preferred_element_type=F32)
        run_sc[...] += later[tk:tk + 1]

    for d in range(per_q):
        blk = per_q - 1 - d
        step(i * per_q + blk, blk * tk)

    def body(n, carry):
        step(i * per_q - 1 - n, None)
        return carry

    lax.fori_loop(0, i * per_q, body, 0)
    o_ref[...] = (acc_sc[...].T * _silu(g_ref[...])).astype(o_ref.dtype)


def _mixer_b(qk, q_col, k_col, vt, v_row, g_arr, g_col, tq, tk):
    s = qk.shape[0]
    kern = functools.partial(_mixer_b_kernel, tq=tq, tk=tk)
    return pl.pallas_call(
        kern,
        out_shape=jax.ShapeDtypeStruct((s, BRANCH), BF16),
        grid=(N_HEADS, s // tq),
        in_specs=[pl.BlockSpec((tq, HEAD_DIM), lambda h, i: (i, q_col + h)),
                  pl.BlockSpec((s, HEAD_DIM), lambda h, i: (0, k_col + h)),
                  pl.BlockSpec((HEAD_DIM, s), lambda h, i: (v_row + h, 0)),
                  pl.BlockSpec((tq, HEAD_DIM), lambda h, i: (i, g_col + h))],
        out_specs=pl.BlockSpec((tq, HEAD_DIM), lambda h, i: (i, h)),
        scratch_shapes=[pltpu.VMEM((HEAD_DIM, tq), F32), pltpu.VMEM((1, tq), F32)],
        compiler_params=_params("parallel", "parallel"),
        name="mixer_b",
    )(qk, qk, vt, g_arr)


def _forget_kernel(wf_ref, u_ref, b_ref, o_ref):
    f = lax.dot_general(wf_ref[...], u_ref[...], _NT, preferred_element_type=F32)
    o_ref[...] = _log_sigmoid(f[:N_HEADS] + b_ref[...])


def _forget_log_gates(wf_t, u, bias):
    s, d = u.shape
    tm = min(1024, s)
    return pl.pallas_call(
        _forget_kernel,
        out_shape=jax.ShapeDtypeStruct((N_HEADS, s), F32),
        grid=(s // tm,),
        in_specs=[pl.BlockSpec((2 * N_HEADS, d), lambda i: (0, 0)),
                  pl.BlockSpec((tm, d), lambda i: (i, 0)),
                  pl.BlockSpec((N_HEADS, 1), lambda i: (0, 0))],
        out_specs=pl.BlockSpec((N_HEADS, tm), lambda i: (0, i)),
        compiler_params=_params("parallel"),
        name="forget_gates",
    )(wf_t, u, bias.reshape(N_HEADS, 1))


def _cumsum_kernel(x_ref, o_ref, *, nb_shift):
    x = x_ref[...]
    nrows = x.shape[0]
    r = lax.broadcasted_iota(jnp.int32, (128, 128), 0)
    c = lax.broadcasted_iota(jnp.int32, (128, 128), 1)
    upto = jnp.where(r <= c, 1.0, 0.0).astype(F32)
    within = jnp.dot(x, upto, preferred_element_type=F32, precision=lax.Precision.HIGHEST)
    row_sum = jnp.broadcast_to(jnp.sum(x, axis=1, keepdims=True), (nrows, 128))
    rr = lax.broadcasted_iota(jnp.int32, (nrows, nrows), 0)
    cc = lax.broadcasted_iota(jnp.int32, (nrows, nrows), 1)
    earlier = jnp.where(((rr >> nb_shift) == (cc >> nb_shift)) & (cc < rr), 1.0, 0.0).astype(F32)
    o_ref[...] = within + jnp.dot(earlier, row_sum, preferred_element_type=F32,
                                  precision=lax.Precision.HIGHEST)


def _cumsum_seq(log_f):
    h, s = log_f.shape
    nb = s // 128
    nb_shift = nb.bit_length() - 1
    assert nb == 1 << nb_shift
    x = log_f.reshape(h * nb, 128)
    out = pl.pallas_call(
        functools.partial(_cumsum_kernel, nb_shift=nb_shift),
        out_shape=jax.ShapeDtypeStruct(x.shape, F32),
        compiler_params=pltpu.CompilerParams(vmem_limit_bytes=VMEM_LIMIT_BYTES),
        name="forget_cumsum",
    )(x)
    return out.reshape(h, s)


def _decay_features_kernel(c_ref, qx_ref, kx_ref):
    c = c_ref[0] * LOG2E
    hi = c.astype(BF16)
    r1 = c - hi.astype(F32)
    mid = r1.astype(BF16)
    lo = (r1 - mid.astype(F32)).astype(BF16)
    shape = qx_ref.shape
    lane = lax.broadcasted_iota(jnp.int32, shape, 1)
    parts = [jnp.broadcast_to(t.astype(F32), shape) for t in (hi, mid, lo)]
    one = jnp.ones(shape, F32)
    zero = jnp.zeros(shape, F32)
    qx = jnp.where(lane == 0, parts[0], jnp.where(lane == 1, parts[1], jnp.where(lane == 2, parts[2],
                   jnp.where(lane < 6, one, zero))))
    kx = jnp.where(lane < 3, one, jnp.where(lane == 3, -parts[0], jnp.where(lane == 4, -parts[1],
                   jnp.where(lane == 5, -parts[2], zero))))
    qx_ref[...] = qx.astype(BF16)
    kx_ref[...] = kx.astype(BF16)


def _decay_features(cum_col):
    h, s, _ = cum_col.shape
    tm = min(1024, s)
    out = jax.ShapeDtypeStruct((s, h * HEAD_DIM), BF16)
    spec = pl.BlockSpec((tm, HEAD_DIM), lambda hh, i: (i, hh))
    return pl.pallas_call(
        _decay_features_kernel,
        out_shape=(out, out),
        grid=(h, s // tm),
        in_specs=[pl.BlockSpec((1, tm, 1), lambda hh, i: (hh, i, 0))],
        out_specs=(spec, spec),
        compiler_params=_params("parallel", "parallel"),
        name="decay_features",
    )(cum_col)


def _mixer_c_kernel(q_ref, qx_ref, k_ref, kx_ref, vt_ref, g_ref, o_ref, m_sc, acc_sc, *, t):
    i = pl.program_id(1)
    q = jnp.concatenate([q_ref[...], qx_ref[...]], axis=1)
    ones = _ones_rows(t)
    _init_softmax_state(m_sc, acc_sc)

    def step(j, on_diagonal):
        off = pl.multiple_of(j * t, t)
        k = jnp.concatenate([k_ref[pl.ds(off, t), :], kx_ref[pl.ds(off, t), :]], axis=1)
        s = lax.dot_general(k, q, _NT, preferred_element_type=F32)
        masked_out = None
        if on_diagonal:
            key, query = _key_query_index(t, t, 0)
            masked_out = key > query
        v_aug = jnp.concatenate([vt_ref[:, pl.ds(off, t)], ones], axis=0)
        _softmax_step(s, masked_out, v_aug, m_sc, acc_sc)

    def body(j, carry):
        step(j, False)
        return carry

    lax.fori_loop(0, i, body, 0)
    step(i, True)
    o_ref[...] = (_normalised_rows(acc_sc) * _silu(g_ref[...])).astype(o_ref.dtype)


def _mixer_c(qk, q_col, k_col, qx, kx, vt, v_row, g_arr, g_col, t):
    s = qk.shape[0]
    return pl.pallas_call(
        functools.partial(_mixer_c_kernel, t=t),
        out_shape=jax.ShapeDtypeStruct((s, BRANCH), BF16),
        grid=(N_HEADS, s // t),
        in_specs=[pl.BlockSpec((t, HEAD_DIM), lambda h, i: (i, q_col + h)),
                  pl.BlockSpec((t, HEAD_DIM), lambda h, i: (i, h)),
                  pl.BlockSpec((s, HEAD_DIM), lambda h, i: (0, k_col + h)),
                  pl.BlockSpec((s, HEAD_DIM), lambda h, i: (0, h)),
                  pl.BlockSpec((HEAD_DIM, s), lambda h, i: (v_row + h, 0)),
                  pl.BlockSpec((t, HEAD_DIM), lambda h, i: (i, g_col + h))],
        out_specs=pl.BlockSpec((t, HEAD_DIM), lambda h, i: (i, h)),
        scratch_shapes=[pltpu.VMEM((1, t), F32),
                        pltpu.VMEM((HEAD_DIM + BF16_SUBLANES, t), F32)],
        compiler_params=_params("parallel", "parallel"),
        name="mixer_c",
    )(qk, qx, qk, kx, vt, g_arr)


def _rope_kernel(x_ref, pos_ref, invf_ref, o_ref, *, groups):
    ang = pos_ref[...].astype(F32) * invf_ref[...]
    cos = jnp.cos(ang)
    sin = jnp.sin(ang)
    lane = lax.broadcasted_iota(jnp.int32, ang.shape, 1)
    first_half = (lane & (D_QK - 1)) < ROPE_DIM // 2
    for gidx in range(groups):
        x = x_ref[:, gidx * 128:(gidx + 1) * 128]
        partner = jnp.where(first_half,
                            -pltpu.roll(x, 128 - ROPE_DIM // 2, 1),
                            pltpu.roll(x, ROPE_DIM // 2, 1))
        o_ref[:, gidx * 128:(gidx + 1) * 128] = (x * cos + partner * sin).astype(o_ref.dtype)


def _rope(x, positions, invf):
    s, n = x.shape
    tm = min(512, s)
    return pl.pallas_call(
        functools.partial(_rope_kernel, groups=n // 128),
        out_shape=jax.ShapeDtypeStruct((s, n), BF16),
        grid=(s // tm,),
        in_specs=[pl.BlockSpec((tm, n), lambda i: (i, 0)),
                  pl.BlockSpec((tm, 1), lambda i: (i, 0)),
                  pl.BlockSpec((1, 128), lambda i: (0, 0))],
        out_specs=pl.BlockSpec((tm, n), lambda i: (i, 0)),
        compiler_params=_params("parallel"),
        name="rope",
    )(x, positions, invf)


def _mixer_d_kernel(q_ref, k_ref, vt_ref, g_ref, lam_ref, sub_ref, o_ref,
                    m1_sc, a1_sc, m2_sc, a2_sc, *, t, lambda_init):
    i = pl.program_id(1)
    q = q_ref[...]
    lane = lax.broadcasted_iota(jnp.int32, q.shape, 1)
    zero = jnp.zeros_like(q)
    q1 = jnp.where(lane < D_QK, q, zero)
    q2 = jnp.where(lane >= D_QK, q, zero)
    ones = _ones_rows(t)
    _init_softmax_state(m1_sc, a1_sc)
    _init_softmax_state(m2_sc, a2_sc)

    def step(j, on_diagonal):
        off = pl.multiple_of(j * t, t)
        k = k_ref[pl.ds(off, t), :]
        v_aug = jnp.concatenate([vt_ref[:, pl.ds(off, t)], ones], axis=0)
        masked_out = None
        if on_diagonal:
            key, query = _key_query_index(t, t, 0)
            masked_out = (key >> CHUNK_SHIFT) > (query >> CHUNK_SHIFT)
        for qm, m_sc, a_sc in ((q1, m1_sc, a1_sc), (q2, m2_sc, a2_sc)):
            s = lax.dot_general(k, qm, _NT, preferred_element_type=F32)
            _softmax_step(s, masked_out, v_aug, m_sc, a_sc)

    def body(j, carry):
        step(j, False)
        return carry

    lax.fori_loop(0, i, body, 0)
    step(i, True)

    lv = lam_ref[...]
    lam = (jnp.exp(jnp.sum(lv[0:1] * lv[1:2], axis=1, keepdims=True))
           - jnp.exp(jnp.sum(lv[2:3] * lv[3:4], axis=1, keepdims=True)) + lambda_init)
    o = _normalised_rows(a1_sc) - lam * _normalised_rows(a2_sc)
    ms = jnp.mean(o * o, axis=-1, keepdims=True)
    y = o * lax.rsqrt(ms + EPS) * sub_ref[...] * (1.0 - lambda_init)
    o_ref[...] = (y * _silu(g_ref[...])).astype(o_ref.dtype)


def _mixer_d(qk_arr, q_col, k_col, vt, v_row, g_arr, g_col, lam_vecs, subln, lambda_init, t):
    s = qk_arr.shape[0]
    kern = functools.partial(_mixer_d_kernel, t=t, lambda_init=lambda_init)
    row_stat = pltpu.VMEM((1, t), F32)
    acc = pltpu.VMEM((HEAD_DIM + BF16_SUBLANES, t), F32)
    return pl.pallas_call(
        kern,
        out_shape=jax.ShapeDtypeStruct((s, BRANCH), BF16),
        grid=(N_HEADS, s // t),
        in_specs=[pl.BlockSpec((t, HEAD_DIM), lambda h, i: (i, q_col + h)),
                  pl.BlockSpec((s, HEAD_DIM), lambda h, i: (0, k_col + h)),
                  pl.BlockSpec((HEAD_DIM, s), lambda h, i: (v_row + h, 0)),
                  pl.BlockSpec((t, HEAD_DIM), lambda h, i: (i, g_col + h)),
                  pl.BlockSpec((4, D_QK), lambda h, i: (0, 0)),
                  pl.BlockSpec((1, HEAD_DIM), lambda h, i: (0, 0))],
        out_specs=pl.BlockSpec((t, HEAD_DIM), lambda h, i: (i, h)),
        scratch_shapes=[row_stat, acc, row_stat, acc],
        compiler_params=_params("parallel", "parallel"),
        name="mixer_d",
    )(qk_arr, qk_arr, vt, g_arr, lam_vecs, subln.reshape(1, HEAD_DIM))


def _out_proj_kernel(ya_ref, yb_ref, w_ref, h_ref, g_ref, o_ref):
    half = ya_ref.shape[1]
    y = (jnp.dot(ya_ref[...], w_ref[:half, :], preferred_element_type=F32)
         + jnp.dot(yb_ref[...], w_ref[half:, :], preferred_element_type=F32))
    ms = jnp.mean(y * y, axis=-1, keepdims=True)
    o_ref[...] = h_ref[...] + y * lax.rsqrt(ms + EPS) * g_ref[...]


def _out_proj(ya, yb, w, h, g):
    s, d = h.shape
    tm = min(256, s)
    return pl.pallas_call(
        _out_proj_kernel,
        out_shape=jax.ShapeDtypeStruct((s, d), F32),
        grid=(s // tm,),
        in_specs=[pl.BlockSpec((tm, BRANCH), lambda i: (i, 0)),
                  pl.BlockSpec((tm, BRANCH), lambda i: (i, 0)),
                  pl.BlockSpec((2 * BRANCH, d), lambda i: (0, 0)),
                  pl.BlockSpec((tm, d), lambda i: (i, 0)),
                  pl.BlockSpec((1, d), lambda i: (0, 0))],
        out_specs=pl.BlockSpec((tm, d), lambda i: (i, 0)),
        compiler_params=_params("parallel"),
        name="out_proj",
    )(ya, yb, w, h, g.reshape(1, d))


def _rel_bias_tile(rel_bias, tq):
    band = A_PAD + tq
    dist = jnp.arange(tq)[:, None] + A_PAD - jnp.arange(band)[None, :]
    return rel_bias.astype(F32)[:, jnp.clip(dist, -REL_CLIP, REL_CLIP) + REL_CLIP]


def _even_layer(h, w_in, rel_bias, w_out, norm_pre, norm_post):
    s = h.shape[0]
    u = _rmsnorm(h, norm_pre)
    b = BRANCH
    log2_scale = HEAD_DIM ** -0.5 * LOG2E
    w_rows = jnp.concatenate([w_in[:, :3 * b], w_in[:, 4 * b:5 * b] * log2_scale, w_in[:, 5 * b:6 * b]],
                             axis=1).astype(BF16)
    w_g = jnp.concatenate([w_in[:, 3 * b:4 * b], w_in[:, 7 * b:]], axis=1).astype(BF16)
    rows = _matmul(u, w_rows, BF16)
    gates = _matmul(u, w_g, F32)
    vt_b = _matmul_t(w_in[:, 6 * b:7 * b].T.astype(BF16), u, BF16)
    tq_a = min(256, s)
    kv_pad = jnp.pad(rows[:, b:3 * b], ((A_PAD, 0), (0, 0)))
    y_a = _mixer_a(rows, 0, kv_pad, 0, N_HEADS, gates, 0, _rel_bias_tile(rel_bias, tq_a), tq_a)
    y_b = _mixer_b(rows, 3 * N_HEADS, 4 * N_HEADS, vt_b, 0, gates, N_HEADS, min(512, s), min(256, s))
    return _out_proj(y_a, y_b, w_out.astype(BF16), h, norm_post)


def _odd_layer(h, positions, w_in, forget_bias, lq1, lk1, lq2, lk2, subln, w_out, norm_pre, norm_post,
               lambda_init):
    s, d = h.shape
    u = _rmsnorm(h, norm_pre)
    b = BRANCH
    f0 = 4 * b
    d0 = f0 + N_HEADS
    v0 = d0 + 4 * N_HEADS * D_QK
    w_rows = jnp.concatenate([w_in[:, :b] * (HEAD_DIM ** -0.5 * LOG2E), w_in[:, b:2 * b]],
                             axis=1).astype(BF16)
    w_g = jnp.concatenate([w_in[:, 3 * b:4 * b], w_in[:, v0 + b:]], axis=1).astype(BF16)
    w_vt = jnp.concatenate([w_in[:, 2 * b:3 * b], w_in[:, v0:v0 + b]], axis=1).T.astype(BF16)
    wd = w_in[:, d0:v0].reshape(d, 4, N_HEADS, D_QK)
    w_qk = jnp.concatenate([
        jnp.concatenate([wd[:, 0], wd[:, 1]], axis=2).reshape(d, b) * (D_QK ** -0.5 * LOG2E),
        jnp.concatenate([wd[:, 2], wd[:, 3]], axis=2).reshape(d, b),
    ], axis=1).astype(BF16)
    wf_t = jnp.pad(w_in[:, f0:d0].T, ((0, N_HEADS), (0, 0))).astype(BF16)

    rows = _matmul(u, w_rows, BF16)
    gates = _matmul(u, w_g, F32)
    vt = _matmul_t(w_vt, u, BF16)
    qk_d = _matmul(u, w_qk, F32)

    cum = _cumsum_seq(_forget_log_gates(wf_t, u, forget_bias))
    qx, kx = _decay_features(cum.reshape(N_HEADS, s, 1))
    t = min(512, s)
    y_c = _mixer_c(rows, 0, N_HEADS, qx, kx, vt, 0, gates, 0, t)

    lane = jnp.arange(128) % D_QK
    inv_freq = ROPE_THETA ** (-jnp.arange(0, ROPE_DIM, 2, dtype=F32) / ROPE_DIM)
    invf = jnp.where(lane < ROPE_DIM, inv_freq[lane % (ROPE_DIM // 2)], 0.0).astype(F32).reshape(1, 128)
    qk_rot = _rope(qk_d, positions.reshape(s, 1), invf)
    lam_vecs = jnp.stack([lq1, lk1, lq2, lk2]).astype(F32)
    y_d = _mixer_d(qk_rot, 0, N_HEADS, vt, N_HEADS, gates, N_HEADS, lam_vecs, subln, lambda_init, t)
    return _out_proj(y_c, y_d, w_out.astype(BF16), h, norm_post)


def kernel(x, positions, even_w_in, even_rel_bias, even_w_out, even_norm_pre, even_norm_post,
           odd_w_in, odd_forget_bias, odd_lambda_q1, odd_lambda_k1, odd_lambda_q2, odd_lambda_k2,
           odd_subln, odd_w_out, odd_norm_pre, odd_norm_post):
    assert x.shape[0] == 1
    h = x[0]
    depth = even_w_in.shape[0] + odd_w_in.shape[0]
    for layer in range(depth):
        i = layer // 2
        if layer % 2 == 0:
            h = _even_layer(h, even_w_in[i], even_rel_bias[i], even_w_out[i],
                            even_norm_pre[i], even_norm_post[i])
        else:
            lambda_init = 0.8 - 0.6 * math.exp(-0.3 * layer)
            h = _odd_layer(h, positions, odd_w_in[i], odd_forget_bias[i],
                           odd_lambda_q1[i], odd_lambda_k1[i], odd_lambda_q2[i], odd_lambda_k2[i],
                           odd_subln[i], odd_w_out[i], odd_norm_pre[i], odd_norm_post[i], lambda_init)
    return h[None]
```

```python
import functools
import math

import jax
import jax.numpy as jnp
import numpy as np
from jax import lax
from jax.experimental import pallas as pl
from jax.experimental.pallas import tpu as pltpu

F32 = jnp.float32
BF16 = jnp.bfloat16

HEAD_DIM = 128
N_HEADS = 8
BRANCH = N_HEADS * HEAD_DIM
CHUNK = 64
CHUNK_SHIFT = 6
A_LEFT_CHUNKS = 8
A_PAD = A_LEFT_CHUNKS * CHUNK
REL_CLIP = 128
D_QK = 64
ROPE_THETA = 500000.0
ROPE_DIM = 16
EPS = 1e-6
NEG_INF = float("-inf")
LOG2E = math.log2(math.e)
BF16_SUBLANES = 16
VMEM_LIMIT_BYTES = 48 * 1024 * 1024

_NT = (((1,), (1,)), ((), ()))


def _params(*semantics):
    return pltpu.CompilerParams(dimension_semantics=semantics, vmem_limit_bytes=VMEM_LIMIT_BYTES)


def _silu(g):
    return g * jax.nn.sigmoid(g)


def _log_sigmoid(z):
    return jnp.minimum(z, 0.0) - jnp.log1p(jnp.exp(-jnp.abs(z)))


def _rmsnorm_kernel(x_ref, g_ref, o_ref):
    x = x_ref[...]
    ms = jnp.mean(x * x, axis=-1, keepdims=True)
    o_ref[...] = (x * lax.rsqrt(ms + EPS) * g_ref[...]).astype(o_ref.dtype)


def _rmsnorm(x, g):
    s, d = x.shape
    tm = min(512, s)
    return pl.pallas_call(
        _rmsnorm_kernel,
        out_shape=jax.ShapeDtypeStruct((s, d), BF16),
        grid=(s // tm,),
        in_specs=[pl.BlockSpec((tm, d), lambda i: (i, 0)),
                  pl.BlockSpec((1, d), lambda i: (0, 0))],
        out_specs=pl.BlockSpec((tm, d), lambda i: (i, 0)),
        compiler_params=_params("parallel"),
        name="rmsnorm",
    )(x, g.reshape(1, d))


def _matmul_kernel(x_ref, w_ref, o_ref):
    o_ref[...] = jnp.dot(x_ref[...], w_ref[...], preferred_element_type=F32).astype(o_ref.dtype)


def _matmul(x, w, out_dtype):
    m, k = x.shape
    n = w.shape[1]
    tm = min(512, m)
    tn = 1024 if n % 1024 == 0 else n
    return pl.pallas_call(
        _matmul_kernel,
        out_shape=jax.ShapeDtypeStruct((m, n), out_dtype),
        grid=(n // tn, m // tm),
        in_specs=[pl.BlockSpec((tm, k), lambda j, i: (i, 0)),
                  pl.BlockSpec((k, tn), lambda j, i: (0, j))],
        out_specs=pl.BlockSpec((tm, tn), lambda j, i: (i, j)),
        compiler_params=_params("parallel", "parallel"),
        name="in_proj",
    )(x, w)


def _matmul_t_kernel(w_ref, x_ref, o_ref):
    o_ref[...] = lax.dot_general(w_ref[...], x_ref[...], _NT,
                                 preferred_element_type=F32).astype(o_ref.dtype)


def _matmul_t(w_t, x, out_dtype):
    n, k = w_t.shape
    m = x.shape[0]
    tm = min(512, m)
    tn = 1024 if n % 1024 == 0 else n
    return pl.pallas_call(
        _matmul_t_kernel,
        out_shape=jax.ShapeDtypeStruct((n, m), out_dtype),
        grid=(n // tn, m // tm),
        in_specs=[pl.BlockSpec((tn, k), lambda j, i: (j, 0)),
                  pl.BlockSpec((tm, k), lambda j, i: (i, 0))],
        out_specs=pl.BlockSpec((tn, tm), lambda j, i: (j, i)),
        compiler_params=_params("parallel", "parallel"),
        name="in_proj_t",
    )(w_t, x)


def _mixer_a_kernel(q_ref, k_ref, v_ref, g_ref, b_ref, o_ref, *, tq, band, scale):
    i = pl.program_id(1)
    start = pl.multiple_of(i * tq, tq)
    k = k_ref[pl.ds(start, band), :]
    v = v_ref[pl.ds(start, band), :]
    s = lax.dot_general(q_ref[...], k, _NT, preferred_element_type=F32) * scale + b_ref[0]
    r = lax.broadcasted_iota(jnp.int32, (tq, band), 0)
    c = lax.broadcasted_iota(jnp.int32, (tq, band), 1)
    qc = r >> CHUNK_SHIFT
    kc = c >> CHUNK_SHIFT
    valid = (kc >= qc) & (kc <= qc + A_LEFT_CHUNKS) & (c + start >= A_PAD)
    s = jnp.where(valid, s, NEG_INF)
    m = jnp.max(s, axis=1, keepdims=True)
    p = jnp.exp(s - m)
    l = jnp.sum(p, axis=1, keepdims=True)
    o = jnp.dot(p.astype(BF16), v, preferred_element_type=F32) / l
    o_ref[...] = (o * _silu(g_ref[...])).astype(o_ref.dtype)


def _mixer_a(q_arr, q_col, kv_pad, k_col, v_col, g_arr, g_col, bias_tile, tq):
    s = q_arr.shape[0]
    sp = kv_pad.shape[0]
    band = A_PAD + tq
    kern = functools.partial(_mixer_a_kernel, tq=tq, band=band, scale=HEAD_DIM ** -0.5)
    return pl.pallas_call(
        kern,
        out_shape=jax.ShapeDtypeStruct((s, BRANCH), BF16),
        grid=(N_HEADS, s // tq),
        in_specs=[pl.BlockSpec((tq, HEAD_DIM), lambda h, i: (i, q_col + h)),
                  pl.BlockSpec((sp, HEAD_DIM), lambda h, i: (0, k_col + h)),
                  pl.BlockSpec((sp, HEAD_DIM), lambda h, i: (0, v_col + h)),
                  pl.BlockSpec((tq, HEAD_DIM), lambda h, i: (i, g_col + h)),
                  pl.BlockSpec((1, tq, band), lambda h, i: (h, 0, 0))],
        out_specs=pl.BlockSpec((tq, HEAD_DIM), lambda h, i: (i, h)),
        compiler_params=_params("parallel", "parallel"),
        name="mixer_a",
    )(q_arr, kv_pad, kv_pad, g_arr, bias_tile)


def _key_query_index(tk, tq, key_off):
    key = lax.broadcasted_iota(jnp.int32, (tk, tq), 0) + key_off
    query = lax.broadcasted_iota(jnp.int32, (tk, tq), 1)
    return key, query


def _softmax_step(s, masked_out, v_aug, m_sc, acc_sc):
    if masked_out is not None:
        s = jnp.where(masked_out, NEG_INF, s)
    m_old = m_sc[...]
    m_new = jnp.maximum(m_old, jnp.max(s, axis=0, keepdims=True))
    p = jnp.exp2(s - m_new).astype(BF16)
    alpha = jnp.exp2(m_old - m_new)
    acc_sc[...] = alpha * acc_sc[...] + jnp.dot(v_aug, p, preferred_element_type=F32)
    m_sc[...] = m_new


def _init_softmax_state(m_sc, acc_sc):
    m_sc[...] = jnp.full_like(m_sc, NEG_INF)
    acc_sc[...] = jnp.zeros_like(acc_sc)


def _normalised_rows(acc_sc):
    acc = acc_sc[...]
    return (acc[:HEAD_DIM] / acc[HEAD_DIM:HEAD_DIM + 1]).T


def _ones_rows(width):
    return jnp.ones((BF16_SUBLANES, width), BF16)


def _sweep_pairs(count, scores_into, consume_from, buf_a, buf_b, count_is_even=False):
    def body(p, carry):
        scores_into(buf_b, 2 * p + 1)
        consume_from(buf_a, 2 * p)
        scores_into(buf_a, 2 * p + 2)
        consume_from(buf_b, 2 * p + 1)
        return carry

    lax.fori_loop(0, count // 2, body, 0)
    if count_is_even:
        return

    @pl.when(count % 2 == 1)
    def _():
        consume_from(buf_a, count - 1)


def _mixer_b_kernel(q_ref, k_ref, vt_ref, g_ref, o_ref, acc_sc, run_sc, s0_sc, s1_sc, *, tq, tk):
    assert tq == 2 * tk
    i = pl.program_id(1)
    q = q_ref[...]
    kk = lax.broadcasted_iota(jnp.int32, (tk + BF16_SUBLANES, tk), 0)
    jj = lax.broadcasted_iota(jnp.int32, (tk + BF16_SUBLANES, tk), 1)
    suffix = jnp.where((jj > kk) | (kk >= tk), 1.0, 0.0).astype(BF16)
    acc_sc[...] = jnp.zeros_like(acc_sc)
    run_sc[...] = jnp.zeros_like(run_sc)

    def scores_into(buf, j):
        off = pl.multiple_of(j * tk, tk)
        buf[...] = lax.dot_general(k_ref[pl.ds(off, tk), :], q, _NT, preferred_element_type=F32)

    def consume_from(buf, j, key_off=None):
        off = pl.multiple_of(j * tk, tk)
        z = buf[...]
        log_beta = jnp.minimum(z, 0.0) - jnp.log2(1.0 + jnp.exp2(-jnp.abs(z)))
        log_keep = log_beta - z
        if key_off is not None:
            key, query = _key_query_index(tk, tq, key_off)
            strict = key < query
            log_keep = jnp.where(strict, log_keep, 0.0)
        hi = log_keep.astype(BF16)
        lo = (log_keep - hi.astype(F32)).astype(BF16)
        later = (jnp.dot(suffix, hi, preferred_element_type=F32)
                 + jnp.dot(suffix, lo, preferred_element_type=F32))
        w = jnp.exp2(log_beta + later[:tk] + run_sc[...])
        if key_off is not None:
            w = jnp.where(strict, w, 0.0)
        acc_sc[...] += jnp.dot(vt_ref[:, pl.ds(off, tk)], w.astype(BF16), preferred_element_type=F32)
        run_sc[...] += later[tk:tk + 1]

    def below(n):
        return jnp.maximum(2 * i - 1 - n, 0)

    scores_into(s1_sc, 2 * i + 1)
    scores_into(s0_sc, 2 * i)
    consume_from(s1_sc, 2 * i + 1, tk)
    scores_into(s1_sc, below(0))
    consume_from(s0_sc, 2 * i, 0)
    _sweep_pairs(2 * i, lambda buf, n: scores_into(buf, below(n)),
                 lambda buf, n: consume_from(buf, below(n)), s1_sc, s0_sc, count_is_even=True)
    o_ref[...] = (acc_sc[...].T * _silu(g_ref[...])).astype(o_ref.dtype)


def _mixer_b(qk, q_col, k_col, vt, v_row, g_arr, g_col, tq, tk):
    s = qk.shape[0]
    kern = functools.partial(_mixer_b_kernel, tq=tq, tk=tk)
    return pl.pallas_call(
        kern,
        out_shape=jax.ShapeDtypeStruct((s, BRANCH), BF16),
        grid=(N_HEADS, s // tq),
        in_specs=[pl.BlockSpec((tq, HEAD_DIM), lambda h, i: (i, q_col + h)),
                  pl.BlockSpec((s, HEAD_DIM), lambda h, i: (0, k_col + h)),
                  pl.BlockSpec((HEAD_DIM, s), lambda h, i: (v_row + h, 0)),
                  pl.BlockSpec((tq, HEAD_DIM), lambda h, i: (i, g_col + h))],
        out_specs=pl.BlockSpec((tq, HEAD_DIM), lambda h, i: (i, h)),
        scratch_shapes=[pltpu.VMEM((HEAD_DIM, tq), F32), pltpu.VMEM((1, tq), F32),
                        pltpu.VMEM((tk, tq), F32), pltpu.VMEM((tk, tq), F32)],
        compiler_params=_params("parallel", "parallel"),
        name="mixer_b",
    )(qk, qk, vt, g_arr)


def _forget_kernel(wf_ref, u_ref, b_ref, o_ref):
    f = lax.dot_general(wf_ref[...], u_ref[...], _NT, preferred_element_type=F32)
    o_ref[...] = _log_sigmoid(f[:N_HEADS] + b_ref[...])


def _forget_log_gates(wf_t, u, bias):
    s, d = u.shape
    tm = min(1024, s)
    return pl.pallas_call(
        _forget_kernel,
        out_shape=jax.ShapeDtypeStruct((N_HEADS, s), F32),
        grid=(s // tm,),
        in_specs=[pl.BlockSpec((2 * N_HEADS, d), lambda i: (0, 0)),
                  pl.BlockSpec((tm, d), lambda i: (i, 0)),
                  pl.BlockSpec((N_HEADS, 1), lambda i: (0, 0))],
        out_specs=pl.BlockSpec((N_HEADS, tm), lambda i: (0, i)),
        compiler_params=_params("parallel"),
        name="forget_gates",
    )(wf_t, u, bias.reshape(N_HEADS, 1))


def _cumsum_kernel(x_ref, o_ref, *, nb_shift):
    x = x_ref[...]
    nrows = x.shape[0]
    r = lax.broadcasted_iota(jnp.int32, (128, 128), 0)
    c = lax.broadcasted_iota(jnp.int32, (128, 128), 1)
    upto = jnp.where(r <= c, 1.0, 0.0).astype(F32)
    within = jnp.dot(x, upto, preferred_element_type=F32, precision=lax.Precision.HIGHEST)
    row_sum = jnp.broadcast_to(jnp.sum(x, axis=1, keepdims=True), (nrows, 128))
    rr = lax.broadcasted_iota(jnp.int32, (nrows, nrows), 0)
    cc = lax.broadcasted_iota(jnp.int32, (nrows, nrows), 1)
    earlier = jnp.where(((rr >> nb_shift) == (cc >> nb_shift)) & (cc < rr), 1.0, 0.0).astype(F32)
    o_ref[...] = within + jnp.dot(earlier, row_sum, preferred_element_type=F32,
                                  precision=lax.Precision.HIGHEST)


def _cumsum_seq(log_f):
    h, s = log_f.shape
    nb = s // 128
    nb_shift = nb.bit_length() - 1
    assert nb == 1 << nb_shift
    x = log_f.reshape(h * nb, 128)
    out = pl.pallas_call(
        functools.partial(_cumsum_kernel, nb_shift=nb_shift),
        out_shape=jax.ShapeDtypeStruct(x.shape, F32),
        compiler_params=pltpu.CompilerParams(vmem_limit_bytes=VMEM_LIMIT_BYTES),
        name="forget_cumsum",
    )(x)
    return out.reshape(h, s)


def _decay_features_kernel(c_ref, qx_ref, kx_ref):
    c = c_ref[0] * LOG2E
    hi = c.astype(BF16)
    r1 = c - hi.astype(F32)
    mid = r1.astype(BF16)
    lo = (r1 - mid.astype(F32)).astype(BF16)
    shape = qx_ref.shape
    lane = lax.broadcasted_iota(jnp.int32, shape, 1)
    parts = [jnp.broadcast_to(t.astype(F32), shape) for t in (hi, mid, lo)]
    one = jnp.ones(shape, F32)
    zero = jnp.zeros(shape, F32)
    qx = jnp.where(lane == 0, parts[0], jnp.where(lane == 1, parts[1], jnp.where(lane == 2, parts[2],
                   jnp.where(lane < 6, one, zero))))
    kx = jnp.where(lane < 3, one, jnp.where(lane == 3, -parts[0], jnp.where(lane == 4, -parts[1],
                   jnp.where(lane == 5, -parts[2], zero))))
    qx_ref[...] = qx.astype(BF16)
    kx_ref[...] = kx.astype(BF16)


def _decay_features(cum_col):
    h, s, _ = cum_col.shape
    tm = min(1024, s)
    out = jax.ShapeDtypeStruct((s, h * HEAD_DIM), BF16)
    spec = pl.BlockSpec((tm, HEAD_DIM), lambda hh, i: (i, hh))
    return pl.pallas_call(
        _decay_features_kernel,
        out_shape=(out, out),
        grid=(h, s // tm),
        in_specs=[pl.BlockSpec((1, tm, 1), lambda hh, i: (hh, i, 0))],
        out_specs=(spec, spec),
        compiler_params=_params("parallel", "parallel"),
        name="decay_features",
    )(cum_col)


def _mixer_c_kernel(q_ref, qx_ref, k_ref, kx_ref, vt_ref, g_ref, o_ref, m_sc, acc_sc, s0_sc, s1_sc, *, t):
    i = pl.program_id(1)
    q = jnp.concatenate([q_ref[...], qx_ref[...]], axis=1)
    ones = _ones_rows(t)
    _init_softmax_state(m_sc, acc_sc)
    last = jnp.maximum(i - 1, 0)

    def scores_into(buf, j):
        off = pl.multiple_of(j * t, t)
        k = jnp.concatenate([k_ref[pl.ds(off, t), :], kx_ref[pl.ds(off, t), :]], axis=1)
        buf[...] = lax.dot_general(k, q, _NT, preferred_element_type=F32)

    def consume_from(buf, j, masked_out=None):
        off = pl.multiple_of(j * t, t)
        v_aug = jnp.concatenate([vt_ref[:, pl.ds(off, t)], ones], axis=0)
        _softmax_step(buf[...], masked_out, v_aug, m_sc, acc_sc)

    scores_into(s1_sc, i)
    scores_into(s0_sc, 0)
    key, query = _key_query_index(t, t, 0)
    consume_from(s1_sc, i, key > query)
    _sweep_pairs(i, lambda buf, n: scores_into(buf, jnp.minimum(n, last)), consume_from, s0_sc, s1_sc)
    o_ref[...] = (_normalised_rows(acc_sc) * _silu(g_ref[...])).astype(o_ref.dtype)


def _mixer_c(qk, q_col, k_col, qx, kx, vt, v_row, g_arr, g_col, t):
    s = qk.shape[0]
    return pl.pallas_call(
        functools.partial(_mixer_c_kernel, t=t),
        out_shape=jax.ShapeDtypeStruct((s, BRANCH), BF16),
        grid=(N_HEADS, s // t),
        in_specs=[pl.BlockSpec((t, HEAD_DIM), lambda h, i: (i, q_col + h)),
                  pl.BlockSpec((t, HEAD_DIM), lambda h, i: (i, h)),
                  pl.BlockSpec((s, HEAD_DIM), lambda h, i: (0, k_col + h)),
                  pl.BlockSpec((s, HEAD_DIM), lambda h, i: (0, h)),
                  pl.BlockSpec((HEAD_DIM, s), lambda h, i: (v_row + h, 0)),
                  pl.BlockSpec((t, HEAD_DIM), lambda h, i: (i, g_col + h))],
        out_specs=pl.BlockSpec((t, HEAD_DIM), lambda h, i: (i, h)),
        scratch_shapes=[pltpu.VMEM((1, t), F32),
                        pltpu.VMEM((HEAD_DIM + BF16_SUBLANES, t), F32),
                        pltpu.VMEM((t, t), F32), pltpu.VMEM((t, t), F32)],
        compiler_params=_params("parallel", "parallel"),
        name="mixer_c",
    )(qk, qx, qk, kx, vt, g_arr)


def _rope_kernel(x_ref, pos_ref, invf_ref, o_ref, *, groups):
    ang = pos_ref[...].astype(F32) * invf_ref[...]
    cos = jnp.cos(ang)
    sin = jnp.sin(ang)
    lane = lax.broadcasted_iota(jnp.int32, ang.shape, 1)
    first_half = (lane & (D_QK - 1)) < ROPE_DIM // 2
    for gidx in range(groups):
        x = x_ref[:, gidx * 128:(gidx + 1) * 128]
        partner = jnp.where(first_half,
                            -pltpu.roll(x, 128 - ROPE_DIM // 2, 1),
                            pltpu.roll(x, ROPE_DIM // 2, 1))
        o_ref[:, gidx * 128:(gidx + 1) * 128] = (x * cos + partner * sin).astype(o_ref.dtype)


def _rope(x, positions, invf):
    s, n = x.shape
    tm = min(512, s)
    return pl.pallas_call(
        functools.partial(_rope_kernel, groups=n // 128),
        out_shape=jax.ShapeDtypeStruct((s, n), BF16),
        grid=(s // tm,),
        in_specs=[pl.BlockSpec((tm, n), lambda i: (i, 0)),
                  pl.BlockSpec((tm, 1), lambda i: (i, 0)),
                  pl.BlockSpec((1, 128), lambda i: (0, 0))],
        out_specs=pl.BlockSpec((tm, n), lambda i: (i, 0)),
        compiler_params=_params("parallel"),
        name="rope",
    )(x, positions, invf)


def _mixer_d_kernel(q_ref, k_ref, vt_ref, g_ref, lam_ref, sub_ref, o_ref,
                    m1_sc, a1_sc, m2_sc, a2_sc, sa1_sc, sa2_sc, sb1_sc, sb2_sc, *, t, lambda_init):
    i = pl.program_id(1)
    q = q_ref[...]
    lane = lax.broadcasted_iota(jnp.int32, q.shape, 1)
    zero = jnp.zeros_like(q)
    q1 = jnp.where(lane < D_QK, q, zero)
    q2 = jnp.where(lane >= D_QK, q, zero)
    ones = _ones_rows(t)
    _init_softmax_state(m1_sc, a1_sc)
    _init_softmax_state(m2_sc, a2_sc)

    last = jnp.maximum(i - 1, 0)

    def scores_into(bufs, j):
        off = pl.multiple_of(j * t, t)
        k = k_ref[pl.ds(off, t), :]
        for qm, buf in zip((q1, q2), bufs):
            buf[...] = lax.dot_general(k, qm, _NT, preferred_element_type=F32)

    def consume_from(bufs, j, masked_out=None):
        off = pl.multiple_of(j * t, t)
        v_aug = jnp.concatenate([vt_ref[:, pl.ds(off, t)], ones], axis=0)
        for buf, m_sc, a_sc in zip(bufs, (m1_sc, m2_sc), (a1_sc, a2_sc)):
            _softmax_step(buf[...], masked_out, v_aug, m_sc, a_sc)

    bufs_a, bufs_b = (sa1_sc, sa2_sc), (sb1_sc, sb2_sc)
    scores_into(bufs_b, i)
    scores_into(bufs_a, 0)
    key, query = _key_query_index(t, t, 0)
    consume_from(bufs_b, i, (key >> CHUNK_SHIFT) > (query >> CHUNK_SHIFT))
    _sweep_pairs(i, lambda bufs, n: scores_into(bufs, jnp.minimum(n, last)), consume_from, bufs_a, bufs_b)

    lv = lam_ref[...]
    lam = (jnp.exp(jnp.sum(lv[0:1] * lv[1:2], axis=1, keepdims=True))
           - jnp.exp(jnp.sum(lv[2:3] * lv[3:4], axis=1, keepdims=True)) + lambda_init)
    o = _normalised_rows(a1_sc) - lam * _normalised_rows(a2_sc)
    ms = jnp.mean(o * o, axis=-1, keepdims=True)
    y = o * lax.rsqrt(ms + EPS) * sub_ref[...] * (1.0 - lambda_init)
    o_ref[...] = (y * _silu(g_ref[...])).astype(o_ref.dtype)


def _mixer_d(qk_arr, q_col, k_col, vt, v_row, g_arr, g_col, lam_vecs, subln, lambda_init, t):
    s = qk_arr.shape[0]
    kern = functools.partial(_mixer_d_kernel, t=t, lambda_init=lambda_init)
    row_stat = pltpu.VMEM((1, t), F32)
    acc = pltpu.VMEM((HEAD_DIM + BF16_SUBLANES, t), F32)
    return pl.pallas_call(
        kern,
        out_shape=jax.ShapeDtypeStruct((s, BRANCH), BF16),
        grid=(N_HEADS, s // t),
        in_specs=[pl.BlockSpec((t, HEAD_DIM), lambda h, i: (i, q_col + h)),
                  pl.BlockSpec((s, HEAD_DIM), lambda h, i: (0, k_col + h)),
                  pl.BlockSpec((HEAD_DIM, s), lambda h, i: (v_row + h, 0)),
                  pl.BlockSpec((t, HEAD_DIM), lambda h, i: (i, g_col + h)),
                  pl.BlockSpec((4, D_QK), lambda h, i: (0, 0)),
                  pl.BlockSpec((1, HEAD_DIM), lambda h, i: (0, 0))],
        out_specs=pl.BlockSpec((t, HEAD_DIM), lambda h, i: (i, h)),
        scratch_shapes=[row_stat, acc, row_stat, acc] + [pltpu.VMEM((t, t), F32)] * 4,
        compiler_params=_params("parallel", "parallel"),
        name="mixer_d",
    )(qk_arr, qk_arr, vt, g_arr, lam_vecs, subln.reshape(1, HEAD_DIM))


def _out_proj_kernel(ya_ref, yb_ref, w_ref, h_ref, g_ref, o_ref):
    half = ya_ref.shape[1]
    y = (jnp.dot(ya_ref[...], w_ref[:half, :], preferred_element_type=F32)
         + jnp.dot(yb_ref[...], w_ref[half:, :], preferred_element_type=F32))
    ms = jnp.mean(y * y, axis=-1, keepdims=True)
    o_ref[...] = h_ref[...] + y * lax.rsqrt(ms + EPS) * g_ref[...]


def _out_proj(ya, yb, w, h, g):
    s, d = h.shape
    tm = min(256, s)
    return pl.pallas_call(
        _out_proj_kernel,
        out_shape=jax.ShapeDtypeStruct((s, d), F32),
        grid=(s // tm,),
        in_specs=[pl.BlockSpec((tm, BRANCH), lambda i: (i, 0)),
                  pl.BlockSpec((tm, BRANCH), lambda i: (i, 0)),
                  pl.BlockSpec((2 * BRANCH, d), lambda i: (0, 0)),
                  pl.BlockSpec((tm, d), lambda i: (i, 0)),
                  pl.BlockSpec((1, d), lambda i: (0, 0))],
        out_specs=pl.BlockSpec((tm, d), lambda i: (i, 0)),
        compiler_params=_params("parallel"),
        name="out_proj",
    )(ya, yb, w, h, g.reshape(1, d))


def _rel_bias_tile(rel_bias, tq):
    band = A_PAD + tq
    period = band + tq
    k = np.arange(period)
    c_minus_r = np.where(k < band, k, k - period)
    line = rel_bias.astype(F32)[:, np.clip(A_PAD - c_minus_r, -REL_CLIP, REL_CLIP) + REL_CLIP]
    flat = jnp.tile(line, (1, tq))[:, :tq * (period - 1)]
    return flat.reshape(rel_bias.shape[0], tq, period - 1)[:, :, :band]


def _even_layer(h, w_in, rel_bias, w_out, norm_pre, norm_post):
    s = h.shape[0]
    u = _rmsnorm(h, norm_pre)
    b = BRANCH
    log2_scale = HEAD_DIM ** -0.5 * LOG2E
    w_rows = jnp.concatenate([w_in[:, :3 * b], w_in[:, 4 * b:5 * b] * log2_scale, w_in[:, 5 * b:6 * b]],
                             axis=1).astype(BF16)
    w_g = jnp.concatenate([w_in[:, 3 * b:4 * b], w_in[:, 7 * b:]], axis=1).astype(BF16)
    rows = _matmul(u, w_rows, BF16)
    gates = _matmul(u, w_g, F32)
    vt_b = _matmul_t(w_in[:, 6 * b:7 * b].T.astype(BF16), u, BF16)
    tq_a = min(256, s)
    kv_pad = jnp.pad(rows[:, b:3 * b], ((A_PAD, 0), (0, 0)))
    y_a = _mixer_a(rows, 0, kv_pad, 0, N_HEADS, gates, 0, _rel_bias_tile(rel_bias, tq_a), tq_a)
    y_b = _mixer_b(rows, 3 * N_HEADS, 4 * N_HEADS, vt_b, 0, gates, N_HEADS, min(512, s), min(256, s))
    return _out_proj(y_a, y_b, w_out.astype(BF16), h, norm_post)


def _odd_layer(h, positions, w_in, forget_bias, lq1, lk1, lq2, lk2, subln, w_out, norm_pre, norm_post,
               lambda_init):
    s, d = h.shape
    u = _rmsnorm(h, norm_pre)
    b = BRANCH
    f0 = 4 * b
    d0 = f0 + N_HEADS
    v0 = d0 + 4 * N_HEADS * D_QK
    w_rows = jnp.concatenate([w_in[:, :b] * (HEAD_DIM ** -0.5 * LOG2E), w_in[:, b:2 * b]],
                             axis=1).astype(BF16)
    w_g = jnp.concatenate([w_in[:, 3 * b:4 * b], w_in[:, v0 + b:]], axis=1).astype(BF16)
    w_vt = jnp.concatenate([w_in[:, 2 * b:3 * b], w_in[:, v0:v0 + b]], axis=1).T.astype(BF16)
    wd = w_in[:, d0:v0].reshape(d, 4, N_HEADS, D_QK)
    w_qk = jnp.concatenate([
        jnp.concatenate([wd[:, 0], wd[:, 1]], axis=2).reshape(d, b) * (D_QK ** -0.5 * LOG2E),
        jnp.concatenate([wd[:, 2], wd[:, 3]], axis=2).reshape(d, b),
    ], axis=1).astype(BF16)
    wf_t = jnp.pad(w_in[:, f0:d0].T, ((0, N_HEADS), (0, 0))).astype(BF16)

    rows = _matmul(u, w_rows, BF16)
    gates = _matmul(u, w_g, F32)
    vt = _matmul_t(w_vt, u, BF16)
    qk_d = _matmul(u, w_qk, F32)

    cum = _cumsum_seq(_forget_log_gates(wf_t, u, forget_bias))
    qx, kx = _decay_features(cum.reshape(N_HEADS, s, 1))
    t = min(512, s)
    y_c = _mixer_c(rows, 0, N_HEADS, qx, kx, vt, 0, gates, 0, t)

    lane = jnp.arange(128) % D_QK
    inv_freq = ROPE_THETA ** (-jnp.arange(0, ROPE_DIM, 2, dtype=F32) / ROPE_DIM)
    invf = jnp.where(lane < ROPE_DIM, inv_freq[lane % (ROPE_DIM // 2)], 0.0).astype(F32).reshape(1, 128)
    qk_rot = _rope(qk_d, positions.reshape(s, 1), invf)
    lam_vecs = jnp.stack([lq1, lk1, lq2, lk2]).astype(F32)
    y_d = _mixer_d(qk_rot, 0, N_HEADS, vt, N_HEADS, gates, N_HEADS, lam_vecs, subln, lambda_init, t)
    return _out_proj(y_c, y_d, w_out.astype(BF16), h, norm_post)


def kernel(x, positions, even_w_in, even_rel_bias, even_w_out, even_norm_pre, even_norm_post,
           odd_w_in, odd_forget_bias, odd_lambda_q1, odd_lambda_k1, odd_lambda_q2, odd_lambda_k2,
           odd_subln, odd_w_out, odd_norm_pre, odd_norm_post):
    assert x.shape[0] == 1
    h = x[0]
    depth = even_w_in.shape[0] + odd_w_in.shape[0]
    for layer in range(depth):
        i = layer // 2
        if layer % 2 == 0:
            h = _even_layer(h, even_w_in[i], even_rel_bias[i], even_w_out[i],
                            even_norm_pre[i], even_norm_post[i])
        else:
            lambda_init = 0.8 - 0.6 * math.exp(-0.3 * layer)
            h = _odd_layer(h, positions, odd_w_in[i], odd_forget_bias[i],
                           odd_lambda_q1[i], odd_lambda_k1[i], odd_lambda_q2[i], odd_lambda_k2[i],
                           odd_subln[i], odd_w_out[i], odd_norm_pre[i], odd_norm_post[i], lambda_init)
    return h[None]
```

```python
import functools
import math

import jax
import jax.numpy as jnp
import numpy as np
from jax import lax
from jax.experimental import pallas as pl
from jax.experimental.pallas import tpu as pltpu

F32 = jnp.float32
BF16 = jnp.bfloat16

HEAD_DIM = 128
N_HEADS = 8
BRANCH = N_HEADS * HEAD_DIM
CHUNK = 64
CHUNK_SHIFT = 6
A_LEFT_CHUNKS = 8
A_PAD = A_LEFT_CHUNKS * CHUNK
REL_CLIP = 128
D_QK = 64
ROPE_THETA = 500000.0
ROPE_DIM = 16
EPS = 1e-6
NEG_INF = float("-inf")
LOG2E = math.log2(math.e)
BF16_SUBLANES = 16
SWEEP_STEPS = 4
VMEM_LIMIT_BYTES = 48 * 1024 * 1024

_NT = (((1,), (1,)), ((), ()))


def _params(*semantics):
    return pltpu.CompilerParams(dimension_semantics=semantics, vmem_limit_bytes=VMEM_LIMIT_BYTES)


def _silu(g):
    return g * jax.nn.sigmoid(g)


def _log_sigmoid(z):
    return jnp.minimum(z, 0.0) - jnp.log1p(jnp.exp(-jnp.abs(z)))


def _rmsnorm_kernel(x_ref, g_ref, o_ref):
    x = x_ref[...]
    ms = jnp.mean(x * x, axis=-1, keepdims=True)
    o_ref[...] = (x * lax.rsqrt(ms + EPS) * g_ref[...]).astype(o_ref.dtype)


def _rmsnorm(x, g):
    s, d = x.shape
    tm = min(512, s)
    return pl.pallas_call(
        _rmsnorm_kernel,
        out_shape=jax.ShapeDtypeStruct((s, d), BF16),
        grid=(s // tm,),
        in_specs=[pl.BlockSpec((tm, d), lambda i: (i, 0)),
                  pl.BlockSpec((1, d), lambda i: (0, 0))],
        out_specs=pl.BlockSpec((tm, d), lambda i: (i, 0)),
        compiler_params=_params("parallel"),
        name="rmsnorm",
    )(x, g.reshape(1, d))


def _matmul_kernel(x_ref, w_ref, o_ref):
    o_ref[...] = jnp.dot(x_ref[...], w_ref[...], preferred_element_type=F32).astype(o_ref.dtype)


def _matmul(x, w, out_dtype):
    m, k = x.shape
    n = w.shape[1]
    tm = min(512, m)
    tn = 1024 if n % 1024 == 0 else n
    return pl.pallas_call(
        _matmul_kernel,
        out_shape=jax.ShapeDtypeStruct((m, n), out_dtype),
        grid=(n // tn, m // tm),
        in_specs=[pl.BlockSpec((tm, k), lambda j, i: (i, 0)),
                  pl.BlockSpec((k, tn), lambda j, i: (0, j))],
        out_specs=pl.BlockSpec((tm, tn), lambda j, i: (i, j)),
        compiler_params=_params("parallel", "parallel"),
        name="in_proj",
    )(x, w)


def _matmul_t_kernel(w_ref, x_ref, o_ref):
    o_ref[...] = lax.dot_general(w_ref[...], x_ref[...], _NT,
                                 preferred_element_type=F32).astype(o_ref.dtype)


def _matmul_t(w_t, x, out_dtype):
    n, k = w_t.shape
    m = x.shape[0]
    tm = min(512, m)
    tn = 1024 if n % 1024 == 0 else n
    return pl.pallas_call(
        _matmul_t_kernel,
        out_shape=jax.ShapeDtypeStruct((n, m), out_dtype),
        grid=(n // tn, m // tm),
        in_specs=[pl.BlockSpec((tn, k), lambda j, i: (j, 0)),
                  pl.BlockSpec((tm, k), lambda j, i: (i, 0))],
        out_specs=pl.BlockSpec((tn, tm), lambda j, i: (j, i)),
        compiler_params=_params("parallel", "parallel"),
        name="in_proj_t",
    )(w_t, x)


def _mixer_a_kernel(q_ref, k_ref, v_ref, g_ref, b_ref, o_ref, *, tq, band, scale):
    i = pl.program_id(1)
    start = pl.multiple_of(i * tq, tq)
    k = k_ref[pl.ds(start, band), :]
    v = v_ref[pl.ds(start, band), :]
    s = lax.dot_general(q_ref[...], k, _NT, preferred_element_type=F32) * scale + b_ref[0]
    r = lax.broadcasted_iota(jnp.int32, (tq, band), 0)
    c = lax.broadcasted_iota(jnp.int32, (tq, band), 1)
    qc = r >> CHUNK_SHIFT
    kc = c >> CHUNK_SHIFT
    valid = (kc >= qc) & (kc <= qc + A_LEFT_CHUNKS) & (c + start >= A_PAD)
    s = jnp.where(valid, s, NEG_INF)
    m = jnp.max(s, axis=1, keepdims=True)
    p = jnp.exp(s - m)
    l = jnp.sum(p, axis=1, keepdims=True)
    o = jnp.dot(p.astype(BF16), v, preferred_element_type=F32) / l
    o_ref[...] = (o * _silu(g_ref[...])).astype(o_ref.dtype)


def _mixer_a(q_arr, q_col, kv_pad, k_col, v_col, g_arr, g_col, bias_tile, tq):
    s = q_arr.shape[0]
    sp = kv_pad.shape[0]
    band = A_PAD + tq
    kern = functools.partial(_mixer_a_kernel, tq=tq, band=band, scale=HEAD_DIM ** -0.5)
    return pl.pallas_call(
        kern,
        out_shape=jax.ShapeDtypeStruct((s, BRANCH), BF16),
        grid=(N_HEADS, s // tq),
        in_specs=[pl.BlockSpec((tq, HEAD_DIM), lambda h, i: (i, q_col + h)),
                  pl.BlockSpec((sp, HEAD_DIM), lambda h, i: (0, k_col + h)),
                  pl.BlockSpec((sp, HEAD_DIM), lambda h, i: (0, v_col + h)),
                  pl.BlockSpec((tq, HEAD_DIM), lambda h, i: (i, g_col + h)),
                  pl.BlockSpec((1, tq, band), lambda h, i: (h, 0, 0))],
        out_specs=pl.BlockSpec((tq, HEAD_DIM), lambda h, i: (i, h)),
        compiler_params=_params("parallel", "parallel"),
        name="mixer_a",
    )(q_arr, kv_pad, kv_pad, g_arr, bias_tile)


def _key_query_index(tk, tq, key_off):
    key = lax.broadcasted_iota(jnp.int32, (tk, tq), 0) + key_off
    query = lax.broadcasted_iota(jnp.int32, (tk, tq), 1)
    return key, query


def _softmax_step(s, masked_out, v_aug, m_sc, acc_sc):
    if masked_out is not None:
        s = jnp.where(masked_out, NEG_INF, s)
    m_old = m_sc[...]
    m_new = jnp.maximum(m_old, jnp.max(s, axis=0, keepdims=True))
    p = jnp.exp2(s - m_new).astype(BF16)
    alpha = jnp.exp2(m_old - m_new)
    acc_sc[...] = alpha * acc_sc[...] + jnp.dot(v_aug, p, preferred_element_type=F32)
    m_sc[...] = m_new


def _init_softmax_state(m_sc, acc_sc):
    m_sc[...] = jnp.full_like(m_sc, NEG_INF)
    acc_sc[...] = jnp.zeros_like(acc_sc)


def _normalised_rows(acc_sc):
    acc = acc_sc[...]
    return (acc[:HEAD_DIM] / acc[HEAD_DIM:HEAD_DIM + 1]).T


def _ones_rows(width):
    return jnp.ones((BF16_SUBLANES, width), BF16)


def _sweep_pairs(count, scores_into, consume_from, buf_a, buf_b, count_is_even=False):
    def body_of(steps):
        def body(_, first):
            for u in range(steps):
                cur, nxt = (buf_a, buf_b) if u % 2 == 0 else (buf_b, buf_a)
                scores_into(nxt, first + u + 1)
                consume_from(cur, first + u)
            return first + steps
        return body

    done = lax.fori_loop(0, count // SWEEP_STEPS, body_of(SWEEP_STEPS), 0)
    lax.fori_loop(0, (count - done) // 2, body_of(2), done)
    if count_is_even:
        return

    @pl.when(count % 2 == 1)
    def _():
        consume_from(buf_a, count - 1)


def _mixer_b_kernel(q_ref, k_ref, vt_ref, g_ref, o_ref, acc_sc, run_sc, s0_sc, s1_sc, *, tq, tk):
    assert tq == 2 * tk
    i = pl.program_id(1)
    q = q_ref[...]
    half = tk // 2
    kk = lax.broadcasted_iota(jnp.int32, (half + BF16_SUBLANES, tk), 0)
    jj = lax.broadcasted_iota(jnp.int32, (half + BF16_SUBLANES, tk), 1) & (half - 1)
    suffix = jnp.where((jj > kk) | (kk >= half), 1.0, 0.0).astype(BF16)
    acc_sc[...] = jnp.zeros_like(acc_sc)
    run_sc[...] = jnp.zeros_like(run_sc)

    def scores_into(buf, j):
        off = pl.multiple_of(j * tk, tk)
        buf[...] = lax.dot_general(k_ref[pl.ds(off, tk), :], q, _NT, preferred_element_type=F32)

    def consume_from(buf, j, key_off=None):
        off = pl.multiple_of(j * tk, tk)
        z = buf[...]
        neg_part = jnp.minimum(z, 0.0)
        log_beta = neg_part - jnp.log(1.0 + jnp.exp2(neg_part + neg_part - z)) * LOG2E
        log_keep = log_beta - z
        if key_off is not None:
            key, query = _key_query_index(tk, tq, key_off)
            strict = key < query
            log_keep = jnp.where(strict, log_keep, 0.0)
        hi = log_keep.astype(BF16)
        lo = (log_keep - hi.astype(F32)).astype(BF16)
        late = jnp.dot(suffix, jnp.concatenate([hi[half:], lo[half:]], axis=0),
                       preferred_element_type=F32)
        early = jnp.dot(suffix, jnp.concatenate([hi[:half], lo[:half]], axis=0),
                        preferred_element_type=F32)
        late_total = late[half:half + 1]
        w = jnp.exp2(jnp.concatenate([log_beta[:half] + (early[:half] + late_total),
                                      log_beta[half:] + late[:half]], axis=0))
        if key_off is not None:
            w = jnp.where(strict, w, 0.0)
        acc_sc[...] += jnp.exp2(run_sc[...]) * jnp.dot(vt_ref[:, pl.ds(off, tk)], w.astype(BF16),
                                                       preferred_element_type=F32)
        run_sc[...] += late_total + early[half:half + 1]

    def below(n):
        return jnp.maximum(2 * i - 1 - n, 0)

    scores_into(s1_sc, 2 * i + 1)
    scores_into(s0_sc, 2 * i)
    consume_from(s1_sc, 2 * i + 1, tk)
    scores_into(s1_sc, below(0))
    consume_from(s0_sc, 2 * i, 0)
    _sweep_pairs(2 * i, lambda buf, n: scores_into(buf, below(n)),
                 lambda buf, n: consume_from(buf, below(n)), s1_sc, s0_sc, count_is_even=True)
    o_ref[...] = (acc_sc[...].T * _silu(g_ref[...])).astype(o_ref.dtype)


def _mixer_b(qk, q_col, k_col, vt, v_row, g_arr, g_col, tq, tk):
    s = qk.shape[0]
    kern = functools.partial(_mixer_b_kernel, tq=tq, tk=tk)
    return pl.pallas_call(
        kern,
        out_shape=jax.ShapeDtypeStruct((s, BRANCH), BF16),
        grid=(N_HEADS, s // tq),
        in_specs=[pl.BlockSpec((tq, HEAD_DIM), lambda h, i: (i, q_col + h)),
                  pl.BlockSpec((s, HEAD_DIM), lambda h, i: (0, k_col + h)),
                  pl.BlockSpec((HEAD_DIM, s), lambda h, i: (v_row + h, 0)),
                  pl.BlockSpec((tq, HEAD_DIM), lambda h, i: (i, g_col + h))],
        out_specs=pl.BlockSpec((tq, HEAD_DIM), lambda h, i: (i, h)),
        scratch_shapes=[pltpu.VMEM((HEAD_DIM, tq), F32), pltpu.VMEM((1, tq), F32),
                        pltpu.VMEM((tk, tq), F32), pltpu.VMEM((tk, tq), F32)],
        compiler_params=_params("parallel", "parallel"),
        name="mixer_b",
    )(qk, qk, vt, g_arr)


def _forget_kernel(wf_ref, u_ref, b_ref, o_ref):
    f = lax.dot_general(wf_ref[...], u_ref[...], _NT, preferred_element_type=F32)
    o_ref[...] = _log_sigmoid(f[:N_HEADS] + b_ref[...])


def _forget_log_gates(wf_t, u, bias):
    s, d = u.shape
    tm = min(1024, s)
    return pl.pallas_call(
        _forget_kernel,
        out_shape=jax.ShapeDtypeStruct((N_HEADS, s), F32),
        grid=(s // tm,),
        in_specs=[pl.BlockSpec((2 * N_HEADS, d), lambda i: (0, 0)),
                  pl.BlockSpec((tm, d), lambda i: (i, 0)),
                  pl.BlockSpec((N_HEADS, 1), lambda i: (0, 0))],
        out_specs=pl.BlockSpec((N_HEADS, tm), lambda i: (0, i)),
        compiler_params=_params("parallel"),
        name="forget_gates",
    )(wf_t, u, bias.reshape(N_HEADS, 1))


def _cumsum_kernel(x_ref, o_ref, *, nb_shift):
    x = x_ref[...]
    nrows = x.shape[0]
    r = lax.broadcasted_iota(jnp.int32, (128, 128), 0)
    c = lax.broadcasted_iota(jnp.int32, (128, 128), 1)
    upto = jnp.where(r <= c, 1.0, 0.0).astype(F32)
    within = jnp.dot(x, upto, preferred_element_type=F32, precision=lax.Precision.HIGHEST)
    row_sum = jnp.broadcast_to(jnp.sum(x, axis=1, keepdims=True), (nrows, 128))
    rr = lax.broadcasted_iota(jnp.int32, (nrows, nrows), 0)
    cc = lax.broadcasted_iota(jnp.int32, (nrows, nrows), 1)
    earlier = jnp.where(((rr >> nb_shift) == (cc >> nb_shift)) & (cc < rr), 1.0, 0.0).astype(F32)
    o_ref[...] = within + jnp.dot(earlier, row_sum, preferred_element_type=F32,
                                  precision=lax.Precision.HIGHEST)


def _cumsum_seq(log_f):
    h, s = log_f.shape
    nb = s // 128
    nb_shift = nb.bit_length() - 1
    assert nb == 1 << nb_shift
    x = log_f.reshape(h * nb, 128)
    out = pl.pallas_call(
        functools.partial(_cumsum_kernel, nb_shift=nb_shift),
        out_shape=jax.ShapeDtypeStruct(x.shape, F32),
        compiler_params=pltpu.CompilerParams(vmem_limit_bytes=VMEM_LIMIT_BYTES),
        name="forget_cumsum",
    )(x)
    return out.reshape(h, s)


def _decay_features_kernel(c_ref, qx_ref, kx_ref):
    c = c_ref[0] * LOG2E
    hi = c.astype(BF16)
    r1 = c - hi.astype(F32)
    mid = r1.astype(BF16)
    lo = (r1 - mid.astype(F32)).astype(BF16)
    shape = qx_ref.shape
    lane = lax.broadcasted_iota(jnp.int32, shape, 1)
    parts = [jnp.broadcast_to(t.astype(F32), shape) for t in (hi, mid, lo)]
    one = jnp.ones(shape, F32)
    zero = jnp.zeros(shape, F32)
    qx = jnp.where(lane == 0, parts[0], jnp.where(lane == 1, parts[1], jnp.where(lane == 2, parts[2],
                   jnp.where(lane < 6, one, zero))))
    kx = jnp.where(lane < 3, one, jnp.where(lane == 3, -parts[0], jnp.where(lane == 4, -parts[1],
                   jnp.where(lane == 5, -parts[2], zero))))
    qx_ref[...] = qx.astype(BF16)
    kx_ref[...] = kx.astype(BF16)


def _decay_features(cum_col):
    h, s, _ = cum_col.shape
    tm = min(1024, s)
    out = jax.ShapeDtypeStruct((s, h * HEAD_DIM), BF16)
    spec = pl.BlockSpec((tm, HEAD_DIM), lambda hh, i: (i, hh))
    return pl.pallas_call(
        _decay_features_kernel,
        out_shape=(out, out),
        grid=(h, s // tm),
        in_specs=[pl.BlockSpec((1, tm, 1), lambda hh, i: (hh, i, 0))],
        out_specs=(spec, spec),
        compiler_params=_params("parallel", "parallel"),
        name="decay_features",
    )(cum_col)


def _mixer_c_kernel(q_ref, qx_ref, k_ref, kx_ref, vt_ref, g_ref, o_ref, m_sc, acc_sc, s0_sc, s1_sc, *, t):
    i = pl.program_id(1)
    q = jnp.concatenate([q_ref[...], qx_ref[...]], axis=1)
    ones = _ones_rows(t)
    _init_softmax_state(m_sc, acc_sc)
    last = jnp.maximum(i - 1, 0)

    def scores_into(buf, j):
        off = pl.multiple_of(j * t, t)
        k = jnp.concatenate([k_ref[pl.ds(off, t), :], kx_ref[pl.ds(off, t), :]], axis=1)
        buf[...] = lax.dot_general(k, q, _NT, preferred_element_type=F32)

    def consume_from(buf, j, masked_out=None):
        off = pl.multiple_of(j * t, t)
        v_aug = jnp.concatenate([vt_ref[:, pl.ds(off, t)], ones], axis=0)
        _softmax_step(buf[...], masked_out, v_aug, m_sc, acc_sc)

    scores_into(s1_sc, i)
    scores_into(s0_sc, 0)
    key, query = _key_query_index(t, t, 0)
    consume_from(s1_sc, i, key > query)
    _sweep_pairs(i, lambda buf, n: scores_into(buf, jnp.minimum(n, last)), consume_from, s0_sc, s1_sc)
    o_ref[...] = (_normalised_rows(acc_sc) * _silu(g_ref[...])).astype(o_ref.dtype)


def _mixer_c(qk, q_col, k_col, qx, kx, vt, v_row, g_arr, g_col, t):
    s = qk.shape[0]
    return pl.pallas_call(
        functools.partial(_mixer_c_kernel, t=t),
        out_shape=jax.ShapeDtypeStruct((s, BRANCH), BF16),
        grid=(N_HEADS, s // t),
        in_specs=[pl.BlockSpec((t, HEAD_DIM), lambda h, i: (i, q_col + h)),
                  pl.BlockSpec((t, HEAD_DIM), lambda h, i: (i, h)),
                  pl.BlockSpec((s, HEAD_DIM), lambda h, i: (0, k_col + h)),
                  pl.BlockSpec((s, HEAD_DIM), lambda h, i: (0, h)),
                  pl.BlockSpec((HEAD_DIM, s), lambda h, i: (v_row + h, 0)),
                  pl.BlockSpec((t, HEAD_DIM), lambda h, i: (i, g_col + h))],
        out_specs=pl.BlockSpec((t, HEAD_DIM), lambda h, i: (i, h)),
        scratch_shapes=[pltpu.VMEM((1, t), F32),
                        pltpu.VMEM((HEAD_DIM + BF16_SUBLANES, t), F32),
                        pltpu.VMEM((t, t), F32), pltpu.VMEM((t, t), F32)],
        compiler_params=_params("parallel", "parallel"),
        name="mixer_c",
    )(qk, qx, qk, kx, vt, g_arr)


def _rope_kernel(x_ref, pos_ref, invf_ref, o_ref, *, groups):
    ang = pos_ref[...].astype(F32) * invf_ref[...]
    cos = jnp.cos(ang)
    sin = jnp.sin(ang)
    lane = lax.broadcasted_iota(jnp.int32, ang.shape, 1)
    first_half = (lane & (D_QK - 1)) < ROPE_DIM // 2
    for gidx in range(groups):
        x = x_ref[:, gidx * 128:(gidx + 1) * 128]
        partner = jnp.where(first_half,
                            -pltpu.roll(x, 128 - ROPE_DIM // 2, 1),
                            pltpu.roll(x, ROPE_DIM // 2, 1))
        o_ref[:, gidx * 128:(gidx + 1) * 128] = (x * cos + partner * sin).astype(o_ref.dtype)


def _rope(x, positions, invf):
    s, n = x.shape
    tm = min(512, s)
    return pl.pallas_call(
        functools.partial(_rope_kernel, groups=n // 128),
        out_shape=jax.ShapeDtypeStruct((s, n), BF16),
        grid=(s // tm,),
        in_specs=[pl.BlockSpec((tm, n), lambda i: (i, 0)),
                  pl.BlockSpec((tm, 1), lambda i: (i, 0)),
                  pl.BlockSpec((1, 128), lambda i: (0, 0))],
        out_specs=pl.BlockSpec((tm, n), lambda i: (i, 0)),
        compiler_params=_params("parallel"),
        name="rope",
    )(x, positions, invf)


def _mixer_d_kernel(q_ref, k_ref, vt_ref, g_ref, lam_ref, sub_ref, o_ref,
                    m1_sc, a1_sc, m2_sc, a2_sc, sa1_sc, sa2_sc, sb1_sc, sb2_sc, *, t, lambda_init):
    i = pl.program_id(1)
    q = q_ref[...]
    lane = lax.broadcasted_iota(jnp.int32, q.shape, 1)
    zero = jnp.zeros_like(q)
    q1 = jnp.where(lane < D_QK, q, zero)
    q2 = jnp.where(lane >= D_QK, q, zero)
    ones = _ones_rows(t)
    _init_softmax_state(m1_sc, a1_sc)
    _init_softmax_state(m2_sc, a2_sc)

    last = jnp.maximum(i - 1, 0)

    def scores_into(bufs, j):
        off = pl.multiple_of(j * t, t)
        k = k_ref[pl.ds(off, t), :]
        for qm, buf in zip((q1, q2), bufs):
            buf[...] = lax.dot_general(k, qm, _NT, preferred_element_type=F32)

    def consume_from(bufs, j, masked_out=None):
        off = pl.multiple_of(j * t, t)
        v_aug = jnp.concatenate([vt_ref[:, pl.ds(off, t)], ones], axis=0)
        for buf, m_sc, a_sc in zip(bufs, (m1_sc, m2_sc), (a1_sc, a2_sc)):
            _softmax_step(buf[...], masked_out, v_aug, m_sc, a_sc)

    bufs_a, bufs_b = (sa1_sc, sa2_sc), (sb1_sc, sb2_sc)
    scores_into(bufs_b, i)
    scores_into(bufs_a, 0)
    key, query = _key_query_index(t, t, 0)
    consume_from(bufs_b, i, (key >> CHUNK_SHIFT) > (query >> CHUNK_SHIFT))
    _sweep_pairs(i, lambda bufs, n: scores_into(bufs, jnp.minimum(n, last)), consume_from, bufs_a, bufs_b)

    lv = lam_ref[...]
    lam = (jnp.exp(jnp.sum(lv[0:1] * lv[1:2], axis=1, keepdims=True))
           - jnp.exp(jnp.sum(lv[2:3] * lv[3:4], axis=1, keepdims=True)) + lambda_init)
    o = _normalised_rows(a1_sc) - lam * _normalised_rows(a2_sc)
    ms = jnp.mean(o * o, axis=-1, keepdims=True)
    y = o * lax.rsqrt(ms + EPS) * sub_ref[...] * (1.0 - lambda_init)
    o_ref[...] = (y * _silu(g_ref[...])).astype(o_ref.dtype)


def _mixer_d(qk_arr, q_col, k_col, vt, v_row, g_arr, g_col, lam_vecs, subln, lambda_init, t):
    s = qk_arr.shape[0]
    kern = functools.partial(_mixer_d_kernel, t=t, lambda_init=lambda_init)
    row_stat = pltpu.VMEM((1, t), F32)
    acc = pltpu.VMEM((HEAD_DIM + BF16_SUBLANES, t), F32)
    return pl.pallas_call(
        kern,
        out_shape=jax.ShapeDtypeStruct((s, BRANCH), BF16),
        grid=(N_HEADS, s // t),
        in_specs=[pl.BlockSpec((t, HEAD_DIM), lambda h, i: (i, q_col + h)),
                  pl.BlockSpec((s, HEAD_DIM), lambda h, i: (0, k_col + h)),
                  pl.BlockSpec((HEAD_DIM, s), lambda h, i: (v_row + h, 0)),
                  pl.BlockSpec((t, HEAD_DIM), lambda h, i: (i, g_col + h)),
                  pl.BlockSpec((4, D_QK), lambda h, i: (0, 0)),
                  pl.BlockSpec((1, HEAD_DIM), lambda h, i: (0, 0))],
        out_specs=pl.BlockSpec((t, HEAD_DIM), lambda h, i: (i, h)),
        scratch_shapes=[row_stat, acc, row_stat, acc] + [pltpu.VMEM((t, t), F32)] * 4,
        compiler_params=_params("parallel", "parallel"),
        name="mixer_d",
    )(qk_arr, qk_arr, vt, g_arr, lam_vecs, subln.reshape(1, HEAD_DIM))


def _out_proj_kernel(ya_ref, yb_ref, w_ref, h_ref, g_ref, o_ref):
    half = ya_ref.shape[1]
    y = (jnp.dot(ya_ref[...], w_ref[:half, :], preferred_element_type=F32)
         + jnp.dot(yb_ref[...], w_ref[half:, :], preferred_element_type=F32))
    ms = jnp.mean(y * y, axis=-1, keepdims=True)
    o_ref[...] = h_ref[...] + y * lax.rsqrt(ms + EPS) * g_ref[...]


def _out_proj(ya, yb, w, h, g):
    s, d = h.shape
    tm = min(256, s)
    return pl.pallas_call(
        _out_proj_kernel,
        out_shape=jax.ShapeDtypeStruct((s, d), F32),
        grid=(s // tm,),
        in_specs=[pl.BlockSpec((tm, BRANCH), lambda i: (i, 0)),
                  pl.BlockSpec((tm, BRANCH), lambda i: (i, 0)),
                  pl.BlockSpec((2 * BRANCH, d), lambda i: (0, 0)),
                  pl.BlockSpec((tm, d), lambda i: (i, 0)),
                  pl.BlockSpec((1, d), lambda i: (0, 0))],
        out_specs=pl.BlockSpec((tm, d), lambda i: (i, 0)),
        compiler_params=_params("parallel"),
        name="out_proj",
    )(ya, yb, w, h, g.reshape(1, d))


def _rel_bias_tile(rel_bias, tq):
    band = A_PAD + tq
    period = band + tq
    k = np.arange(period)
    c_minus_r = np.where(k < band, k, k - period)
    line = rel_bias.astype(F32)[:, np.clip(A_PAD - c_minus_r, -REL_CLIP, REL_CLIP) + REL_CLIP]
    flat = jnp.tile(line, (1, tq))[:, :tq * (period - 1)]
    return flat.reshape(rel_bias.shape[0], tq, period - 1)[:, :, :band]


def _even_layer(h, w_in, rel_bias, w_out, norm_pre, norm_post):
    s = h.shape[0]
    u = _rmsnorm(h, norm_pre)
    b = BRANCH
    log2_scale = HEAD_DIM ** -0.5 * LOG2E
    w_rows = jnp.concatenate([w_in[:, :3 * b], w_in[:, 4 * b:5 * b] * log2_scale, w_in[:, 5 * b:6 * b]],
                             axis=1).astype(BF16)
    w_g = jnp.concatenate([w_in[:, 3 * b:4 * b], w_in[:, 7 * b:]], axis=1).astype(BF16)
    rows = _matmul(u, w_rows, BF16)
    gates = _matmul(u, w_g, F32)
    vt_b = _matmul_t(w_in[:, 6 * b:7 * b].T.astype(BF16), u, BF16)
    tq_a = min(256, s)
    kv_pad = jnp.pad(rows[:, b:3 * b], ((A_PAD, 0), (0, 0)))
    y_a = _mixer_a(rows, 0, kv_pad, 0, N_HEADS, gates, 0, _rel_bias_tile(rel_bias, tq_a), tq_a)
    y_b = _mixer_b(rows, 3 * N_HEADS, 4 * N_HEADS, vt_b, 0, gates, N_HEADS, min(512, s), min(256, s))
    return _out_proj(y_a, y_b, w_out.astype(BF16), h, norm_post)


def _odd_layer(h, positions, w_in, forget_bias, lq1, lk1, lq2, lk2, subln, w_out, norm_pre, norm_post,
               lambda_init):
    s, d = h.shape
    u = _rmsnorm(h, norm_pre)
    b = BRANCH
    f0 = 4 * b
    d0 = f0 + N_HEADS
    v0 = d0 + 4 * N_HEADS * D_QK
    w_rows = jnp.concatenate([w_in[:, :b] * (HEAD_DIM ** -0.5 * LOG2E), w_in[:, b:2 * b]],
                             axis=1).astype(BF16)
    w_g = jnp.concatenate([w_in[:, 3 * b:4 * b], w_in[:, v0 + b:]], axis=1).astype(BF16)
    w_vt = jnp.concatenate([w_in[:, 2 * b:3 * b], w_in[:, v0:v0 + b]], axis=1).T.astype(BF16)
    wd = w_in[:, d0:v0].reshape(d, 4, N_HEADS, D_QK)
    w_qk = jnp.concatenate([
        jnp.concatenate([wd[:, 0], wd[:, 1]], axis=2).reshape(d, b) * (D_QK ** -0.5 * LOG2E),
        jnp.concatenate([wd[:, 2], wd[:, 3]], axis=2).reshape(d, b),
    ], axis=1).astype(BF16)
    wf_t = jnp.pad(w_in[:, f0:d0].T, ((0, N_HEADS), (0, 0))).astype(BF16)

    rows = _matmul(u, w_rows, BF16)
    gates = _matmul(u, w_g, F32)
    vt = _matmul_t(w_vt, u, BF16)
    qk_d = _matmul(u, w_qk, F32)

    cum = _cumsum_seq(_forget_log_gates(wf_t, u, forget_bias))
    qx, kx = _decay_features(cum.reshape(N_HEADS, s, 1))
    t = min(512, s)
    y_c = _mixer_c(rows, 0, N_HEADS, qx, kx, vt, 0, gates, 0, t)

    lane = jnp.arange(128) % D_QK
    inv_freq = ROPE_THETA ** (-jnp.arange(0, ROPE_DIM, 2, dtype=F32) / ROPE_DIM)
    invf = jnp.where(lane < ROPE_DIM, inv_freq[lane % (ROPE_DIM // 2)], 0.0).astype(F32).reshape(1, 128)
    qk_rot = _rope(qk_d, positions.reshape(s, 1), invf)
    lam_vecs = jnp.stack([lq1, lk1, lq2, lk2]).astype(F32)
    y_d = _mixer_d(qk_rot, 0, N_HEADS, vt, N_HEADS, gates, N_HEADS, lam_vecs, subln, lambda_init, t)
    return _out_proj(y_c, y_d, w_out.astype(BF16), h, norm_post)


def kernel(x, positions, even_w_in, even_rel_bias, even_w_out, even_norm_pre, even_norm_post,
           odd_w_in, odd_forget_bias, odd_lambda_q1, odd_lambda_k1, odd_lambda_q2, odd_lambda_k2,
           odd_subln, odd_w_out, odd_norm_pre, odd_norm_post):
    assert x.shape[0] == 1
    h = x[0]
    depth = even_w_in.shape[0] + odd_w_in.shape[0]
    for layer in range(depth):
        i = layer // 2
        if layer % 2 == 0:
            h = _even_layer(h, even_w_in[i], even_rel_bias[i], even_w_out[i],
                            even_norm_pre[i], even_norm_post[i])
        else:
            lambda_init = 0.8 - 0.6 * math.exp(-0.3 * layer)
            h = _odd_layer(h, positions, odd_w_in[i], odd_forget_bias[i],
                           odd_lambda_q1[i], odd_lambda_k1[i], odd_lambda_q2[i], odd_lambda_k2[i],
                           odd_subln[i], odd_w_out[i], odd_norm_pre[i], odd_norm_post[i], lambda_init)
    return h[None]
```

```python
import functools
import math

import jax
import jax.numpy as jnp
import numpy as np
from jax import lax
from jax.experimental import pallas as pl
from jax.experimental.pallas import tpu as pltpu

F32 = jnp.float32
BF16 = jnp.bfloat16

HEAD_DIM = 128
N_HEADS = 8
BRANCH = N_HEADS * HEAD_DIM
CHUNK = 64
CHUNK_SHIFT = 6
A_LEFT_CHUNKS = 8
A_PAD = A_LEFT_CHUNKS * CHUNK
REL_CLIP = 128
D_QK = 64
ROPE_THETA = 500000.0
ROPE_DIM = 16
EPS = 1e-6
NEG_INF = float("-inf")
LOG2E = math.log2(math.e)
BF16_SUBLANES = 16
SWEEP_STEPS = 4
VMEM_LIMIT_BYTES = 48 * 1024 * 1024

_NT = (((1,), (1,)), ((), ()))


def _params(*semantics):
    return pltpu.CompilerParams(dimension_semantics=semantics, vmem_limit_bytes=VMEM_LIMIT_BYTES)


def _silu(g):
    return g * jax.nn.sigmoid(g)


def _log_sigmoid(z):
    return jnp.minimum(z, 0.0) - jnp.log1p(jnp.exp(-jnp.abs(z)))


def _rmsnorm_kernel(x_ref, g_ref, o_ref):
    x = x_ref[...]
    ms = jnp.mean(x * x, axis=-1, keepdims=True)
    o_ref[...] = (x * lax.rsqrt(ms + EPS) * g_ref[...]).astype(o_ref.dtype)


def _rmsnorm(x, g):
    s, d = x.shape
    tm = min(512, s)
    return pl.pallas_call(
        _rmsnorm_kernel,
        out_shape=jax.ShapeDtypeStruct((s, d), BF16),
        grid=(s // tm,),
        in_specs=[pl.BlockSpec((tm, d), lambda i: (i, 0)),
                  pl.BlockSpec((1, d), lambda i: (0, 0))],
        out_specs=pl.BlockSpec((tm, d), lambda i: (i, 0)),
        compiler_params=_params("parallel"),
        name="rmsnorm",
    )(x, g.reshape(1, d))


def _matmul_kernel(x_ref, w_ref, o_ref):
    o_ref[...] = jnp.dot(x_ref[...], w_ref[...], preferred_element_type=F32).astype(o_ref.dtype)


def _matmul(x, w, out_dtype):
    m, k = x.shape
    n = w.shape[1]
    tm = min(512, m)
    tn = 1024 if n % 1024 == 0 else n
    return pl.pallas_call(
        _matmul_kernel,
        out_shape=jax.ShapeDtypeStruct((m, n), out_dtype),
        grid=(n // tn, m // tm),
        in_specs=[pl.BlockSpec((tm, k), lambda j, i: (i, 0)),
                  pl.BlockSpec((k, tn), lambda j, i: (0, j))],
        out_specs=pl.BlockSpec((tm, tn), lambda j, i: (i, j)),
        compiler_params=_params("parallel", "parallel"),
        name="in_proj",
    )(x, w)


def _matmul_t_kernel(w_ref, x_ref, o_ref):
    o_ref[...] = lax.dot_general(w_ref[...], x_ref[...], _NT,
                                 preferred_element_type=F32).astype(o_ref.dtype)


def _matmul_t(w_t, x, out_dtype):
    n, k = w_t.shape
    m = x.shape[0]
    tm = min(512, m)
    tn = 1024 if n % 1024 == 0 else n
    return pl.pallas_call(
        _matmul_t_kernel,
        out_shape=jax.ShapeDtypeStruct((n, m), out_dtype),
        grid=(n // tn, m // tm),
        in_specs=[pl.BlockSpec((tn, k), lambda j, i: (j, 0)),
                  pl.BlockSpec((tm, k), lambda j, i: (i, 0))],
        out_specs=pl.BlockSpec((tn, tm), lambda j, i: (j, i)),
        compiler_params=_params("parallel", "parallel"),
        name="in_proj_t",
    )(w_t, x)


def _mixer_a_kernel(q_ref, k_ref, v_ref, g_ref, b_ref, o_ref, *, tq, band, scale):
    i = pl.program_id(1)
    start = pl.multiple_of(i * tq, tq)
    k = k_ref[pl.ds(start, band), :]
    v = v_ref[pl.ds(start, band), :]
    s = lax.dot_general(q_ref[...], k, _NT, preferred_element_type=F32) * scale + b_ref[0]
    r = lax.broadcasted_iota(jnp.int32, (tq, band), 0)
    c = lax.broadcasted_iota(jnp.int32, (tq, band), 1)
    qc = r >> CHUNK_SHIFT
    kc = c >> CHUNK_SHIFT
    valid = (kc >= qc) & (kc <= qc + A_LEFT_CHUNKS) & (c + start >= A_PAD)
    s = jnp.where(valid, s, NEG_INF)
    m = jnp.max(s, axis=1, keepdims=True)
    p = jnp.exp(s - m)
    l = jnp.sum(p, axis=1, keepdims=True)
    o = jnp.dot(p.astype(BF16), v, preferred_element_type=F32) / l
    o_ref[...] = (o * _silu(g_ref[...])).astype(o_ref.dtype)


def _mixer_a(q_arr, q_col, kv_pad, k_col, v_col, g_arr, g_col, bias_tile, tq):
    s = q_arr.shape[0]
    sp = kv_pad.shape[0]
    band = A_PAD + tq
    kern = functools.partial(_mixer_a_kernel, tq=tq, band=band, scale=HEAD_DIM ** -0.5)
    return pl.pallas_call(
        kern,
        out_shape=jax.ShapeDtypeStruct((s, BRANCH), BF16),
        grid=(N_HEADS, s // tq),
        in_specs=[pl.BlockSpec((tq, HEAD_DIM), lambda h, i: (i, q_col + h)),
                  pl.BlockSpec((sp, HEAD_DIM), lambda h, i: (0, k_col + h)),
                  pl.BlockSpec((sp, HEAD_DIM), lambda h, i: (0, v_col + h)),
                  pl.BlockSpec((tq, HEAD_DIM), lambda h, i: (i, g_col + h)),
                  pl.BlockSpec((1, tq, band), lambda h, i: (h, 0, 0))],
        out_specs=pl.BlockSpec((tq, HEAD_DIM), lambda h, i: (i, h)),
        compiler_params=_params("parallel", "parallel"),
        name="mixer_a",
    )(q_arr, kv_pad, kv_pad, g_arr, bias_tile)


def _key_query_index(tk, tq, key_off):
    key = lax.broadcasted_iota(jnp.int32, (tk, tq), 0) + key_off
    query = lax.broadcasted_iota(jnp.int32, (tk, tq), 1)
    return key, query


def _softmax_step(s, masked_out, v_aug, m_sc, acc_sc):
    if masked_out is not None:
        s = jnp.where(masked_out, NEG_INF, s)
    m_old = m_sc[...]
    m_new = jnp.maximum(m_old, jnp.max(s, axis=0, keepdims=True))
    p = jnp.exp2(s - m_new).astype(BF16)
    alpha = jnp.exp2(m_old - m_new)
    acc_sc[...] = alpha * acc_sc[...] + jnp.dot(v_aug, p, preferred_element_type=F32)
    m_sc[...] = m_new


def _init_softmax_state(m_sc, acc_sc):
    m_sc[...] = jnp.full_like(m_sc, NEG_INF)
    acc_sc[...] = jnp.zeros_like(acc_sc)


def _normalised_rows(acc_sc):
    acc = acc_sc[...]
    return (acc[:HEAD_DIM] / acc[HEAD_DIM:HEAD_DIM + 1]).T


def _ones_rows(width):
    return jnp.ones((BF16_SUBLANES, width), BF16)


def _pipelined_sweep(head, count, rest_block, scores_into, consume, s_bufs,
                     accumulate=None, w_bufs=(None, None), count_is_even=False):
    n_head = len(head)
    last_head = head[-1][0]
    defer_accumulate = accumulate is not None

    def step(par, block, mask, prev_block, next_block):
        if next_block is not None:
            scores_into(s_bufs[1 - par], next_block)
        consume(s_bufs[par], w_bufs[par], block, mask)
        if defer_accumulate and prev_block is not None:
            accumulate(w_bufs[1 - par], prev_block)

    def rest_prev(n):
        return jnp.where(n == 0, last_head, rest_block(n - 1))

    scores_into(s_bufs[0], head[0][0])
    for g, (block, mask) in enumerate(head):
        step(g % 2, block, mask, head[g - 1][0] if g else None,
             head[g + 1][0] if g + 1 < n_head else rest_block(0))

    def body_of(steps):
        def body(_, first):
            for u in range(steps):
                n = first + u
                step((n_head + u) % 2, rest_block(n), None,
                     rest_block(n - 1) if u else rest_prev(n), rest_block(n + 1))
            return first + steps
        return body

    done = lax.fori_loop(0, count // SWEEP_STEPS, body_of(SWEEP_STEPS), 0)
    lax.fori_loop(0, (count - done) // 2, body_of(2), done)
    last_block = rest_prev(count)
    if count_is_even:
        if defer_accumulate:
            accumulate(w_bufs[(n_head - 1) % 2], last_block)
        return

    @pl.when(count % 2 == 1)
    def _():
        n = count - 1
        step(n_head % 2, rest_block(n), None, rest_prev(n), None)
        if defer_accumulate:
            accumulate(w_bufs[n_head % 2], last_block)

    if defer_accumulate:
        @pl.when(count % 2 == 0)
        def _():
            accumulate(w_bufs[(n_head - 1) % 2], last_block)


def _mixer_b_kernel(q_ref, k_ref, vt_ref, g_ref, o_ref, acc_sc, run_sc, s0_sc, s1_sc,
                    w0_sc, c0_sc, w1_sc, c1_sc, *, tq, tk):
    assert tq == 2 * tk
    i = pl.program_id(1)
    q = q_ref[...]
    kk = lax.broadcasted_iota(jnp.int32, (tk + BF16_SUBLANES, tk), 0)
    jj = lax.broadcasted_iota(jnp.int32, (tk + BF16_SUBLANES, tk), 1)
    suffix = jnp.where((jj > kk) | (kk >= tk), 1.0, 0.0).astype(BF16)
    acc_sc[...] = jnp.zeros_like(acc_sc)
    run_sc[...] = jnp.zeros_like(run_sc)

    def scores_into(buf, j):
        off = pl.multiple_of(j * tk, tk)
        buf[...] = lax.dot_general(k_ref[pl.ds(off, tk), :], q, _NT, preferred_element_type=F32)

    def weights_from(buf, w_buf, j, key_off):
        w_ref, carried_ref = w_buf
        z = buf[...]
        neg_abs = lax.bitcast_convert_type(
            lax.bitcast_convert_type(z, jnp.uint32) | jnp.uint32(0x80000000), F32)
        log_beta = jnp.minimum(z, 0.0) - jnp.log(1.0 + jnp.exp2(neg_abs)) * LOG2E
        log_keep = log_beta - z
        if key_off is not None:
            key, query = _key_query_index(tk, tq, key_off)
            strict = key < query
            log_keep = jnp.where(strict, log_keep, 0.0)
        later = jnp.dot(suffix, log_keep.astype(BF16), preferred_element_type=F32)
        w = jnp.exp2(log_beta + later[:tk])
        if key_off is not None:
            w = jnp.where(strict, w, 0.0)
        w_ref[...] = w.astype(BF16)
        carried_ref[...] = jnp.exp2(run_sc[...])
        run_sc[...] += later[tk:tk + 1]

    def accumulate(w_buf, j):
        w_ref, carried_ref = w_buf
        off = pl.multiple_of(j * tk, tk)
        acc_sc[...] += carried_ref[...] * jnp.dot(vt_ref[:, pl.ds(off, tk)], w_ref[...],
                                                  preferred_element_type=F32)

    def below(n):
        return jnp.clip(2 * i - 1 - n, 0, 2 * i)

    _pipelined_sweep([(2 * i + 1, tk), (2 * i, 0)], 2 * i, below, scores_into, weights_from,
                     (s0_sc, s1_sc), accumulate, ((w0_sc, c0_sc), (w1_sc, c1_sc)), count_is_even=True)
    o_ref[...] = (acc_sc[...].T * _silu(g_ref[...])).astype(o_ref.dtype)


def _mixer_b(qk, q_col, k_col, vt, v_row, g_arr, g_col, tq, tk):
    s = qk.shape[0]
    kern = functools.partial(_mixer_b_kernel, tq=tq, tk=tk)
    return pl.pallas_call(
        kern,
        out_shape=jax.ShapeDtypeStruct((s, BRANCH), BF16),
        grid=(N_HEADS, s // tq),
        in_specs=[pl.BlockSpec((tq, HEAD_DIM), lambda h, i: (i, q_col + h)),
                  pl.BlockSpec((s, HEAD_DIM), lambda h, i: (0, k_col + h)),
                  pl.BlockSpec((HEAD_DIM, s), lambda h, i: (v_row + h, 0)),
                  pl.BlockSpec((tq, HEAD_DIM), lambda h, i: (i, g_col + h))],
        out_specs=pl.BlockSpec((tq, HEAD_DIM), lambda h, i: (i, h)),
        scratch_shapes=[pltpu.VMEM((HEAD_DIM, tq), F32), pltpu.VMEM((1, tq), F32),
                        pltpu.VMEM((tk, tq), F32), pltpu.VMEM((tk, tq), F32),
                        pltpu.VMEM((tk, tq), BF16), pltpu.VMEM((1, tq), F32),
                        pltpu.VMEM((tk, tq), BF16), pltpu.VMEM((1, tq), F32)],
        compiler_params=_params("parallel", "parallel"),
        name="mixer_b",
    )(qk, qk, vt, g_arr)


def _forget_kernel(wf_ref, u_ref, b_ref, o_ref):
    f = lax.dot_general(wf_ref[...], u_ref[...], _NT, preferred_element_type=F32)
    o_ref[...] = _log_sigmoid(f[:N_HEADS] + b_ref[...])


def _forget_log_gates(wf_t, u, bias):
    s, d = u.shape
    tm = min(1024, s)
    return pl.pallas_call(
        _forget_kernel,
        out_shape=jax.ShapeDtypeStruct((N_HEADS, s), F32),
        grid=(s // tm,),
        in_specs=[pl.BlockSpec((2 * N_HEADS, d), lambda i: (0, 0)),
                  pl.BlockSpec((tm, d), lambda i: (i, 0)),
                  pl.BlockSpec((N_HEADS, 1), lambda i: (0, 0))],
        out_specs=pl.BlockSpec((N_HEADS, tm), lambda i: (0, i)),
        compiler_params=_params("parallel"),
        name="forget_gates",
    )(wf_t, u, bias.reshape(N_HEADS, 1))


def _cumsum_kernel(x_ref, o_ref, *, nb_shift):
    x = x_ref[...]
    nrows = x.shape[0]
    r = lax.broadcasted_iota(jnp.int32, (128, 128), 0)
    c = lax.broadcasted_iota(jnp.int32, (128, 128), 1)
    upto = jnp.where(r <= c, 1.0, 0.0).astype(F32)
    within = jnp.dot(x, upto, preferred_element_type=F32, precision=lax.Precision.HIGHEST)
    row_sum = jnp.broadcast_to(jnp.sum(x, axis=1, keepdims=True), (nrows, 128))
    rr = lax.broadcasted_iota(jnp.int32, (nrows, nrows), 0)
    cc = lax.broadcasted_iota(jnp.int32, (nrows, nrows), 1)
    earlier = jnp.where(((rr >> nb_shift) == (cc >> nb_shift)) & (cc < rr), 1.0, 0.0).astype(F32)
    o_ref[...] = within + jnp.dot(earlier, row_sum, preferred_element_type=F32,
                                  precision=lax.Precision.HIGHEST)


def _cumsum_seq(log_f):
    h, s = log_f.shape
    nb = s // 128
    nb_shift = nb.bit_length() - 1
    assert nb == 1 << nb_shift
    x = log_f.reshape(h * nb, 128)
    out = pl.pallas_call(
        functools.partial(_cumsum_kernel, nb_shift=nb_shift),
        out_shape=jax.ShapeDtypeStruct(x.shape, F32),
        compiler_params=pltpu.CompilerParams(vmem_limit_bytes=VMEM_LIMIT_BYTES),
        name="forget_cumsum",
    )(x)
    return out.reshape(h, s)


def _decay_features_kernel(c_ref, qx_ref, kx_ref):
    c = c_ref[0] * LOG2E
    hi = c.astype(BF16)
    r1 = c - hi.astype(F32)
    mid = r1.astype(BF16)
    lo = (r1 - mid.astype(F32)).astype(BF16)
    shape = qx_ref.shape
    lane = lax.broadcasted_iota(jnp.int32, shape, 1)
    parts = [jnp.broadcast_to(t.astype(F32), shape) for t in (hi, mid, lo)]
    one = jnp.ones(shape, F32)
    zero = jnp.zeros(shape, F32)
    qx = jnp.where(lane == 0, parts[0], jnp.where(lane == 1, parts[1], jnp.where(lane == 2, parts[2],
                   jnp.where(lane < 6, one, zero))))
    kx = jnp.where(lane < 3, one, jnp.where(lane == 3, -parts[0], jnp.where(lane == 4, -parts[1],
                   jnp.where(lane == 5, -parts[2], zero))))
    qx_ref[...] = qx.astype(BF16)
    kx_ref[...] = kx.astype(BF16)


def _decay_features(cum_col):
    h, s, _ = cum_col.shape
    tm = min(1024, s)
    out = jax.ShapeDtypeStruct((s, h * HEAD_DIM), BF16)
    spec = pl.BlockSpec((tm, HEAD_DIM), lambda hh, i: (i, hh))
    return pl.pallas_call(
        _decay_features_kernel,
        out_shape=(out, out),
        grid=(h, s // tm),
        in_specs=[pl.BlockSpec((1, tm, 1), lambda hh, i: (hh, i, 0))],
        out_specs=(spec, spec),
        compiler_params=_params("parallel", "parallel"),
        name="decay_features",
    )(cum_col)


def _mixer_c_kernel(q_ref, qx_ref, k_ref, kx_ref, vt_ref, g_ref, o_ref, m_sc, acc_sc, s0_sc, s1_sc, *, t):
    i = pl.program_id(1)
    q = jnp.concatenate([q_ref[...], qx_ref[...]], axis=1)
    ones = _ones_rows(t)
    _init_softmax_state(m_sc, acc_sc)

    def scores_into(buf, j):
        off = pl.multiple_of(j * t, t)
        k = jnp.concatenate([k_ref[pl.ds(off, t), :], kx_ref[pl.ds(off, t), :]], axis=1)
        buf[...] = lax.dot_general(k, q, _NT, preferred_element_type=F32)

    def consume(buf, _, j, masked_out):
        off = pl.multiple_of(j * t, t)
        v_aug = jnp.concatenate([vt_ref[:, pl.ds(off, t)], ones], axis=0)
        _softmax_step(buf[...], masked_out, v_aug, m_sc, acc_sc)

    key, query = _key_query_index(t, t, 0)
    _pipelined_sweep([(i, key > query)], i, lambda n: jnp.clip(n, 0, i), scores_into, consume,
                     (s0_sc, s1_sc))
    o_ref[...] = (_normalised_rows(acc_sc) * _silu(g_ref[...])).astype(o_ref.dtype)


def _mixer_c(qk, q_col, k_col, qx, kx, vt, v_row, g_arr, g_col, t):
    s = qk.shape[0]
    return pl.pallas_call(
        functools.partial(_mixer_c_kernel, t=t),
        out_shape=jax.ShapeDtypeStruct((s, BRANCH), BF16),
        grid=(N_HEADS, s // t),
        in_specs=[pl.BlockSpec((t, HEAD_DIM), lambda h, i: (i, q_col + h)),
                  pl.BlockSpec((t, HEAD_DIM), lambda h, i: (i, h)),
                  pl.BlockSpec((s, HEAD_DIM), lambda h, i: (0, k_col + h)),
                  pl.BlockSpec((s, HEAD_DIM), lambda h, i: (0, h)),
                  pl.BlockSpec((HEAD_DIM, s), lambda h, i: (v_row + h, 0)),
                  pl.BlockSpec((t, HEAD_DIM), lambda h, i: (i, g_col + h))],
        out_specs=pl.BlockSpec((t, HEAD_DIM), lambda h, i: (i, h)),
        scratch_shapes=[pltpu.VMEM((1, t), F32),
                        pltpu.VMEM((HEAD_DIM + BF16_SUBLANES, t), F32),
                        pltpu.VMEM((t, t), F32), pltpu.VMEM((t, t), F32)],
        compiler_params=_params("parallel", "parallel"),
        name="mixer_c",
    )(qk, qx, qk, kx, vt, g_arr)


def _rope_kernel(x_ref, pos_ref, invf_ref, o_ref, *, groups):
    ang = pos_ref[...].astype(F32) * invf_ref[...]
    cos = jnp.cos(ang)
    sin = jnp.sin(ang)
    lane = lax.broadcasted_iota(jnp.int32, ang.shape, 1)
    first_half = (lane & (D_QK - 1)) < ROPE_DIM // 2
    for gidx in range(groups):
        x = x_ref[:, gidx * 128:(gidx + 1) * 128]
        partner = jnp.where(first_half,
                            -pltpu.roll(x, 128 - ROPE_DIM // 2, 1),
                            pltpu.roll(x, ROPE_DIM // 2, 1))
        o_ref[:, gidx * 128:(gidx + 1) * 128] = (x * cos + partner * sin).astype(o_ref.dtype)


def _rope(x, positions, invf):
    s, n = x.shape
    tm = min(512, s)
    return pl.pallas_call(
        functools.partial(_rope_kernel, groups=n // 128),
        out_shape=jax.ShapeDtypeStruct((s, n), BF16),
        grid=(s // tm,),
        in_specs=[pl.BlockSpec((tm, n), lambda i: (i, 0)),
                  pl.BlockSpec((tm, 1), lambda i: (i, 0)),
                  pl.BlockSpec((1, 128), lambda i: (0, 0))],
        out_specs=pl.BlockSpec((tm, n), lambda i: (i, 0)),
        compiler_params=_params("parallel"),
        name="rope",
    )(x, positions, invf)


def _mixer_d_kernel(q_ref, k_ref, vt_ref, g_ref, lam_ref, sub_ref, o_ref,
                    m1_sc, a1_sc, m2_sc, a2_sc, sa1_sc, sa2_sc, sb1_sc, sb2_sc, *, t, lambda_init):
    i = pl.program_id(1)
    q = q_ref[...]
    lane = lax.broadcasted_iota(jnp.int32, q.shape, 1)
    zero = jnp.zeros_like(q)
    q1 = jnp.where(lane < D_QK, q, zero)
    q2 = jnp.where(lane >= D_QK, q, zero)
    ones = _ones_rows(t)
    _init_softmax_state(m1_sc, a1_sc)
    _init_softmax_state(m2_sc, a2_sc)

    def scores_into(bufs, j):
        off = pl.multiple_of(j * t, t)
        k = k_ref[pl.ds(off, t), :]
        for qm, buf in zip((q1, q2), bufs):
            buf[...] = lax.dot_general(k, qm, _NT, preferred_element_type=F32)

    def consume(bufs, _, j, masked_out):
        off = pl.multiple_of(j * t, t)
        v_aug = jnp.concatenate([vt_ref[:, pl.ds(off, t)], ones], axis=0)
        for buf, m_sc, a_sc in zip(bufs, (m1_sc, m2_sc), (a1_sc, a2_sc)):
            _softmax_step(buf[...], masked_out, v_aug, m_sc, a_sc)

    key, query = _key_query_index(t, t, 0)
    _pipelined_sweep([(i, (key >> CHUNK_SHIFT) > (query >> CHUNK_SHIFT))], i, lambda n: jnp.clip(n, 0, i),
                     scores_into, consume, ((sa1_sc, sa2_sc), (sb1_sc, sb2_sc)))

    lv = lam_ref[...]
    lam = (jnp.exp(jnp.sum(lv[0:1] * lv[1:2], axis=1, keepdims=True))
           - jnp.exp(jnp.sum(lv[2:3] * lv[3:4], axis=1, keepdims=True)) + lambda_init)
    o = _normalised_rows(a1_sc) - lam * _normalised_rows(a2_sc)
    ms = jnp.mean(o * o, axis=-1, keepdims=True)
    y = o * lax.rsqrt(ms + EPS) * sub_ref[...] * (1.0 - lambda_init)
    o_ref[...] = (y * _silu(g_ref[...])).astype(o_ref.dtype)


def _mixer_d(qk_arr, q_col, k_col, vt, v_row, g_arr, g_col, lam_vecs, subln, lambda_init, t):
    s = qk_arr.shape[0]
    kern = functools.partial(_mixer_d_kernel, t=t, lambda_init=lambda_init)
    row_stat = pltpu.VMEM((1, t), F32)
    acc = pltpu.VMEM((HEAD_DIM + BF16_SUBLANES, t), F32)
    return pl.pallas_call(
        kern,
        out_shape=jax.ShapeDtypeStruct((s, BRANCH), BF16),
        grid=(N_HEADS, s // t),
        in_specs=[pl.BlockSpec((t, HEAD_DIM), lambda h, i: (i, q_col + h)),
                  pl.BlockSpec((s, HEAD_DIM), lambda h, i: (0, k_col + h)),
                  pl.BlockSpec((HEAD_DIM, s), lambda h, i: (v_row + h, 0)),
                  pl.BlockSpec((t, HEAD_DIM), lambda h, i: (i, g_col + h)),
                  pl.BlockSpec((4, D_QK), lambda h, i: (0, 0)),
                  pl.BlockSpec((1, HEAD_DIM), lambda h, i: (0, 0))],
        out_specs=pl.BlockSpec((t, HEAD_DIM), lambda h, i: (i, h)),
        scratch_shapes=[row_stat, acc, row_stat, acc] + [pltpu.VMEM((t, t), F32)] * 4,
        compiler_params=_params("parallel", "parallel"),
        name="mixer_d",
    )(qk_arr, qk_arr, vt, g_arr, lam_vecs, subln.reshape(1, HEAD_DIM))


def _out_proj_kernel(ya_ref, yb_ref, w_ref, h_ref, g_ref, o_ref):
    half = ya_ref.shape[1]
    y = (jnp.dot(ya_ref[...], w_ref[:half, :], preferred_element_type=F32)
         + jnp.dot(yb_ref[...], w_ref[half:, :], preferred_element_type=F32))
    ms = jnp.mean(y * y, axis=-1, keepdims=True)
    o_ref[...] = h_ref[...] + y * lax.rsqrt(ms + EPS) * g_ref[...]


def _out_proj(ya, yb, w, h, g):
    s, d = h.shape
    tm = min(256, s)
    return pl.pallas_call(
        _out_proj_kernel,
        out_shape=jax.ShapeDtypeStruct((s, d), F32),
        grid=(s // tm,),
        in_specs=[pl.BlockSpec((tm, BRANCH), lambda i: (i, 0)),
                  pl.BlockSpec((tm, BRANCH), lambda i: (i, 0)),
                  pl.BlockSpec((2 * BRANCH, d), lambda i: (0, 0)),
                  pl.BlockSpec((tm, d), lambda i: (i, 0)),
                  pl.BlockSpec((1, d), lambda i: (0, 0))],
        out_specs=pl.BlockSpec((tm, d), lambda i: (i, 0)),
        compiler_params=_params("parallel"),
        name="out_proj",
    )(ya, yb, w, h, g.reshape(1, d))


def _rel_bias_tile(rel_bias, tq):
    band = A_PAD + tq
    period = band + tq
    k = np.arange(period)
    c_minus_r = np.where(k < band, k, k - period)
    line = rel_bias.astype(F32)[:, np.clip(A_PAD - c_minus_r, -REL_CLIP, REL_CLIP) + REL_CLIP]
    flat = jnp.tile(line, (1, tq))[:, :tq * (period - 1)]
    return flat.reshape(rel_bias.shape[0], tq, period - 1)[:, :, :band]


def _even_layer(h, w_in, rel_bias, w_out, norm_pre, norm_post):
    s = h.shape[0]
    u = _rmsnorm(h, norm_pre)
    b = BRANCH
    log2_scale = HEAD_DIM ** -0.5 * LOG2E
    w_rows = jnp.concatenate([w_in[:, :3 * b], w_in[:, 4 * b:5 * b] * log2_scale, w_in[:, 5 * b:6 * b]],
                             axis=1).astype(BF16)
    w_g = jnp.concatenate([w_in[:, 3 * b:4 * b], w_in[:, 7 * b:]], axis=1).astype(BF16)
    rows = _matmul(u, w_rows, BF16)
    gates = _matmul(u, w_g, F32)
    vt_b = _matmul_t(w_in[:, 6 * b:7 * b].T.astype(BF16), u, BF16)
    tq_a = min(256, s)
    kv_pad = jnp.pad(rows[:, b:3 * b], ((A_PAD, 0), (0, 0)))
    y_a = _mixer_a(rows, 0, kv_pad, 0, N_HEADS, gates, 0, _rel_bias_tile(rel_bias, tq_a), tq_a)
    y_b = _mixer_b(rows, 3 * N_HEADS, 4 * N_HEADS, vt_b, 0, gates, N_HEADS, min(512, s), min(256, s))
    return _out_proj(y_a, y_b, w_out.astype(BF16), h, norm_post)


def _odd_layer(h, positions, w_in, forget_bias, lq1, lk1, lq2, lk2, subln, w_out, norm_pre, norm_post,
               lambda_init):
    s, d = h.shape
    u = _rmsnorm(h, norm_pre)
    b = BRANCH
    f0 = 4 * b
    d0 = f0 + N_HEADS
    v0 = d0 + 4 * N_HEADS * D_QK
    w_rows = jnp.concatenate([w_in[:, :b] * (HEAD_DIM ** -0.5 * LOG2E), w_in[:, b:2 * b]],
                             axis=1).astype(BF16)
    w_g = jnp.concatenate([w_in[:, 3 * b:4 * b], w_in[:, v0 + b:]], axis=1).astype(BF16)
    w_vt = jnp.concatenate([w_in[:, 2 * b:3 * b], w_in[:, v0:v0 + b]], axis=1).T.astype(BF16)
    wd = w_in[:, d0:v0].reshape(d, 4, N_HEADS, D_QK)
    w_qk = jnp.concatenate([
        jnp.concatenate([wd[:, 0], wd[:, 1]], axis=2).reshape(d, b) * (D_QK ** -0.5 * LOG2E),
        jnp.concatenate([wd[:, 2], wd[:, 3]], axis=2).reshape(d, b),
    ], axis=1).astype(BF16)
    wf_t = jnp.pad(w_in[:, f0:d0].T, ((0, N_HEADS), (0, 0))).astype(BF16)

    rows = _matmul(u, w_rows, BF16)
    gates = _matmul(u, w_g, F32)
    vt = _matmul_t(w_vt, u, BF16)
    qk_d = _matmul(u, w_qk, F32)

    cum = _cumsum_seq(_forget_log_gates(wf_t, u, forget_bias))
    qx, kx = _decay_features(cum.reshape(N_HEADS, s, 1))
    t = min(512, s)
    y_c = _mixer_c(rows, 0, N_HEADS, qx, kx, vt, 0, gates, 0, t)

    lane = jnp.arange(128) % D_QK
    inv_freq = ROPE_THETA ** (-jnp.arange(0, ROPE_DIM, 2, dtype=F32) / ROPE_DIM)
    invf = jnp.where(lane < ROPE_DIM, inv_freq[lane % (ROPE_DIM // 2)], 0.0).astype(F32).reshape(1, 128)
    qk_rot = _rope(qk_d, positions.reshape(s, 1), invf)
    lam_vecs = jnp.stack([lq1, lk1, lq2, lk2]).astype(F32)
    y_d = _mixer_d(qk_rot, 0, N_HEADS, vt, N_HEADS, gates, N_HEADS, lam_vecs, subln, lambda_init, t)
    return _out_proj(y_c, y_d, w_out.astype(BF16), h, norm_post)


def kernel(x, positions, even_w_in, even_rel_bias, even_w_out, even_norm_pre, even_norm_post,
           odd_w_in, odd_forget_bias, odd_lambda_q1, odd_lambda_k1, odd_lambda_q2, odd_lambda_k2,
           odd_subln, odd_w_out, odd_norm_pre, odd_norm_post):
    assert x.shape[0] == 1
    h = x[0]
    depth = even_w_in.shape[0] + odd_w_in.shape[0]
    for layer in range(depth):
        i = layer // 2
        if layer % 2 == 0:
            h = _even_layer(h, even_w_in[i], even_rel_bias[i], even_w_out[i],
                            even_norm_pre[i], even_norm_post[i])
        else:
            lambda_init = 0.8 - 0.6 * math.exp(-0.3 * layer)
            h = _odd_layer(h, positions, odd_w_in[i], odd_forget_bias[i],
                           odd_lambda_q1[i], odd_lambda_k1[i], odd_lambda_q2[i], odd_lambda_k2[i],
                           odd_subln[i], odd_w_out[i], odd_norm_pre[i], odd_norm_post[i], lambda_init)
    return h[None]
```

```python
import functools
import math

import jax
import jax.numpy as jnp
import numpy as np
from jax import lax
from jax.experimental import pallas as pl
from jax.experimental.pallas import tpu as pltpu

F32 = jnp.float32
BF16 = jnp.bfloat16

HEAD_DIM = 128
N_HEADS = 8
BRANCH = N_HEADS * HEAD_DIM
CHUNK = 64
CHUNK_SHIFT = 6
A_LEFT_CHUNKS = 8
A_PAD = A_LEFT_CHUNKS * CHUNK
REL_CLIP = 128
D_QK = 64
ROPE_THETA = 500000.0
ROPE_DIM = 16
EPS = 1e-6
NEG_INF = float("-inf")
LOG2E = math.log2(math.e)
BF16_SUBLANES = 16
VMEM_LIMIT_BYTES = 48 * 1024 * 1024

_NT = (((1,), (1,)), ((), ()))


def _params(*semantics):
    return pltpu.CompilerParams(dimension_semantics=semantics, vmem_limit_bytes=VMEM_LIMIT_BYTES)


def _silu(g):
    return g * jax.nn.sigmoid(g)


def _log_sigmoid(z):
    return jnp.minimum(z, 0.0) - jnp.log1p(jnp.exp(-jnp.abs(z)))


def _rmsnorm_kernel(x_ref, g_ref, o_ref):
    x = x_ref[...]
    ms = jnp.mean(x * x, axis=-1, keepdims=True)
    o_ref[...] = (x * lax.rsqrt(ms + EPS) * g_ref[...]).astype(o_ref.dtype)


def _rmsnorm(x, g):
    s, d = x.shape
    tm = min(512, s)
    return pl.pallas_call(
        _rmsnorm_kernel,
        out_shape=jax.ShapeDtypeStruct((s, d), BF16),
        grid=(s // tm,),
        in_specs=[pl.BlockSpec((tm, d), lambda i: (i, 0)),
                  pl.BlockSpec((1, d), lambda i: (0, 0))],
        out_specs=pl.BlockSpec((tm, d), lambda i: (i, 0)),
        compiler_params=_params("parallel"),
        name="rmsnorm",
    )(x, g.reshape(1, d))


def _matmul_kernel(x_ref, w_ref, o_ref):
    o_ref[...] = jnp.dot(x_ref[...], w_ref[...], preferred_element_type=F32).astype(o_ref.dtype)


def _matmul(x, w, out_dtype):
    m, k = x.shape
    n = w.shape[1]
    tm = min(512, m)
    tn = 1024 if n % 1024 == 0 else n
    return pl.pallas_call(
        _matmul_kernel,
        out_shape=jax.ShapeDtypeStruct((m, n), out_dtype),
        grid=(n // tn, m // tm),
        in_specs=[pl.BlockSpec((tm, k), lambda j, i: (i, 0)),
                  pl.BlockSpec((k, tn), lambda j, i: (0, j))],
        out_specs=pl.BlockSpec((tm, tn), lambda j, i: (i, j)),
        compiler_params=_params("parallel", "parallel"),
        name="in_proj",
    )(x, w)


def _matmul_t_kernel(w_ref, x_ref, o_ref):
    o_ref[...] = lax.dot_general(w_ref[...], x_ref[...], _NT,
                                 preferred_element_type=F32).astype(o_ref.dtype)


def _matmul_t(w_t, x, out_dtype):
    n, k = w_t.shape
    m = x.shape[0]
    tm = min(512, m)
    tn = 1024 if n % 1024 == 0 else n
    return pl.pallas_call(
        _matmul_t_kernel,
        out_shape=jax.ShapeDtypeStruct((n, m), out_dtype),
        grid=(n // tn, m // tm),
        in_specs=[pl.BlockSpec((tn, k), lambda j, i: (j, 0)),
                  pl.BlockSpec((tm, k), lambda j, i: (i, 0))],
        out_specs=pl.BlockSpec((tn, tm), lambda j, i: (j, i)),
        compiler_params=_params("parallel", "parallel"),
        name="in_proj_t",
    )(w_t, x)


def _mixer_a_kernel(q_ref, k_ref, vt_ref, g_ref, b_ref, o_ref, *, tq, band):
    ones = _ones_rows(band)
    for sub in range(q_ref.shape[0] // tq):
        rows = slice(sub * tq, (sub + 1) * tq)
        first = (pl.program_id(1) * (q_ref.shape[0] // tq) + sub) * tq - A_PAD
        start = pl.multiple_of(jnp.maximum(first, 0), tq)
        cut = pl.multiple_of(start - first, tq)
        s = (lax.dot_general(k_ref[pl.ds(start, band), :], q_ref[rows, :], _NT, preferred_element_type=F32)
             + b_ref[0, pl.ds(cut, band), :])
        m = jnp.max(s, axis=0, keepdims=True)
        p = jnp.exp2(s - m).astype(BF16)
        v_aug = jnp.concatenate([vt_ref[:, pl.ds(start, band)], ones], axis=0)
        acc = jnp.dot(v_aug, p, preferred_element_type=F32)
        o = (acc[:HEAD_DIM] / acc[HEAD_DIM:HEAD_DIM + 1]).T
        o_ref[rows, :] = (o * _silu(g_ref[rows, :])).astype(o_ref.dtype)


def _mixer_a(qk, q_col, k_col, vt, v_row, g_arr, g_col, bias_tile, tq):
    s = qk.shape[0]
    assert A_PAD % tq == 0
    band = A_PAD + tq
    tstep = min(4 * tq, s)
    kern = functools.partial(_mixer_a_kernel, tq=tq, band=band)
    return pl.pallas_call(
        kern,
        out_shape=jax.ShapeDtypeStruct((s, BRANCH), BF16),
        grid=(N_HEADS, s // tstep),
        in_specs=[pl.BlockSpec((tstep, HEAD_DIM), lambda h, i: (i, q_col + h)),
                  pl.BlockSpec((s, HEAD_DIM), lambda h, i: (0, k_col + h)),
                  pl.BlockSpec((HEAD_DIM, s), lambda h, i: (v_row + h, 0)),
                  pl.BlockSpec((tstep, HEAD_DIM), lambda h, i: (i, g_col + h)),
                  pl.BlockSpec((1, band + A_PAD, tq), lambda h, i: (h, 0, 0))],
        out_specs=pl.BlockSpec((tstep, HEAD_DIM), lambda h, i: (i, h)),
        compiler_params=_params("parallel", "parallel"),
        name="mixer_a",
    )(qk, qk, vt, g_arr, bias_tile)


def _key_query_index(tk, tq, key_off):
    key = lax.broadcasted_iota(jnp.int32, (tk, tq), 0) + key_off
    query = lax.broadcasted_iota(jnp.int32, (tk, tq), 1)
    return key, query


def _softmax_step(s, masked_out, v_aug, m_sc, acc_sc):
    if masked_out is not None:
        s = jnp.where(masked_out, NEG_INF, s)
    m_old = m_sc[...]
    m_new = jnp.maximum(m_old, jnp.max(s, axis=0, keepdims=True))
    p = jnp.exp2(s - m_new).astype(BF16)
    alpha = jnp.exp2(m_old - m_new)
    acc_sc[...] = alpha * acc_sc[...] + jnp.dot(v_aug, p, preferred_element_type=F32)
    m_sc[...] = m_new


def _init_softmax_state(m_sc, acc_sc):
    m_sc[...] = jnp.full_like(m_sc, NEG_INF)
    acc_sc[...] = jnp.zeros_like(acc_sc)


def _normalised_rows(acc_sc):
    acc = acc_sc[...]
    return (acc[:HEAD_DIM] / acc[HEAD_DIM:HEAD_DIM + 1]).T


def _ones_rows(width):
    return jnp.ones((BF16_SUBLANES, width), BF16)


def _pipelined_sweep(head, count, rest_block, scores_into, consume, s_bufs,
                     accumulate=None, w_bufs=(None, None), count_is_even=False, steps_per_body=4):
    n_head = len(head)
    last_head = head[-1][0]
    defer_accumulate = accumulate is not None

    def step(par, block, mask, prev_block, next_block):
        if next_block is not None:
            scores_into(s_bufs[1 - par], next_block)
        consume(s_bufs[par], w_bufs[par], block, mask)
        if defer_accumulate and prev_block is not None:
            accumulate(w_bufs[1 - par], prev_block)

    def rest_prev(n):
        return jnp.where(n == 0, last_head, rest_block(n - 1))

    scores_into(s_bufs[0], head[0][0])
    for g, (block, mask) in enumerate(head):
        step(g % 2, block, mask, head[g - 1][0] if g else None,
             head[g + 1][0] if g + 1 < n_head else rest_block(0))

    def body_of(steps):
        def body(_, first):
            for u in range(steps):
                n = first + u
                step((n_head + u) % 2, rest_block(n), None,
                     rest_block(n - 1) if u else rest_prev(n), rest_block(n + 1))
            return first + steps
        return body

    done = lax.fori_loop(0, count // steps_per_body, body_of(steps_per_body), 0)
    if steps_per_body > 2:
        lax.fori_loop(0, (count - done) // 2, body_of(2), done)
    last_block = rest_prev(count)
    if count_is_even:
        if defer_accumulate:
            accumulate(w_bufs[(n_head - 1) % 2], last_block)
        return

    @pl.when(count % 2 == 1)
    def _():
        n = count - 1
        step(n_head % 2, rest_block(n), None, rest_prev(n), None)
        if defer_accumulate:
            accumulate(w_bufs[n_head % 2], last_block)

    if defer_accumulate:
        @pl.when(count % 2 == 0)
        def _():
            accumulate(w_bufs[(n_head - 1) % 2], last_block)


def _mixer_b_kernel(q_ref, k_ref, vt_ref, g_ref, o_ref, acc_sc, run_sc, s0_sc, s1_sc,
                    w0_sc, c0_sc, w1_sc, c1_sc, *, tq, tk):
    assert tq == 2 * tk
    i = pl.program_id(1)
    q = q_ref[...]
    kk = lax.broadcasted_iota(jnp.int32, (tk + BF16_SUBLANES, tk), 0)
    jj = lax.broadcasted_iota(jnp.int32, (tk + BF16_SUBLANES, tk), 1)
    suffix = jnp.where((jj > kk) | (kk >= tk), 1.0, 0.0).astype(BF16)
    acc_sc[...] = jnp.zeros_like(acc_sc)
    run_sc[...] = jnp.zeros_like(run_sc)

    def scores_into(buf, j):
        off = pl.multiple_of(j * tk, tk)
        buf[...] = lax.dot_general(k_ref[pl.ds(off, tk), :], q, _NT, preferred_element_type=F32)

    def weights_from(buf, w_buf, j, key_off):
        w_ref, carried_ref = w_buf
        z = buf[...]
        neg_abs = lax.bitcast_convert_type(
            lax.bitcast_convert_type(z, jnp.uint32) | jnp.uint32(0x80000000), F32)
        log_beta = jnp.minimum(z, 0.0) - jnp.log(1.0 + jnp.exp2(neg_abs)) * LOG2E
        log_keep = log_beta - z
        if key_off is not None:
            key, query = _key_query_index(tk, tq, key_off)
            strict = key < query
            log_keep = jnp.where(strict, log_keep, 0.0)
        later = jnp.dot(suffix, log_keep.astype(BF16), preferred_element_type=F32)
        w = jnp.exp2(log_beta + later[:tk])
        if key_off is not None:
            w = jnp.where(strict, w, 0.0)
        w_ref[...] = w.astype(BF16)
        carried_ref[...] = jnp.exp2(run_sc[...])
        run_sc[...] += later[tk:tk + 1]

    def accumulate(w_buf, j):
        w_ref, carried_ref = w_buf
        off = pl.multiple_of(j * tk, tk)
        acc_sc[...] += carried_ref[...] * jnp.dot(vt_ref[:, pl.ds(off, tk)], w_ref[...],
                                                  preferred_element_type=F32)

    def below(n):
        return jnp.clip(2 * i - 1 - n, 0, 2 * i)

    _pipelined_sweep([(2 * i + 1, tk), (2 * i, 0)], 2 * i, below, scores_into, weights_from,
                     (s0_sc, s1_sc), accumulate, ((w0_sc, c0_sc), (w1_sc, c1_sc)), count_is_even=True)
    o_ref[...] = (acc_sc[...].T * _silu(g_ref[...])).astype(o_ref.dtype)


def _mixer_b(qk, q_col, k_col, vt, v_row, g_arr, g_col, tq, tk):
    s = qk.shape[0]
    kern = functools.partial(_mixer_b_kernel, tq=tq, tk=tk)
    return pl.pallas_call(
        kern,
        out_shape=jax.ShapeDtypeStruct((s, BRANCH), BF16),
        grid=(N_HEADS, s // tq),
        in_specs=[pl.BlockSpec((tq, HEAD_DIM), lambda h, i: (i, q_col + h)),
                  pl.BlockSpec((s, HEAD_DIM), lambda h, i: (0, k_col + h)),
                  pl.BlockSpec((HEAD_DIM, s), lambda h, i: (v_row + h, 0)),
                  pl.BlockSpec((tq, HEAD_DIM), lambda h, i: (i, g_col + h))],
        out_specs=pl.BlockSpec((tq, HEAD_DIM), lambda h, i: (i, h)),
        scratch_shapes=[pltpu.VMEM((HEAD_DIM, tq), F32), pltpu.VMEM((1, tq), F32),
                        pltpu.VMEM((tk, tq), F32), pltpu.VMEM((tk, tq), F32),
                        pltpu.VMEM((tk, tq), BF16), pltpu.VMEM((1, tq), F32),
                        pltpu.VMEM((tk, tq), BF16), pltpu.VMEM((1, tq), F32)],
        compiler_params=_params("parallel", "parallel"),
        name="mixer_b",
    )(qk, qk, vt, g_arr)


def _forget_kernel(wf_ref, u_ref, b_ref, o_ref):
    f = lax.dot_general(wf_ref[...], u_ref[...], _NT, preferred_element_type=F32)
    o_ref[...] = _log_sigmoid(f[:N_HEADS] + b_ref[...])


def _forget_log_gates(wf_t, u, bias):
    s, d = u.shape
    tm = min(1024, s)
    return pl.pallas_call(
        _forget_kernel,
        out_shape=jax.ShapeDtypeStruct((N_HEADS, s), F32),
        grid=(s // tm,),
        in_specs=[pl.BlockSpec((2 * N_HEADS, d), lambda i: (0, 0)),
                  pl.BlockSpec((tm, d), lambda i: (i, 0)),
                  pl.BlockSpec((N_HEADS, 1), lambda i: (0, 0))],
        out_specs=pl.BlockSpec((N_HEADS, tm), lambda i: (0, i)),
        compiler_params=_params("parallel"),
        name="forget_gates",
    )(wf_t, u, bias.reshape(N_HEADS, 1))


def _cumsum_kernel(x_ref, o_ref, *, nb_shift):
    x = x_ref[...]
    nrows = x.shape[0]
    r = lax.broadcasted_iota(jnp.int32, (128, 128), 0)
    c = lax.broadcasted_iota(jnp.int32, (128, 128), 1)
    upto = jnp.where(r <= c, 1.0, 0.0).astype(F32)
    within = jnp.dot(x, upto, preferred_element_type=F32, precision=lax.Precision.HIGHEST)
    row_sum = jnp.broadcast_to(jnp.sum(x, axis=1, keepdims=True), (nrows, 128))
    rr = lax.broadcasted_iota(jnp.int32, (nrows, nrows), 0)
    cc = lax.broadcasted_iota(jnp.int32, (nrows, nrows), 1)
    earlier = jnp.where(((rr >> nb_shift) == (cc >> nb_shift)) & (cc < rr), 1.0, 0.0).astype(F32)
    o_ref[...] = within + jnp.dot(earlier, row_sum, preferred_element_type=F32,
                                  precision=lax.Precision.HIGHEST)


def _cumsum_seq(log_f):
    h, s = log_f.shape
    nb = s // 128
    nb_shift = nb.bit_length() - 1
    assert nb == 1 << nb_shift
    x = log_f.reshape(h * nb, 128)
    out = pl.pallas_call(
        functools.partial(_cumsum_kernel, nb_shift=nb_shift),
        out_shape=jax.ShapeDtypeStruct(x.shape, F32),
        compiler_params=pltpu.CompilerParams(vmem_limit_bytes=VMEM_LIMIT_BYTES),
        name="forget_cumsum",
    )(x)
    return out.reshape(h, s)


def _decay_features_kernel(c_ref, qx_ref, kx_ref):
    c = c_ref[0] * LOG2E
    hi = c.astype(BF16)
    r1 = c - hi.astype(F32)
    mid = r1.astype(BF16)
    lo = (r1 - mid.astype(F32)).astype(BF16)
    shape = qx_ref.shape
    lane = lax.broadcasted_iota(jnp.int32, shape, 1)
    parts = [jnp.broadcast_to(t.astype(F32), shape) for t in (hi, mid, lo)]
    one = jnp.ones(shape, F32)
    zero = jnp.zeros(shape, F32)
    qx = jnp.where(lane == 0, parts[0], jnp.where(lane == 1, parts[1], jnp.where(lane == 2, parts[2],
                   jnp.where(lane < 6, one, zero))))
    kx = jnp.where(lane < 3, one, jnp.where(lane == 3, -parts[0], jnp.where(lane == 4, -parts[1],
                   jnp.where(lane == 5, -parts[2], zero))))
    qx_ref[...] = qx.astype(BF16)
    kx_ref[...] = kx.astype(BF16)


def _decay_features(cum_col):
    h, s, _ = cum_col.shape
    tm = min(1024, s)
    out = jax.ShapeDtypeStruct((s, h * HEAD_DIM), BF16)
    spec = pl.BlockSpec((tm, HEAD_DIM), lambda hh, i: (i, hh))
    return pl.pallas_call(
        _decay_features_kernel,
        out_shape=(out, out),
        grid=(h, s // tm),
        in_specs=[pl.BlockSpec((1, tm, 1), lambda hh, i: (hh, i, 0))],
        out_specs=(spec, spec),
        compiler_params=_params("parallel", "parallel"),
        name="decay_features",
    )(cum_col)


def _causal_sweep(i, tq, tk, masked_out_fn, scores_into, consume, s_bufs):
    per_q = tq // tk
    assert tq == per_q * tk and per_q in (1, 2)
    head = [(per_q * i + d, masked_out_fn(*_key_query_index(tk, tq, d * tk))) for d in range(per_q)]
    _pipelined_sweep(head, per_q * i, lambda n: jnp.clip(n, 0, per_q * i), scores_into, consume, s_bufs,
                     count_is_even=per_q == 2, steps_per_body=4 // per_q)


def _mixer_c_kernel(q_ref, qx_ref, k_ref, kx_ref, vt_ref, g_ref, o_ref, m_sc, acc_sc, s0_sc, s1_sc,
                    *, tq, tk):
    q = jnp.concatenate([q_ref[...], qx_ref[...]], axis=1)
    ones = _ones_rows(tk)
    _init_softmax_state(m_sc, acc_sc)

    def scores_into(buf, j):
        off = pl.multiple_of(j * tk, tk)
        k = jnp.concatenate([k_ref[pl.ds(off, tk), :], kx_ref[pl.ds(off, tk), :]], axis=1)
        buf[...] = lax.dot_general(k, q, _NT, preferred_element_type=F32)

    def consume(buf, _, j, masked_out):
        off = pl.multiple_of(j * tk, tk)
        v_aug = jnp.concatenate([vt_ref[:, pl.ds(off, tk)], ones], axis=0)
        _softmax_step(buf[...], masked_out, v_aug, m_sc, acc_sc)

    _causal_sweep(pl.program_id(1), tq, tk, lambda key, query: key > query, scores_into, consume,
                  (s0_sc, s1_sc))
    o_ref[...] = (_normalised_rows(acc_sc) * _silu(g_ref[...])).astype(o_ref.dtype)


def _mixer_c(qk, q_col, k_col, qx, kx, vt, v_row, g_arr, g_col, tq, tk):
    s = qk.shape[0]
    return pl.pallas_call(
        functools.partial(_mixer_c_kernel, tq=tq, tk=tk),
        out_shape=jax.ShapeDtypeStruct((s, BRANCH), BF16),
        grid=(N_HEADS, s // tq),
        in_specs=[pl.BlockSpec((tq, HEAD_DIM), lambda h, i: (i, q_col + h)),
                  pl.BlockSpec((tq, HEAD_DIM), lambda h, i: (i, h)),
                  pl.BlockSpec((s, HEAD_DIM), lambda h, i: (0, k_col + h)),
                  pl.BlockSpec((s, HEAD_DIM), lambda h, i: (0, h)),
                  pl.BlockSpec((HEAD_DIM, s), lambda h, i: (v_row + h, 0)),
                  pl.BlockSpec((tq, HEAD_DIM), lambda h, i: (i, g_col + h))],
        out_specs=pl.BlockSpec((tq, HEAD_DIM), lambda h, i: (i, h)),
        scratch_shapes=[pltpu.VMEM((1, tq), F32),
                        pltpu.VMEM((HEAD_DIM + BF16_SUBLANES, tq), F32),
                        pltpu.VMEM((tk, tq), F32), pltpu.VMEM((tk, tq), F32)],
        compiler_params=_params("parallel", "parallel"),
        name="mixer_c",
    )(qk, qx, qk, kx, vt, g_arr)


def _rope_kernel(x_ref, pos_ref, invf_ref, o_ref, *, groups):
    ang = pos_ref[...].astype(F32) * invf_ref[...]
    cos = jnp.cos(ang)
    sin = jnp.sin(ang)
    lane = lax.broadcasted_iota(jnp.int32, ang.shape, 1)
    first_half = (lane & (D_QK - 1)) < ROPE_DIM // 2
    for gidx in range(groups):
        x = x_ref[:, gidx * 128:(gidx + 1) * 128]
        partner = jnp.where(first_half,
                            -pltpu.roll(x, 128 - ROPE_DIM // 2, 1),
                            pltpu.roll(x, ROPE_DIM // 2, 1))
        o_ref[:, gidx * 128:(gidx + 1) * 128] = (x * cos + partner * sin).astype(o_ref.dtype)


def _rope(x, positions, invf):
    s, n = x.shape
    tm = min(512, s)
    return pl.pallas_call(
        functools.partial(_rope_kernel, groups=n // 128),
        out_shape=jax.ShapeDtypeStruct((s, n), BF16),
        grid=(s // tm,),
        in_specs=[pl.BlockSpec((tm, n), lambda i: (i, 0)),
                  pl.BlockSpec((tm, 1), lambda i: (i, 0)),
                  pl.BlockSpec((1, 128), lambda i: (0, 0))],
        out_specs=pl.BlockSpec((tm, n), lambda i: (i, 0)),
        compiler_params=_params("parallel"),
        name="rope",
    )(x, positions, invf)


def _mixer_d_kernel(q_ref, k_ref, vt_ref, g_ref, lam_ref, sub_ref, o_ref,
                    m1_sc, a1_sc, m2_sc, a2_sc, sa1_sc, sa2_sc, sb1_sc, sb2_sc, *, tq, tk, lambda_init):
    q = q_ref[...]
    lane = lax.broadcasted_iota(jnp.int32, q.shape, 1)
    zero = jnp.zeros_like(q)
    q1 = jnp.where(lane < D_QK, q, zero)
    q2 = jnp.where(lane >= D_QK, q, zero)
    ones = _ones_rows(tk)
    _init_softmax_state(m1_sc, a1_sc)
    _init_softmax_state(m2_sc, a2_sc)

    def scores_into(bufs, j):
        off = pl.multiple_of(j * tk, tk)
        k = k_ref[pl.ds(off, tk), :]
        for qm, buf in zip((q1, q2), bufs):
            buf[...] = lax.dot_general(k, qm, _NT, preferred_element_type=F32)

    def consume(bufs, _, j, masked_out):
        off = pl.multiple_of(j * tk, tk)
        v_aug = jnp.concatenate([vt_ref[:, pl.ds(off, tk)], ones], axis=0)
        for buf, m_sc, a_sc in zip(bufs, (m1_sc, m2_sc), (a1_sc, a2_sc)):
            _softmax_step(buf[...], masked_out, v_aug, m_sc, a_sc)

    _causal_sweep(pl.program_id(1), tq, tk,
                  lambda key, query: (key >> CHUNK_SHIFT) > (query >> CHUNK_SHIFT),
                  scores_into, consume, ((sa1_sc, sa2_sc), (sb1_sc, sb2_sc)))

    lv = lam_ref[...]
    lam = (jnp.exp(jnp.sum(lv[0:1] * lv[1:2], axis=1, keepdims=True))
           - jnp.exp(jnp.sum(lv[2:3] * lv[3:4], axis=1, keepdims=True)) + lambda_init)
    o = _normalised_rows(a1_sc) - lam * _normalised_rows(a2_sc)
    ms = jnp.mean(o * o, axis=-1, keepdims=True)
    y = o * lax.rsqrt(ms + EPS) * sub_ref[...] * (1.0 - lambda_init)
    o_ref[...] = (y * _silu(g_ref[...])).astype(o_ref.dtype)


def _mixer_d(qk_arr, q_col, k_col, vt, v_row, g_arr, g_col, lam_vecs, subln, lambda_init, tq, tk):
    s = qk_arr.shape[0]
    kern = functools.partial(_mixer_d_kernel, tq=tq, tk=tk, lambda_init=lambda_init)
    row_stat = pltpu.VMEM((1, tq), F32)
    acc = pltpu.VMEM((HEAD_DIM + BF16_SUBLANES, tq), F32)
    return pl.pallas_call(
        kern,
        out_shape=jax.ShapeDtypeStruct((s, BRANCH), BF16),
        grid=(N_HEADS, s // tq),
        in_specs=[pl.BlockSpec((tq, HEAD_DIM), lambda h, i: (i, q_col + h)),
                  pl.BlockSpec((s, HEAD_DIM), lambda h, i: (0, k_col + h)),
                  pl.BlockSpec((HEAD_DIM, s), lambda h, i: (v_row + h, 0)),
                  pl.BlockSpec((tq, HEAD_DIM), lambda h, i: (i, g_col + h)),
                  pl.BlockSpec((4, D_QK), lambda h, i: (0, 0)),
                  pl.BlockSpec((1, HEAD_DIM), lambda h, i: (0, 0))],
        out_specs=pl.BlockSpec((tq, HEAD_DIM), lambda h, i: (i, h)),
        scratch_shapes=[row_stat, acc, row_stat, acc] + [pltpu.VMEM((tk, tq), F32)] * 4,
        compiler_params=_params("parallel", "parallel"),
        name="mixer_d",
    )(qk_arr, qk_arr, vt, g_arr, lam_vecs, subln.reshape(1, HEAD_DIM))


def _out_proj_kernel(ya_ref, yb_ref, w_ref, h_ref, g_ref, o_ref):
    half = ya_ref.shape[1]
    y = (jnp.dot(ya_ref[...], w_ref[:half, :], preferred_element_type=F32)
         + jnp.dot(yb_ref[...], w_ref[half:, :], preferred_element_type=F32))
    ms = jnp.mean(y * y, axis=-1, keepdims=True)
    o_ref[...] = h_ref[...] + y * lax.rsqrt(ms + EPS) * g_ref[...]


def _out_proj(ya, yb, w, h, g):
    s, d = h.shape
    tm = min(256, s)
    return pl.pallas_call(
        _out_proj_kernel,
        out_shape=jax.ShapeDtypeStruct((s, d), F32),
        grid=(s // tm,),
        in_specs=[pl.BlockSpec((tm, BRANCH), lambda i: (i, 0)),
                  pl.BlockSpec((tm, BRANCH), lambda i: (i, 0)),
                  pl.BlockSpec((2 * BRANCH, d), lambda i: (0, 0)),
                  pl.BlockSpec((tm, d), lambda i: (i, 0)),
                  pl.BlockSpec((1, d), lambda i: (0, 0))],
        out_specs=pl.BlockSpec((tm, d), lambda i: (i, 0)),
        compiler_params=_params("parallel"),
        name="out_proj",
    )(ya, yb, w, h, g.reshape(1, d))


def _rel_bias_tile(rel_bias, tq):
    rows = 2 * A_PAD + tq
    period = rows + tq
    k = np.arange(period)
    c_minus_r = np.where(k < rows, k, k - period)
    line = rel_bias.astype(F32)[:, np.clip(A_PAD - c_minus_r, -REL_CLIP, REL_CLIP) + REL_CLIP] * LOG2E
    flat = jnp.tile(line, (1, tq))[:, :tq * (period - 1)]
    tile = flat.reshape(rel_bias.shape[0], tq, period - 1)[:, :, :rows]
    qc = np.arange(tq)[:, None] >> CHUNK_SHIFT
    kc = np.arange(rows)[None, :] >> CHUNK_SHIFT
    in_band = (kc >= qc) & (kc <= qc + A_LEFT_CHUNKS)
    return jnp.swapaxes(jnp.where(in_band, tile, NEG_INF), 1, 2)


def _even_layer(h, w_in, rel_bias, w_out, norm_pre, norm_post):
    s = h.shape[0]
    u = _rmsnorm(h, norm_pre)
    b = BRANCH
    log2_scale = HEAD_DIM ** -0.5 * LOG2E
    w_rows = jnp.concatenate([w_in[:, :b] * log2_scale, w_in[:, b:2 * b],
                              w_in[:, 4 * b:5 * b] * log2_scale, w_in[:, 5 * b:6 * b]], axis=1).astype(BF16)
    w_g = jnp.concatenate([w_in[:, 3 * b:4 * b], w_in[:, 7 * b:]], axis=1).astype(BF16)
    w_vt = jnp.concatenate([w_in[:, 2 * b:3 * b], w_in[:, 6 * b:7 * b]], axis=1).T.astype(BF16)
    rows = _matmul(u, w_rows, BF16)
    gates = _matmul(u, w_g, F32)
    vt = _matmul_t(w_vt, u, BF16)
    tq_a = min(256, s)
    y_a = _mixer_a(rows, 0, N_HEADS, vt, 0, gates, 0, _rel_bias_tile(rel_bias, tq_a), tq_a)
    y_b = _mixer_b(rows, 2 * N_HEADS, 3 * N_HEADS, vt, N_HEADS, gates, N_HEADS, min(512, s), min(256, s))
    return _out_proj(y_a, y_b, w_out.astype(BF16), h, norm_post)


def _odd_layer(h, positions, w_in, forget_bias, lq1, lk1, lq2, lk2, subln, w_out, norm_pre, norm_post,
               lambda_init):
    s, d = h.shape
    u = _rmsnorm(h, norm_pre)
    b = BRANCH
    f0 = 4 * b
    d0 = f0 + N_HEADS
    v0 = d0 + 4 * N_HEADS * D_QK
    w_rows = jnp.concatenate([w_in[:, :b] * (HEAD_DIM ** -0.5 * LOG2E), w_in[:, b:2 * b]],
                             axis=1).astype(BF16)
    w_g = jnp.concatenate([w_in[:, 3 * b:4 * b], w_in[:, v0 + b:]], axis=1).astype(BF16)
    w_vt = jnp.concatenate([w_in[:, 2 * b:3 * b], w_in[:, v0:v0 + b]], axis=1).T.astype(BF16)
    wd = w_in[:, d0:v0].reshape(d, 4, N_HEADS, D_QK)
    w_qk = jnp.concatenate([
        jnp.concatenate([wd[:, 0], wd[:, 1]], axis=2).reshape(d, b) * (D_QK ** -0.5 * LOG2E),
        jnp.concatenate([wd[:, 2], wd[:, 3]], axis=2).reshape(d, b),
    ], axis=1).astype(BF16)
    wf_t = jnp.pad(w_in[:, f0:d0].T, ((0, N_HEADS), (0, 0))).astype(BF16)

    rows = _matmul(u, w_rows, BF16)
    gates = _matmul(u, w_g, F32)
    vt = _matmul_t(w_vt, u, BF16)
    qk_d = _matmul(u, w_qk, F32)

    cum = _cumsum_seq(_forget_log_gates(wf_t, u, forget_bias))
    qx, kx = _decay_features(cum.reshape(N_HEADS, s, 1))
    tq = tk = min(512, s)
    y_c = _mixer_c(rows, 0, N_HEADS, qx, kx, vt, 0, gates, 0, tq, tk)

    lane = jnp.arange(128) % D_QK
    inv_freq = ROPE_THETA ** (-jnp.arange(0, ROPE_DIM, 2, dtype=F32) / ROPE_DIM)
    invf = jnp.where(lane < ROPE_DIM, inv_freq[lane % (ROPE_DIM // 2)], 0.0).astype(F32).reshape(1, 128)
    qk_rot = _rope(qk_d, positions.reshape(s, 1), invf)
    lam_vecs = jnp.stack([lq1, lk1, lq2, lk2]).astype(F32)
    y_d = _mixer_d(qk_rot, 0, N_HEADS, vt, N_HEADS, gates, N_HEADS, lam_vecs, subln, lambda_init, tq, tk)
    return _out_proj(y_c, y_d, w_out.astype(BF16), h, norm_post)


def kernel(x, positions, even_w_in, even_rel_bias, even_w_out, even_norm_pre, even_norm_post,
           odd_w_in, odd_forget_bias, odd_lambda_q1, odd_lambda_k1, odd_lambda_q2, odd_lambda_k2,
           odd_subln, odd_w_out, odd_norm_pre, odd_norm_post):
    assert x.shape[0] == 1
    h = x[0]
    depth = even_w_in.shape[0] + odd_w_in.shape[0]
    for layer in range(depth):
        i = layer // 2
        if layer % 2 == 0:
            h = _even_layer(h, even_w_in[i], even_rel_bias[i], even_w_out[i],
                            even_norm_pre[i], even_norm_post[i])
        else:
            lambda_init = 0.8 - 0.6 * math.exp(-0.3 * layer)
            h = _odd_layer(h, positions, odd_w_in[i], odd_forget_bias[i],
                           odd_lambda_q1[i], odd_lambda_k1[i], odd_lambda_q2[i], odd_lambda_k2[i],
                           odd_subln[i], odd_w_out[i], odd_norm_pre[i], odd_norm_post[i], lambda_init)
    return h[None]
```

```python
import functools
import math

import jax
import jax.numpy as jnp
import numpy as np
from jax import lax
from jax.experimental import pallas as pl
from jax.experimental.pallas import tpu as pltpu

F32 = jnp.float32
BF16 = jnp.bfloat16

HEAD_DIM = 128
N_HEADS = 8
BRANCH = N_HEADS * HEAD_DIM
CHUNK = 64
CHUNK_SHIFT = 6
A_LEFT_CHUNKS = 8
A_PAD = A_LEFT_CHUNKS * CHUNK
REL_CLIP = 128
D_QK = 64
ROPE_THETA = 500000.0
ROPE_DIM = 16
EPS = 1e-6
NEG_INF = float("-inf")
LOG2E = math.log2(math.e)
BF16_SUBLANES = 16
VMEM_LIMIT_BYTES = 48 * 1024 * 1024

_NT = (((1,), (1,)), ((), ()))


def _params(*semantics):
    return pltpu.CompilerParams(dimension_semantics=semantics, vmem_limit_bytes=VMEM_LIMIT_BYTES)


def _silu(g):
    return g * jax.nn.sigmoid(g)


def _log_sigmoid(z):
    return jnp.minimum(z, 0.0) - jnp.log1p(jnp.exp(-jnp.abs(z)))


def _rmsnorm_kernel(x_ref, g_ref, o_ref):
    x = x_ref[...]
    ms = jnp.mean(x * x, axis=-1, keepdims=True)
    o_ref[...] = (x * lax.rsqrt(ms + EPS) * g_ref[...]).astype(o_ref.dtype)


def _rmsnorm(x, g):
    s, d = x.shape
    tm = min(512, s)
    return pl.pallas_call(
        _rmsnorm_kernel,
        out_shape=jax.ShapeDtypeStruct((s, d), BF16),
        grid=(s // tm,),
        in_specs=[pl.BlockSpec((tm, d), lambda i: (i, 0)),
                  pl.BlockSpec((1, d), lambda i: (0, 0))],
        out_specs=pl.BlockSpec((tm, d), lambda i: (i, 0)),
        compiler_params=_params("parallel"),
        name="rmsnorm",
    )(x, g.reshape(1, d))


def _matmul_kernel(x_ref, w_ref, o_ref):
    o_ref[...] = jnp.dot(x_ref[...], w_ref[...], preferred_element_type=F32).astype(o_ref.dtype)


def _matmul(x, w, out_dtype):
    m, k = x.shape
    n = w.shape[1]
    tm = min(512, m)
    tn = 1024 if n % 1024 == 0 else n
    return pl.pallas_call(
        _matmul_kernel,
        out_shape=jax.ShapeDtypeStruct((m, n), out_dtype),
        grid=(n // tn, m // tm),
        in_specs=[pl.BlockSpec((tm, k), lambda j, i: (i, 0)),
                  pl.BlockSpec((k, tn), lambda j, i: (0, j))],
        out_specs=pl.BlockSpec((tm, tn), lambda j, i: (i, j)),
        compiler_params=_params("parallel", "parallel"),
        name="in_proj",
    )(x, w)


def _matmul_t_kernel(w_ref, x_ref, o_ref):
    o_ref[...] = lax.dot_general(w_ref[...], x_ref[...], _NT,
                                 preferred_element_type=F32).astype(o_ref.dtype)


def _matmul_t(w_t, x, out_dtype):
    n, k = w_t.shape
    m = x.shape[0]
    tm = min(512, m)
    tn = 1024 if n % 1024 == 0 else n
    return pl.pallas_call(
        _matmul_t_kernel,
        out_shape=jax.ShapeDtypeStruct((n, m), out_dtype),
        grid=(n // tn, m // tm),
        in_specs=[pl.BlockSpec((tn, k), lambda j, i: (j, 0)),
                  pl.BlockSpec((tm, k), lambda j, i: (i, 0))],
        out_specs=pl.BlockSpec((tn, tm), lambda j, i: (j, i)),
        compiler_params=_params("parallel", "parallel"),
        name="in_proj_t",
    )(w_t, x)


def _mixer_a_kernel(q_ref, k_ref, vt_ref, g_ref, b_ref, o_ref, *, tq, band):
    ones = _ones_rows(band)
    for sub in range(q_ref.shape[0] // tq):
        rows = slice(sub * tq, (sub + 1) * tq)
        first = (pl.program_id(1) * (q_ref.shape[0] // tq) + sub) * tq - A_PAD
        start = pl.multiple_of(jnp.maximum(first, 0), tq)
        cut = pl.multiple_of(start - first, tq)
        s = (lax.dot_general(k_ref[pl.ds(start, band), :], q_ref[rows, :], _NT, preferred_element_type=F32)
             + b_ref[0, pl.ds(cut, band), :])
        m = jnp.max(s, axis=0, keepdims=True)
        p = jnp.exp2(s - m).astype(BF16)
        v_aug = jnp.concatenate([vt_ref[:, pl.ds(start, band)], ones], axis=0)
        acc = jnp.dot(v_aug, p, preferred_element_type=F32)
        o = (acc[:HEAD_DIM] / acc[HEAD_DIM:HEAD_DIM + 1]).T
        o_ref[rows, :] = (o * _silu(g_ref[rows, :])).astype(o_ref.dtype)


def _mixer_a(qk, q_col, k_col, vt, v_row, g_arr, g_col, bias_tile, tq):
    s = qk.shape[0]
    assert A_PAD % tq == 0
    band = A_PAD + tq
    tstep = min(4 * tq, s)
    kern = functools.partial(_mixer_a_kernel, tq=tq, band=band)
    return pl.pallas_call(
        kern,
        out_shape=jax.ShapeDtypeStruct((s, BRANCH), BF16),
        grid=(N_HEADS, s // tstep),
        in_specs=[pl.BlockSpec((tstep, HEAD_DIM), lambda h, i: (i, q_col + h)),
                  pl.BlockSpec((s, HEAD_DIM), lambda h, i: (0, k_col + h)),
                  pl.BlockSpec((HEAD_DIM, s), lambda h, i: (v_row + h, 0)),
                  pl.BlockSpec((tstep, HEAD_DIM), lambda h, i: (i, g_col + h)),
                  pl.BlockSpec((1, band + A_PAD, tq), lambda h, i: (h, 0, 0))],
        out_specs=pl.BlockSpec((tstep, HEAD_DIM), lambda h, i: (i, h)),
        compiler_params=_params("parallel", "parallel"),
        name="mixer_a",
    )(qk, qk, vt, g_arr, bias_tile)


def _key_query_index(tk, tq, key_off):
    key = lax.broadcasted_iota(jnp.int32, (tk, tq), 0) + key_off
    query = lax.broadcasted_iota(jnp.int32, (tk, tq), 1)
    return key, query


def _softmax_step(s, masked_out, v_aug, m_sc, acc_sc):
    if masked_out is not None:
        s = jnp.where(masked_out, NEG_INF, s)
    m_old = m_sc[...]
    m_new = jnp.maximum(m_old, jnp.max(s, axis=0, keepdims=True))
    p = jnp.exp2(s - m_new).astype(BF16)
    alpha = jnp.exp2(m_old - m_new)
    acc_sc[...] = alpha * acc_sc[...] + jnp.dot(v_aug, p, preferred_element_type=F32)
    m_sc[...] = m_new


def _init_softmax_state(m_sc, acc_sc):
    m_sc[...] = jnp.full_like(m_sc, NEG_INF)
    acc_sc[...] = jnp.zeros_like(acc_sc)


def _normalised_rows(acc_sc):
    acc = acc_sc[...]
    return (acc[:HEAD_DIM] / acc[HEAD_DIM:HEAD_DIM + 1]).T


def _ones_rows(width):
    return jnp.ones((BF16_SUBLANES, width), BF16)


def _pipelined_sweep(head, count, rest_block, scores_into, consume, s_bufs,
                     accumulate=None, w_bufs=(None, None), count_is_even=False, steps_per_body=4):
    n_head = len(head)
    last_head = head[-1][0]
    defer_accumulate = accumulate is not None

    def step(par, block, mask, prev_block, next_block):
        if next_block is not None:
            scores_into(s_bufs[1 - par], next_block)
        consume(s_bufs[par], w_bufs[par], block, mask)
        if defer_accumulate and prev_block is not None:
            accumulate(w_bufs[1 - par], prev_block)

    def rest_prev(n):
        return jnp.where(n == 0, last_head, rest_block(n - 1))

    scores_into(s_bufs[0], head[0][0])
    for g, (block, mask) in enumerate(head):
        step(g % 2, block, mask, head[g - 1][0] if g else None,
             head[g + 1][0] if g + 1 < n_head else rest_block(0))

    def body_of(steps):
        def body(_, first):
            for u in range(steps):
                n = first + u
                step((n_head + u) % 2, rest_block(n), None,
                     rest_block(n - 1) if u else rest_prev(n), rest_block(n + 1))
            return first + steps
        return body

    done = lax.fori_loop(0, count // steps_per_body, body_of(steps_per_body), 0)
    if steps_per_body > 2:
        lax.fori_loop(0, (count - done) // 2, body_of(2), done)
    last_block = rest_prev(count)
    if count_is_even:
        if defer_accumulate:
            accumulate(w_bufs[(n_head - 1) % 2], last_block)
        return

    @pl.when(count % 2 == 1)
    def _():
        n = count - 1
        step(n_head % 2, rest_block(n), None, rest_prev(n), None)
        if defer_accumulate:
            accumulate(w_bufs[n_head % 2], last_block)

    if defer_accumulate:
        @pl.when(count % 2 == 0)
        def _():
            accumulate(w_bufs[(n_head - 1) % 2], last_block)


def _mixer_b_kernel(q_ref, k_ref, vt_ref, g_ref, o_ref, acc_sc, run_sc, s0_sc, s1_sc,
                    w0_sc, c0_sc, w1_sc, c1_sc, *, tq, tk):
    assert tq == 2 * tk
    i = pl.program_id(1)
    q = q_ref[...]
    kk = lax.broadcasted_iota(jnp.int32, (tk + BF16_SUBLANES, tk), 0)
    jj = lax.broadcasted_iota(jnp.int32, (tk + BF16_SUBLANES, tk), 1)
    suffix = jnp.where((jj > kk) | (kk >= tk), 1.0, 0.0).astype(BF16)
    acc_sc[...] = jnp.zeros_like(acc_sc)
    run_sc[...] = jnp.zeros_like(run_sc)

    def scores_into(buf, j):
        off = pl.multiple_of(j * tk, tk)
        buf[...] = lax.dot_general(k_ref[pl.ds(off, tk), :], q, _NT, preferred_element_type=F32)

    def weights_from(buf, w_buf, j, key_off):
        w_ref, carried_ref = w_buf
        z = buf[...]
        neg_abs = lax.bitcast_convert_type(
            lax.bitcast_convert_type(z, jnp.uint32) | jnp.uint32(0x80000000), F32)
        log_beta = jnp.minimum(z, 0.0) - jnp.log(1.0 + jnp.exp2(neg_abs)) * LOG2E
        log_keep = log_beta - z
        if key_off is not None:
            key, query = _key_query_index(tk, tq, key_off)
            strict = key < query
            log_keep = jnp.where(strict, log_keep, 0.0)
        later = jnp.dot(suffix, log_keep.astype(BF16), preferred_element_type=F32)
        w = jnp.exp2(log_beta + later[:tk])
        if key_off is not None:
            w = jnp.where(strict, w, 0.0)
        w_ref[...] = w.astype(BF16)
        carried_ref[...] = jnp.exp2(run_sc[...])
        run_sc[...] += later[tk:tk + 1]

    def accumulate(w_buf, j):
        w_ref, carried_ref = w_buf
        off = pl.multiple_of(j * tk, tk)
        acc_sc[...] += carried_ref[...] * jnp.dot(vt_ref[:, pl.ds(off, tk)], w_ref[...],
                                                  preferred_element_type=F32)

    def below(n):
        return jnp.clip(2 * i - 1 - n, 0, 2 * i)

    _pipelined_sweep([(2 * i + 1, tk), (2 * i, 0)], 2 * i, below, scores_into, weights_from,
                     (s0_sc, s1_sc), accumulate, ((w0_sc, c0_sc), (w1_sc, c1_sc)), count_is_even=True)
    o_ref[...] = (acc_sc[...].T * _silu(g_ref[...])).astype(o_ref.dtype)


def _mixer_b(qk, q_col, k_col, vt, v_row, g_arr, g_col, tq, tk):
    s = qk.shape[0]
    kern = functools.partial(_mixer_b_kernel, tq=tq, tk=tk)
    return pl.pallas_call(
        kern,
        out_shape=jax.ShapeDtypeStruct((s, BRANCH), BF16),
        grid=(N_HEADS, s // tq),
        in_specs=[pl.BlockSpec((tq, HEAD_DIM), lambda h, i: (i, q_col + h)),
                  pl.BlockSpec((s, HEAD_DIM), lambda h, i: (0, k_col + h)),
                  pl.BlockSpec((HEAD_DIM, s), lambda h, i: (v_row + h, 0)),
                  pl.BlockSpec((tq, HEAD_DIM), lambda h, i: (i, g_col + h))],
        out_specs=pl.BlockSpec((tq, HEAD_DIM), lambda h, i: (i, h)),
        scratch_shapes=[pltpu.VMEM((HEAD_DIM, tq), F32), pltpu.VMEM((1, tq), F32),
                        pltpu.VMEM((tk, tq), F32), pltpu.VMEM((tk, tq), F32),
                        pltpu.VMEM((tk, tq), BF16), pltpu.VMEM((1, tq), F32),
                        pltpu.VMEM((tk, tq), BF16), pltpu.VMEM((1, tq), F32)],
        compiler_params=_params("parallel", "parallel"),
        name="mixer_b",
    )(qk, qk, vt, g_arr)


def _decay_features_kernel(u_ref, wf_ref, b_ref, place_ref, ones_ref, qx_ref, kx_ref, carry_sc):
    tm = u_ref.shape[0]

    @pl.when(pl.program_id(0) == 0)
    def _():
        carry_sc[...] = jnp.zeros_like(carry_sc)

    log_f = _log_sigmoid(jnp.dot(u_ref[...], wf_ref[...], preferred_element_type=F32) + b_ref[...])
    r = lax.broadcasted_iota(jnp.int32, (tm, tm), 0)
    c = lax.broadcasted_iota(jnp.int32, (tm, tm), 1)
    upto = jnp.where(c <= r, 1.0, 0.0).astype(F32)
    cum = jnp.dot(upto, log_f, preferred_element_type=F32,
                  precision=lax.Precision.HIGHEST) + carry_sc[...]
    carry_sc[...] = cum[tm - 1:tm, :]

    c2 = cum * LOG2E
    hi = c2.astype(BF16)
    r1 = c2 - hi.astype(F32)
    mid = r1.astype(BF16)
    lo = (r1 - mid.astype(F32)).astype(BF16)
    placed = jnp.dot(jnp.concatenate([hi, mid, lo], axis=1), place_ref[...],
                     preferred_element_type=F32) + ones_ref[...]
    qx_ref[...] = placed[:, :BRANCH].astype(BF16)
    kx_ref[...] = placed[:, BRANCH:].astype(BF16)


def _decay_placement():
    place = np.zeros((3 * HEAD_DIM, 2 * BRANCH), np.float32)
    ones = np.zeros((1, 2 * BRANCH), np.float32)
    for head in range(N_HEADS):
        for term in range(3):
            place[term * HEAD_DIM + head, head * HEAD_DIM + term] = 1.0
            place[term * HEAD_DIM + head, BRANCH + head * HEAD_DIM + 3 + term] = -1.0
            ones[0, head * HEAD_DIM + 3 + term] = 1.0
            ones[0, BRANCH + head * HEAD_DIM + term] = 1.0
    return jnp.asarray(place, BF16), jnp.asarray(ones, F32)


def _decay_features(u, wf, bias):
    s, d = u.shape
    tm = min(512, s)
    place, ones = _decay_placement()
    out = jax.ShapeDtypeStruct((s, BRANCH), BF16)
    spec = pl.BlockSpec((tm, BRANCH), lambda i: (i, 0))
    return pl.pallas_call(
        _decay_features_kernel,
        out_shape=(out, out),
        grid=(s // tm,),
        in_specs=[pl.BlockSpec((tm, d), lambda i: (i, 0)),
                  pl.BlockSpec((d, HEAD_DIM), lambda i: (0, 0)),
                  pl.BlockSpec((1, HEAD_DIM), lambda i: (0, 0)),
                  pl.BlockSpec(place.shape, lambda i: (0, 0)),
                  pl.BlockSpec(ones.shape, lambda i: (0, 0))],
        out_specs=(spec, spec),
        scratch_shapes=[pltpu.VMEM((1, HEAD_DIM), F32)],
        compiler_params=_params("arbitrary"),
        name="decay_features",
    )(u, wf, bias, place, ones)


def _causal_sweep(i, tq, tk, masked_out_fn, scores_into, consume, s_bufs):
    per_q = tq // tk
    assert tq == per_q * tk and per_q in (1, 2)
    head = [(per_q * i + d, masked_out_fn(*_key_query_index(tk, tq, d * tk))) for d in range(per_q)]
    _pipelined_sweep(head, per_q * i, lambda n: jnp.clip(n, 0, per_q * i), scores_into, consume, s_bufs,
                     count_is_even=per_q == 2, steps_per_body=4 // per_q)


def _mixer_c_kernel(q_ref, qx_ref, k_ref, kx_ref, vt_ref, g_ref, o_ref, m_sc, acc_sc, s0_sc, s1_sc,
                    *, tq, tk):
    q = jnp.concatenate([q_ref[...], qx_ref[...]], axis=1)
    ones = _ones_rows(tk)
    _init_softmax_state(m_sc, acc_sc)

    def scores_into(buf, j):
        off = pl.multiple_of(j * tk, tk)
        k = jnp.concatenate([k_ref[pl.ds(off, tk), :], kx_ref[pl.ds(off, tk), :]], axis=1)
        buf[...] = lax.dot_general(k, q, _NT, preferred_element_type=F32)

    def consume(buf, _, j, masked_out):
        off = pl.multiple_of(j * tk, tk)
        v_aug = jnp.concatenate([vt_ref[:, pl.ds(off, tk)], ones], axis=0)
        _softmax_step(buf[...], masked_out, v_aug, m_sc, acc_sc)

    _causal_sweep(pl.program_id(1), tq, tk, lambda key, query: key > query, scores_into, consume,
                  (s0_sc, s1_sc))
    o_ref[...] = (_normalised_rows(acc_sc) * _silu(g_ref[...])).astype(o_ref.dtype)


def _mixer_c(qk, q_col, k_col, qx, kx, vt, v_row, g_arr, g_col, tq, tk):
    s = qk.shape[0]
    return pl.pallas_call(
        functools.partial(_mixer_c_kernel, tq=tq, tk=tk),
        out_shape=jax.ShapeDtypeStruct((s, BRANCH), BF16),
        grid=(N_HEADS, s // tq),
        in_specs=[pl.BlockSpec((tq, HEAD_DIM), lambda h, i: (i, q_col + h)),
                  pl.BlockSpec((tq, HEAD_DIM), lambda h, i: (i, h)),
                  pl.BlockSpec((s, HEAD_DIM), lambda h, i: (0, k_col + h)),
                  pl.BlockSpec((s, HEAD_DIM), lambda h, i: (0, h)),
                  pl.BlockSpec((HEAD_DIM, s), lambda h, i: (v_row + h, 0)),
                  pl.BlockSpec((tq, HEAD_DIM), lambda h, i: (i, g_col + h))],
        out_specs=pl.BlockSpec((tq, HEAD_DIM), lambda h, i: (i, h)),
        scratch_shapes=[pltpu.VMEM((1, tq), F32),
                        pltpu.VMEM((HEAD_DIM + BF16_SUBLANES, tq), F32),
                        pltpu.VMEM((tk, tq), F32), pltpu.VMEM((tk, tq), F32)],
        compiler_params=_params("parallel", "parallel"),
        name="mixer_c",
    )(qk, qx, qk, kx, vt, g_arr)


def _rope_kernel(x_ref, pos_ref, invf_ref, o_ref, *, groups):
    ang = pos_ref[...].astype(F32) * invf_ref[...]
    cos = jnp.cos(ang)
    sin = jnp.sin(ang)
    lane = lax.broadcasted_iota(jnp.int32, ang.shape, 1)
    first_half = (lane & (D_QK - 1)) < ROPE_DIM // 2
    for gidx in range(groups):
        x = x_ref[:, gidx * 128:(gidx + 1) * 128]
        partner = jnp.where(first_half,
                            -pltpu.roll(x, 128 - ROPE_DIM // 2, 1),
                            pltpu.roll(x, ROPE_DIM // 2, 1))
        o_ref[:, gidx * 128:(gidx + 1) * 128] = (x * cos + partner * sin).astype(o_ref.dtype)


def _rope(x, positions, invf):
    s, n = x.shape
    tm = min(512, s)
    return pl.pallas_call(
        functools.partial(_rope_kernel, groups=n // 128),
        out_shape=jax.ShapeDtypeStruct((s, n), BF16),
        grid=(s // tm,),
        in_specs=[pl.BlockSpec((tm, n), lambda i: (i, 0)),
                  pl.BlockSpec((tm, 1), lambda i: (i, 0)),
                  pl.BlockSpec((1, 128), lambda i: (0, 0))],
        out_specs=pl.BlockSpec((tm, n), lambda i: (i, 0)),
        compiler_params=_params("parallel"),
        name="rope",
    )(x, positions, invf)


def _mixer_d_kernel(q_ref, k_ref, vt_ref, g_ref, lam_ref, sub_ref, o_ref,
                    m1_sc, a1_sc, m2_sc, a2_sc, sa1_sc, sa2_sc, sb1_sc, sb2_sc, *, tq, tk, lambda_init):
    q = q_ref[...]
    lane = lax.broadcasted_iota(jnp.int32, q.shape, 1)
    zero = jnp.zeros_like(q)
    q1 = jnp.where(lane < D_QK, q, zero)
    q2 = jnp.where(lane >= D_QK, q, zero)
    ones = _ones_rows(tk)
    _init_softmax_state(m1_sc, a1_sc)
    _init_softmax_state(m2_sc, a2_sc)

    def scores_into(bufs, j):
        off = pl.multiple_of(j * tk, tk)
        k = k_ref[pl.ds(off, tk), :]
        for qm, buf in zip((q1, q2), bufs):
            buf[...] = lax.dot_general(k, qm, _NT, preferred_element_type=F32)

    def consume(bufs, _, j, masked_out):
        off = pl.multiple_of(j * tk, tk)
        v_aug = jnp.concatenate([vt_ref[:, pl.ds(off, tk)], ones], axis=0)
        for buf, m_sc, a_sc in zip(bufs, (m1_sc, m2_sc), (a1_sc, a2_sc)):
            _softmax_step(buf[...], masked_out, v_aug, m_sc, a_sc)

    _causal_sweep(pl.program_id(1), tq, tk,
                  lambda key, query: (key >> CHUNK_SHIFT) > (query >> CHUNK_SHIFT),
                  scores_into, consume, ((sa1_sc, sa2_sc), (sb1_sc, sb2_sc)))

    lv = lam_ref[...]
    lam = (jnp.exp(jnp.sum(lv[0:1] * lv[1:2], axis=1, keepdims=True))
           - jnp.exp(jnp.sum(lv[2:3] * lv[3:4], axis=1, keepdims=True)) + lambda_init)
    o = _normalised_rows(a1_sc) - lam * _normalised_rows(a2_sc)
    ms = jnp.mean(o * o, axis=-1, keepdims=True)
    y = o * lax.rsqrt(ms + EPS) * sub_ref[...] * (1.0 - lambda_init)
    o_ref[...] = (y * _silu(g_ref[...])).astype(o_ref.dtype)


def _mixer_d(qk_arr, q_col, k_col, vt, v_row, g_arr, g_col, lam_vecs, subln, lambda_init, tq, tk):
    s = qk_arr.shape[0]
    kern = functools.partial(_mixer_d_kernel, tq=tq, tk=tk, lambda_init=lambda_init)
    row_stat = pltpu.VMEM((1, tq), F32)
    acc = pltpu.VMEM((HEAD_DIM + BF16_SUBLANES, tq), F32)
    return pl.pallas_call(
        kern,
        out_shape=jax.ShapeDtypeStruct((s, BRANCH), BF16),
        grid=(N_HEADS, s // tq),
        in_specs=[pl.BlockSpec((tq, HEAD_DIM), lambda h, i: (i, q_col + h)),
                  pl.BlockSpec((s, HEAD_DIM), lambda h, i: (0, k_col + h)),
                  pl.BlockSpec((HEAD_DIM, s), lambda h, i: (v_row + h, 0)),
                  pl.BlockSpec((tq, HEAD_DIM), lambda h, i: (i, g_col + h)),
                  pl.BlockSpec((4, D_QK), lambda h, i: (0, 0)),
                  pl.BlockSpec((1, HEAD_DIM), lambda h, i: (0, 0))],
        out_specs=pl.BlockSpec((tq, HEAD_DIM), lambda h, i: (i, h)),
        scratch_shapes=[row_stat, acc, row_stat, acc] + [pltpu.VMEM((tk, tq), F32)] * 4,
        compiler_params=_params("parallel", "parallel"),
        name="mixer_d",
    )(qk_arr, qk_arr, vt, g_arr, lam_vecs, subln.reshape(1, HEAD_DIM))


def _out_proj_kernel(ya_ref, yb_ref, w_ref, h_ref, g_ref, o_ref):
    half = ya_ref.shape[1]
    y = (jnp.dot(ya_ref[...], w_ref[:half, :], preferred_element_type=F32)
         + jnp.dot(yb_ref[...], w_ref[half:, :], preferred_element_type=F32))
    ms = jnp.mean(y * y, axis=-1, keepdims=True)
    o_ref[...] = h_ref[...] + y * lax.rsqrt(ms + EPS) * g_ref[...]


def _out_proj(ya, yb, w, h, g):
    s, d = h.shape
    tm = min(256, s)
    return pl.pallas_call(
        _out_proj_kernel,
        out_shape=jax.ShapeDtypeStruct((s, d), F32),
        grid=(s // tm,),
        in_specs=[pl.BlockSpec((tm, BRANCH), lambda i: (i, 0)),
                  pl.BlockSpec((tm, BRANCH), lambda i: (i, 0)),
                  pl.BlockSpec((2 * BRANCH, d), lambda i: (0, 0)),
                  pl.BlockSpec((tm, d), lambda i: (i, 0)),
                  pl.BlockSpec((1, d), lambda i: (0, 0))],
        out_specs=pl.BlockSpec((tm, d), lambda i: (i, 0)),
        compiler_params=_params("parallel"),
        name="out_proj",
    )(ya, yb, w, h, g.reshape(1, d))


def _rel_bias_tile(rel_bias, tq):
    rows = 2 * A_PAD + tq
    period = rows + tq
    k = np.arange(period)
    c_minus_r = np.where(k < rows, k, k - period)
    line = rel_bias.astype(F32)[:, np.clip(A_PAD - c_minus_r, -REL_CLIP, REL_CLIP) + REL_CLIP] * LOG2E
    flat = jnp.tile(line, (1, tq))[:, :tq * (period - 1)]
    tile = flat.reshape(rel_bias.shape[0], tq, period - 1)[:, :, :rows]
    qc = np.arange(tq)[:, None] >> CHUNK_SHIFT
    kc = np.arange(rows)[None, :] >> CHUNK_SHIFT
    in_band = (kc >= qc) & (kc <= qc + A_LEFT_CHUNKS)
    return jnp.swapaxes(jnp.where(in_band, tile, NEG_INF), 1, 2)


def _even_layer(h, w_in, rel_bias, w_out, norm_pre, norm_post):
    s = h.shape[0]
    u = _rmsnorm(h, norm_pre)
    b = BRANCH
    log2_scale = HEAD_DIM ** -0.5 * LOG2E
    w_rows = jnp.concatenate([w_in[:, :b] * log2_scale, w_in[:, b:2 * b],
                              w_in[:, 4 * b:5 * b] * log2_scale, w_in[:, 5 * b:6 * b]], axis=1).astype(BF16)
    w_g = jnp.concatenate([w_in[:, 3 * b:4 * b], w_in[:, 7 * b:]], axis=1).astype(BF16)
    w_vt = jnp.concatenate([w_in[:, 2 * b:3 * b], w_in[:, 6 * b:7 * b]], axis=1).T.astype(BF16)
    rows = _matmul(u, w_rows, BF16)
    gates = _matmul(u, w_g, F32)
    vt = _matmul_t(w_vt, u, BF16)
    tq_a = min(256, s)
    y_a = _mixer_a(rows, 0, N_HEADS, vt, 0, gates, 0, _rel_bias_tile(rel_bias, tq_a), tq_a)
    y_b = _mixer_b(rows, 2 * N_HEADS, 3 * N_HEADS, vt, N_HEADS, gates, N_HEADS, min(512, s), min(256, s))
    return _out_proj(y_a, y_b, w_out.astype(BF16), h, norm_post)


def _odd_layer(h, positions, w_in, forget_bias, lq1, lk1, lq2, lk2, subln, w_out, norm_pre, norm_post,
               lambda_init):
    s, d = h.shape
    u = _rmsnorm(h, norm_pre)
    b = BRANCH
    f0 = 4 * b
    d0 = f0 + N_HEADS
    v0 = d0 + 4 * N_HEADS * D_QK
    w_rows = jnp.concatenate([w_in[:, :b] * (HEAD_DIM ** -0.5 * LOG2E), w_in[:, b:2 * b]],
                             axis=1).astype(BF16)
    w_g = jnp.concatenate([w_in[:, 3 * b:4 * b], w_in[:, v0 + b:]], axis=1).astype(BF16)
    w_vt = jnp.concatenate([w_in[:, 2 * b:3 * b], w_in[:, v0:v0 + b]], axis=1).T.astype(BF16)
    wd = w_in[:, d0:v0].reshape(d, 4, N_HEADS, D_QK)
    w_qk = jnp.concatenate([
        jnp.concatenate([wd[:, 0], wd[:, 1]], axis=2).reshape(d, b) * (D_QK ** -0.5 * LOG2E),
        jnp.concatenate([wd[:, 2], wd[:, 3]], axis=2).reshape(d, b),
    ], axis=1).astype(BF16)
    w_f = jnp.pad(w_in[:, f0:d0], ((0, 0), (0, HEAD_DIM - N_HEADS))).astype(BF16)
    b_f = jnp.pad(forget_bias.astype(F32), (0, HEAD_DIM - N_HEADS)).reshape(1, HEAD_DIM)

    rows = _matmul(u, w_rows, BF16)
    gates = _matmul(u, w_g, F32)
    vt = _matmul_t(w_vt, u, BF16)
    qk_d = _matmul(u, w_qk, F32)

    qx, kx = _decay_features(u, w_f, b_f)
    tq = tk = min(512, s)
    y_c = _mixer_c(rows, 0, N_HEADS, qx, kx, vt, 0, gates, 0, tq, tk)

    lane = jnp.arange(128) % D_QK
    inv_freq = ROPE_THETA ** (-jnp.arange(0, ROPE_DIM, 2, dtype=F32) / ROPE_DIM)
    invf = jnp.where(lane < ROPE_DIM, inv_freq[lane % (ROPE_DIM // 2)], 0.0).astype(F32).reshape(1, 128)
    qk_rot = _rope(qk_d, positions.reshape(s, 1), invf)
    lam_vecs = jnp.stack([lq1, lk1, lq2, lk2]).astype(F32)
    y_d = _mixer_d(qk_rot, 0, N_HEADS, vt, N_HEADS, gates, N_HEADS, lam_vecs, subln, lambda_init, tq, tk)
    return _out_proj(y_c, y_d, w_out.astype(BF16), h, norm_post)


def kernel(x, positions, even_w_in, even_rel_bias, even_w_out, even_norm_pre, even_norm_post,
           odd_w_in, odd_forget_bias, odd_lambda_q1, odd_lambda_k1, odd_lambda_q2, odd_lambda_k2,
           odd_subln, odd_w_out, odd_norm_pre, odd_norm_post):
    assert x.shape[0] == 1
    h = x[0]
    depth = even_w_in.shape[0] + odd_w_in.shape[0]
    for layer in range(depth):
        i = layer // 2
        if layer % 2 == 0:
            h = _even_layer(h, even_w_in[i], even_rel_bias[i], even_w_out[i],
                            even_norm_pre[i], even_norm_post[i])
        else:
            lambda_init = 0.8 - 0.6 * math.exp(-0.3 * layer)
            h = _odd_layer(h, positions, odd_w_in[i], odd_forget_bias[i],
                           odd_lambda_q1[i], odd_lambda_k1[i], odd_lambda_q2[i], odd_lambda_k2[i],
                           odd_subln[i], odd_w_out[i], odd_norm_pre[i], odd_norm_post[i], lambda_init)
    return h[None]
```

```python
import functools
import math

import jax
import jax.numpy as jnp
import numpy as np
from jax import lax
from jax.experimental import pallas as pl
from jax.experimental.pallas import tpu as pltpu

F32 = jnp.float32
BF16 = jnp.bfloat16

HEAD_DIM = 128
N_HEADS = 8
BRANCH = N_HEADS * HEAD_DIM
CHUNK = 64
CHUNK_SHIFT = 6
A_LEFT_CHUNKS = 8
A_PAD = A_LEFT_CHUNKS * CHUNK
REL_CLIP = 128
D_QK = 64
ROPE_THETA = 500000.0
ROPE_DIM = 16
EPS = 1e-6
NEG_INF = float("-inf")
LOG2E = math.log2(math.e)
BF16_SUBLANES = 16
VMEM_LIMIT_BYTES = 48 * 1024 * 1024

_NT = (((1,), (1,)), ((), ()))


def _params(*semantics):
    return pltpu.CompilerParams(dimension_semantics=semantics, vmem_limit_bytes=VMEM_LIMIT_BYTES)


def _silu(g):
    return g * jax.nn.sigmoid(g)


def _log_sigmoid(z):
    return jnp.minimum(z, 0.0) - jnp.log1p(jnp.exp(-jnp.abs(z)))


def _rmsnorm_kernel(x_ref, g_ref, o_ref):
    x = x_ref[...]
    ms = jnp.mean(x * x, axis=-1, keepdims=True)
    o_ref[...] = (x * lax.rsqrt(ms + EPS) * g_ref[...]).astype(o_ref.dtype)


def _rmsnorm(x, g):
    s, d = x.shape
    tm = min(512, s)
    return pl.pallas_call(
        _rmsnorm_kernel,
        out_shape=jax.ShapeDtypeStruct((s, d), BF16),
        grid=(s // tm,),
        in_specs=[pl.BlockSpec((tm, d), lambda i: (i, 0)),
                  pl.BlockSpec((1, d), lambda i: (0, 0))],
        out_specs=pl.BlockSpec((tm, d), lambda i: (i, 0)),
        compiler_params=_params("parallel"),
        name="rmsnorm",
    )(x, g.reshape(1, d))


def _matmul_kernel(x_ref, w_ref, o_ref):
    o_ref[...] = jnp.dot(x_ref[...], w_ref[...], preferred_element_type=F32).astype(o_ref.dtype)


def _matmul(x, w, out_dtype):
    m, k = x.shape
    n = w.shape[1]
    tm = min(1024, m)
    tn = 1024 if n % 1024 == 0 else n
    return pl.pallas_call(
        _matmul_kernel,
        out_shape=jax.ShapeDtypeStruct((m, n), out_dtype),
        grid=(n // tn, m // tm),
        in_specs=[pl.BlockSpec((tm, k), lambda j, i: (i, 0)),
                  pl.BlockSpec((k, tn), lambda j, i: (0, j))],
        out_specs=pl.BlockSpec((tm, tn), lambda j, i: (i, j)),
        compiler_params=_params("parallel", "parallel"),
        name="in_proj",
    )(x, w)


def _matmul_t_kernel(w_ref, x_ref, o_ref):
    o_ref[...] = lax.dot_general(w_ref[...], x_ref[...], _NT,
                                 preferred_element_type=F32).astype(o_ref.dtype)


def _matmul_t(w_t, x, out_dtype):
    n, k = w_t.shape
    m = x.shape[0]
    tm = min(1024, m)
    tn = 1024 if n % 1024 == 0 else n
    return pl.pallas_call(
        _matmul_t_kernel,
        out_shape=jax.ShapeDtypeStruct((n, m), out_dtype),
        grid=(n // tn, m // tm),
        in_specs=[pl.BlockSpec((tn, k), lambda j, i: (j, 0)),
                  pl.BlockSpec((tm, k), lambda j, i: (i, 0))],
        out_specs=pl.BlockSpec((tn, tm), lambda j, i: (j, i)),
        compiler_params=_params("parallel", "parallel"),
        name="in_proj_t",
    )(w_t, x)


def _mixer_a_kernel(q_ref, k_ref, vt_ref, g_ref, b_ref, o_ref, *, tq, band):
    ones = _ones_rows(band)
    for sub in range(q_ref.shape[0] // tq):
        rows = slice(sub * tq, (sub + 1) * tq)
        first = (pl.program_id(1) * (q_ref.shape[0] // tq) + sub) * tq - A_PAD
        start = pl.multiple_of(jnp.maximum(first, 0), tq)
        cut = pl.multiple_of(start - first, tq)
        s = (lax.dot_general(k_ref[pl.ds(start, band), :], q_ref[rows, :], _NT, preferred_element_type=F32)
             + b_ref[0, pl.ds(cut, band), :])
        m = jnp.max(s, axis=0, keepdims=True)
        p = jnp.exp2(s - m).astype(BF16)
        v_aug = jnp.concatenate([vt_ref[:, pl.ds(start, band)], ones], axis=0)
        acc = jnp.dot(v_aug, p, preferred_element_type=F32)
        o = (acc[:HEAD_DIM] / acc[HEAD_DIM:HEAD_DIM + 1]).T
        o_ref[rows, :] = (o * _silu(g_ref[rows, :])).astype(o_ref.dtype)


def _mixer_a(qk, q_col, k_col, vt, v_row, g_arr, g_col, bias_tile, tq):
    s = qk.shape[0]
    assert A_PAD % tq == 0
    band = A_PAD + tq
    tstep = min(4 * tq, s)
    kern = functools.partial(_mixer_a_kernel, tq=tq, band=band)
    return pl.pallas_call(
        kern,
        out_shape=jax.ShapeDtypeStruct((s, BRANCH), BF16),
        grid=(N_HEADS, s // tstep),
        in_specs=[pl.BlockSpec((tstep, HEAD_DIM), lambda h, i: (i, q_col + h)),
                  pl.BlockSpec((s, HEAD_DIM), lambda h, i: (0, k_col + h)),
                  pl.BlockSpec((HEAD_DIM, s), lambda h, i: (v_row + h, 0)),
                  pl.BlockSpec((tstep, HEAD_DIM), lambda h, i: (i, g_col + h)),
                  pl.BlockSpec((1, band + A_PAD, tq), lambda h, i: (h, 0, 0))],
        out_specs=pl.BlockSpec((tstep, HEAD_DIM), lambda h, i: (i, h)),
        compiler_params=_params("parallel", "parallel"),
        name="mixer_a",
    )(qk, qk, vt, g_arr, bias_tile)


def _key_query_index(tk, tq, key_off):
    key = lax.broadcasted_iota(jnp.int32, (tk, tq), 0) + key_off
    query = lax.broadcasted_iota(jnp.int32, (tk, tq), 1)
    return key, query


def _softmax_step(s, masked_out, v_aug, m_sc, acc_sc):
    if masked_out is not None:
        s = jnp.where(masked_out, NEG_INF, s)
    m_old = m_sc[...]
    m_new = jnp.maximum(m_old, jnp.max(s, axis=0, keepdims=True))
    p = jnp.exp2(s - m_new).astype(BF16)
    alpha = jnp.exp2(m_old - m_new)
    acc_sc[...] = alpha * acc_sc[...] + jnp.dot(v_aug, p, preferred_element_type=F32)
    m_sc[...] = m_new


def _init_softmax_state(m_sc, acc_sc):
    m_sc[...] = jnp.full_like(m_sc, NEG_INF)
    acc_sc[...] = jnp.zeros_like(acc_sc)


def _normalised_rows(acc_sc):
    acc = acc_sc[...]
    return (acc[:HEAD_DIM] / acc[HEAD_DIM:HEAD_DIM + 1]).T


def _ones_rows(width):
    return jnp.ones((BF16_SUBLANES, width), BF16)


def _pipelined_sweep(head, count, rest_block, scores_into, consume, s_bufs,
                     accumulate=None, w_bufs=(None, None), count_is_even=False, steps_per_body=4):
    n_head = len(head)
    last_head = head[-1][0]
    defer_accumulate = accumulate is not None

    def step(par, block, mask, prev_block, next_block):
        if next_block is not None:
            scores_into(s_bufs[1 - par], next_block)
        consume(s_bufs[par], w_bufs[par], block, mask)
        if defer_accumulate and prev_block is not None:
            accumulate(w_bufs[1 - par], prev_block)

    def rest_prev(n):
        return jnp.where(n == 0, last_head, rest_block(n - 1))

    scores_into(s_bufs[0], head[0][0])
    for g, (block, mask) in enumerate(head):
        step(g % 2, block, mask, head[g - 1][0] if g else None,
             head[g + 1][0] if g + 1 < n_head else rest_block(0))

    def body_of(steps):
        def body(_, first):
            for u in range(steps):
                n = first + u
                step((n_head + u) % 2, rest_block(n), None,
                     rest_block(n - 1) if u else rest_prev(n), rest_block(n + 1))
            return first + steps
        return body

    done = lax.fori_loop(0, count // steps_per_body, body_of(steps_per_body), 0)
    if steps_per_body > 2:
        lax.fori_loop(0, (count - done) // 2, body_of(2), done)
    last_block = rest_prev(count)
    if count_is_even:
        if defer_accumulate:
            accumulate(w_bufs[(n_head - 1) % 2], last_block)
        return

    @pl.when(count % 2 == 1)
    def _():
        n = count - 1
        step(n_head % 2, rest_block(n), None, rest_prev(n), None)
        if defer_accumulate:
            accumulate(w_bufs[n_head % 2], last_block)

    if defer_accumulate:
        @pl.when(count % 2 == 0)
        def _():
            accumulate(w_bufs[(n_head - 1) % 2], last_block)


def _mixer_b_kernel(q_ref, k_ref, vt_ref, g_ref, o_ref, acc_sc, run_sc, s0_sc, s1_sc,
                    w0_sc, c0_sc, w1_sc, c1_sc, *, tq, tk):
    per_q = tq // tk
    assert tq == per_q * tk and per_q % 2 == 0
    i = pl.program_id(1)
    q = q_ref[...]
    kk = lax.broadcasted_iota(jnp.int32, (tk + BF16_SUBLANES, tk), 0)
    jj = lax.broadcasted_iota(jnp.int32, (tk + BF16_SUBLANES, tk), 1)
    suffix = jnp.where((jj > kk) | (kk >= tk), 1.0, 0.0).astype(BF16)
    acc_sc[...] = jnp.zeros_like(acc_sc)
    run_sc[...] = jnp.zeros_like(run_sc)

    def scores_into(buf, j):
        off = pl.multiple_of(j * tk, tk)
        buf[...] = lax.dot_general(k_ref[pl.ds(off, tk), :], q, _NT, preferred_element_type=F32)

    def weights_from(buf, w_buf, j, key_off):
        w_ref, carried_ref = w_buf
        z = buf[...]
        neg_abs = lax.bitcast_convert_type(
            lax.bitcast_convert_type(z, jnp.uint32) | jnp.uint32(0x80000000), F32)
        log_beta = jnp.minimum(z, 0.0) - jnp.log(1.0 + jnp.exp2(neg_abs)) * LOG2E
        log_keep = log_beta - z
        if key_off is not None:
            key, query = _key_query_index(tk, tq, key_off)
            strict = key < query
            log_keep = jnp.where(strict, log_keep, 0.0)
        later = jnp.dot(suffix, log_keep.astype(BF16), preferred_element_type=F32)
        w = jnp.exp2(log_beta + later[:tk])
        if key_off is not None:
            w = jnp.where(strict, w, 0.0)
        w_ref[...] = w.astype(BF16)
        carried_ref[...] = jnp.exp2(run_sc[...])
        run_sc[...] += later[tk:tk + 1]

    def accumulate(w_buf, j):
        w_ref, carried_ref = w_buf
        off = pl.multiple_of(j * tk, tk)
        acc_sc[...] += carried_ref[...] * jnp.dot(vt_ref[:, pl.ds(off, tk)], w_ref[...],
                                                  preferred_element_type=F32)

    def below(n):
        return jnp.clip(per_q * i - 1 - n, 0, per_q * i)

    on_diagonal = [(per_q * i + d, d * tk) for d in reversed(range(per_q))]
    _pipelined_sweep(on_diagonal, per_q * i, below, scores_into, weights_from,
                     (s0_sc, s1_sc), accumulate, ((w0_sc, c0_sc), (w1_sc, c1_sc)), count_is_even=True)
    o_ref[...] = (acc_sc[...].T * _silu(g_ref[...])).astype(o_ref.dtype)


def _mixer_b(qk, q_col, k_col, vt, v_row, g_arr, g_col, tq, tk):
    s = qk.shape[0]
    kern = functools.partial(_mixer_b_kernel, tq=tq, tk=tk)
    return pl.pallas_call(
        kern,
        out_shape=jax.ShapeDtypeStruct((s, BRANCH), BF16),
        grid=(N_HEADS, s // tq),
        in_specs=[pl.BlockSpec((tq, HEAD_DIM), lambda h, i: (i, q_col + h)),
                  pl.BlockSpec((s, HEAD_DIM), lambda h, i: (0, k_col + h)),
                  pl.BlockSpec((HEAD_DIM, s), lambda h, i: (v_row + h, 0)),
                  pl.BlockSpec((tq, HEAD_DIM), lambda h, i: (i, g_col + h))],
        out_specs=pl.BlockSpec((tq, HEAD_DIM), lambda h, i: (i, h)),
        scratch_shapes=[pltpu.VMEM((HEAD_DIM, tq), F32), pltpu.VMEM((1, tq), F32),
                        pltpu.VMEM((tk, tq), F32), pltpu.VMEM((tk, tq), F32),
                        pltpu.VMEM((tk, tq), BF16), pltpu.VMEM((1, tq), F32),
                        pltpu.VMEM((tk, tq), BF16), pltpu.VMEM((1, tq), F32)],
        compiler_params=_params("parallel", "parallel"),
        name="mixer_b",
    )(qk, qk, vt, g_arr)


def _decay_features_kernel(u_ref, wf_ref, b_ref, place_ref, ones_ref, qx_ref, kx_ref, carry_sc):
    tm = u_ref.shape[0]

    @pl.when(pl.program_id(0) == 0)
    def _():
        carry_sc[...] = jnp.zeros_like(carry_sc)

    log_f = _log_sigmoid(jnp.dot(u_ref[...], wf_ref[...], preferred_element_type=F32) + b_ref[...])
    r = lax.broadcasted_iota(jnp.int32, (tm, tm), 0)
    c = lax.broadcasted_iota(jnp.int32, (tm, tm), 1)
    upto = jnp.where(c <= r, 1.0, 0.0).astype(F32)
    cum = jnp.dot(upto, log_f, preferred_element_type=F32,
                  precision=lax.Precision.HIGHEST) + carry_sc[...]
    carry_sc[...] = cum[tm - 1:tm, :]

    c2 = cum * LOG2E
    hi = c2.astype(BF16)
    r1 = c2 - hi.astype(F32)
    mid = r1.astype(BF16)
    lo = (r1 - mid.astype(F32)).astype(BF16)
    placed = jnp.dot(jnp.concatenate([hi, mid, lo], axis=1), place_ref[...],
                     preferred_element_type=F32) + ones_ref[...]
    qx_ref[...] = placed[:, :BRANCH].astype(BF16)
    kx_ref[...] = placed[:, BRANCH:].astype(BF16)


def _decay_placement():
    place = np.zeros((3 * HEAD_DIM, 2 * BRANCH), np.float32)
    ones = np.zeros((1, 2 * BRANCH), np.float32)
    for head in range(N_HEADS):
        for term in range(3):
            place[term * HEAD_DIM + head, head * HEAD_DIM + term] = 1.0
            place[term * HEAD_DIM + head, BRANCH + head * HEAD_DIM + 3 + term] = -1.0
            ones[0, head * HEAD_DIM + 3 + term] = 1.0
            ones[0, BRANCH + head * HEAD_DIM + term] = 1.0
    return jnp.asarray(place, BF16), jnp.asarray(ones, F32)


def _decay_features(u, wf, bias):
    s, d = u.shape
    tm = min(512, s)
    place, ones = _decay_placement()
    out = jax.ShapeDtypeStruct((s, BRANCH), BF16)
    spec = pl.BlockSpec((tm, BRANCH), lambda i: (i, 0))
    return pl.pallas_call(
        _decay_features_kernel,
        out_shape=(out, out),
        grid=(s // tm,),
        in_specs=[pl.BlockSpec((tm, d), lambda i: (i, 0)),
                  pl.BlockSpec((d, HEAD_DIM), lambda i: (0, 0)),
                  pl.BlockSpec((1, HEAD_DIM), lambda i: (0, 0)),
                  pl.BlockSpec(place.shape, lambda i: (0, 0)),
                  pl.BlockSpec(ones.shape, lambda i: (0, 0))],
        out_specs=(spec, spec),
        scratch_shapes=[pltpu.VMEM((1, HEAD_DIM), F32)],
        compiler_params=_params("arbitrary"),
        name="decay_features",
    )(u, wf, bias, place, ones)


def _causal_sweep(i, tq, tk, masked_out_fn, scores_into, consume, s_bufs):
    per_q = tq // tk
    assert tq == per_q * tk and per_q in (1, 2)
    head = [(per_q * i + d, masked_out_fn(*_key_query_index(tk, tq, d * tk))) for d in range(per_q)]
    _pipelined_sweep(head, per_q * i, lambda n: jnp.clip(n, 0, per_q * i), scores_into, consume, s_bufs,
                     count_is_even=per_q == 2, steps_per_body=4 // per_q)


def _mixer_c_kernel(q_ref, qx_ref, k_ref, kx_ref, vt_ref, g_ref, o_ref, m_sc, acc_sc, s0_sc, s1_sc,
                    *, tq, tk):
    q = jnp.concatenate([q_ref[...], qx_ref[...]], axis=1)
    ones = _ones_rows(tk)
    _init_softmax_state(m_sc, acc_sc)

    def scores_into(buf, j):
        off = pl.multiple_of(j * tk, tk)
        k = jnp.concatenate([k_ref[pl.ds(off, tk), :], kx_ref[pl.ds(off, tk), :]], axis=1)
        buf[...] = lax.dot_general(k, q, _NT, preferred_element_type=F32)

    def consume(buf, _, j, masked_out):
        off = pl.multiple_of(j * tk, tk)
        v_aug = jnp.concatenate([vt_ref[:, pl.ds(off, tk)], ones], axis=0)
        _softmax_step(buf[...], masked_out, v_aug, m_sc, acc_sc)

    _causal_sweep(pl.program_id(1), tq, tk, lambda key, query: key > query, scores_into, consume,
                  (s0_sc, s1_sc))
    o_ref[...] = (_normalised_rows(acc_sc) * _silu(g_ref[...])).astype(o_ref.dtype)


def _mixer_c(qk, q_col, k_col, qx, kx, vt, v_row, g_arr, g_col, tq, tk):
    s = qk.shape[0]
    return pl.pallas_call(
        functools.partial(_mixer_c_kernel, tq=tq, tk=tk),
        out_shape=jax.ShapeDtypeStruct((s, BRANCH), BF16),
        grid=(N_HEADS, s // tq),
        in_specs=[pl.BlockSpec((tq, HEAD_DIM), lambda h, i: (i, q_col + h)),
                  pl.BlockSpec((tq, HEAD_DIM), lambda h, i: (i, h)),
                  pl.BlockSpec((s, HEAD_DIM), lambda h, i: (0, k_col + h)),
                  pl.BlockSpec((s, HEAD_DIM), lambda h, i: (0, h)),
                  pl.BlockSpec((HEAD_DIM, s), lambda h, i: (v_row + h, 0)),
                  pl.BlockSpec((tq, HEAD_DIM), lambda h, i: (i, g_col + h))],
        out_specs=pl.BlockSpec((tq, HEAD_DIM), lambda h, i: (i, h)),
        scratch_shapes=[pltpu.VMEM((1, tq), F32),
                        pltpu.VMEM((HEAD_DIM + BF16_SUBLANES, tq), F32),
                        pltpu.VMEM((tk, tq), F32), pltpu.VMEM((tk, tq), F32)],
        compiler_params=_params("parallel", "parallel"),
        name="mixer_c",
    )(qk, qx, qk, kx, vt, g_arr)


def _rope_kernel(x_ref, pos_ref, invf_ref, o_ref, *, groups):
    ang = pos_ref[...].astype(F32) * invf_ref[...]
    cos = jnp.cos(ang)
    sin = jnp.sin(ang)
    lane = lax.broadcasted_iota(jnp.int32, ang.shape, 1)
    first_half = (lane & (D_QK - 1)) < ROPE_DIM // 2
    for gidx in range(groups):
        x = x_ref[:, gidx * 128:(gidx + 1) * 128]
        partner = jnp.where(first_half,
                            -pltpu.roll(x, 128 - ROPE_DIM // 2, 1),
                            pltpu.roll(x, ROPE_DIM // 2, 1))
        o_ref[:, gidx * 128:(gidx + 1) * 128] = (x * cos + partner * sin).astype(o_ref.dtype)


def _rope(x, positions, invf):
    s, n = x.shape
    tm = min(512, s)
    return pl.pallas_call(
        functools.partial(_rope_kernel, groups=n // 128),
        out_shape=jax.ShapeDtypeStruct((s, n), BF16),
        grid=(s // tm,),
        in_specs=[pl.BlockSpec((tm, n), lambda i: (i, 0)),
                  pl.BlockSpec((tm, 1), lambda i: (i, 0)),
                  pl.BlockSpec((1, 128), lambda i: (0, 0))],
        out_specs=pl.BlockSpec((tm, n), lambda i: (i, 0)),
        compiler_params=_params("parallel"),
        name="rope",
    )(x, positions, invf)


def _mixer_d_kernel(q_ref, k_ref, vt_ref, g_ref, lam_ref, sub_ref, o_ref,
                    m1_sc, a1_sc, m2_sc, a2_sc, sa1_sc, sa2_sc, sb1_sc, sb2_sc, *, tq, tk, lambda_init):
    q = q_ref[...]
    lane = lax.broadcasted_iota(jnp.int32, q.shape, 1)
    zero = jnp.zeros_like(q)
    q1 = jnp.where(lane < D_QK, q, zero)
    q2 = jnp.where(lane >= D_QK, q, zero)
    ones = _ones_rows(tk)
    _init_softmax_state(m1_sc, a1_sc)
    _init_softmax_state(m2_sc, a2_sc)

    def scores_into(bufs, j):
        off = pl.multiple_of(j * tk, tk)
        k = k_ref[pl.ds(off, tk), :]
        for qm, buf in zip((q1, q2), bufs):
            buf[...] = lax.dot_general(k, qm, _NT, preferred_element_type=F32)

    def consume(bufs, _, j, masked_out):
        off = pl.multiple_of(j * tk, tk)
        v_aug = jnp.concatenate([vt_ref[:, pl.ds(off, tk)], ones], axis=0)
        for buf, m_sc, a_sc in zip(bufs, (m1_sc, m2_sc), (a1_sc, a2_sc)):
            _softmax_step(buf[...], masked_out, v_aug, m_sc, a_sc)

    _causal_sweep(pl.program_id(1), tq, tk,
                  lambda key, query: (key >> CHUNK_SHIFT) > (query >> CHUNK_SHIFT),
                  scores_into, consume, ((sa1_sc, sa2_sc), (sb1_sc, sb2_sc)))

    lv = lam_ref[...]
    lam = (jnp.exp(jnp.sum(lv[0:1] * lv[1:2], axis=1, keepdims=True))
           - jnp.exp(jnp.sum(lv[2:3] * lv[3:4], axis=1, keepdims=True)) + lambda_init)
    o = _normalised_rows(a1_sc) - lam * _normalised_rows(a2_sc)
    ms = jnp.mean(o * o, axis=-1, keepdims=True)
    y = o * lax.rsqrt(ms + EPS) * sub_ref[...] * (1.0 - lambda_init)
    o_ref[...] = (y * _silu(g_ref[...])).astype(o_ref.dtype)


def _mixer_d(qk_arr, q_col, k_col, vt, v_row, g_arr, g_col, lam_vecs, subln, lambda_init, tq, tk):
    s = qk_arr.shape[0]
    kern = functools.partial(_mixer_d_kernel, tq=tq, tk=tk, lambda_init=lambda_init)
    row_stat = pltpu.VMEM((1, tq), F32)
    acc = pltpu.VMEM((HEAD_DIM + BF16_SUBLANES, tq), F32)
    return pl.pallas_call(
        kern,
        out_shape=jax.ShapeDtypeStruct((s, BRANCH), BF16),
        grid=(N_HEADS, s // tq),
        in_specs=[pl.BlockSpec((tq, HEAD_DIM), lambda h, i: (i, q_col + h)),
                  pl.BlockSpec((s, HEAD_DIM), lambda h, i: (0, k_col + h)),
                  pl.BlockSpec((HEAD_DIM, s), lambda h, i: (v_row + h, 0)),
                  pl.BlockSpec((tq, HEAD_DIM), lambda h, i: (i, g_col + h)),
                  pl.BlockSpec((4, D_QK), lambda h, i: (0, 0)),
                  pl.BlockSpec((1, HEAD_DIM), lambda h, i: (0, 0))],
        out_specs=pl.BlockSpec((tq, HEAD_DIM), lambda h, i: (i, h)),
        scratch_shapes=[row_stat, acc, row_stat, acc] + [pltpu.VMEM((tk, tq), F32)] * 4,
        compiler_params=_params("parallel", "parallel"),
        name="mixer_d",
    )(qk_arr, qk_arr, vt, g_arr, lam_vecs, subln.reshape(1, HEAD_DIM))


def _out_proj_kernel(ya_ref, yb_ref, w_ref, h_ref, g_ref, o_ref):
    half = ya_ref.shape[1]
    y = (jnp.dot(ya_ref[...], w_ref[:half, :], preferred_element_type=F32)
         + jnp.dot(yb_ref[...], w_ref[half:, :], preferred_element_type=F32))
    ms = jnp.mean(y * y, axis=-1, keepdims=True)
    o_ref[...] = h_ref[...] + y * lax.rsqrt(ms + EPS) * g_ref[...]


def _out_proj(ya, yb, w, h, g):
    s, d = h.shape
    tm = min(512, s)
    return pl.pallas_call(
        _out_proj_kernel,
        out_shape=jax.ShapeDtypeStruct((s, d), F32),
        grid=(s // tm,),
        in_specs=[pl.BlockSpec((tm, BRANCH), lambda i: (i, 0)),
                  pl.BlockSpec((tm, BRANCH), lambda i: (i, 0)),
                  pl.BlockSpec((2 * BRANCH, d), lambda i: (0, 0)),
                  pl.BlockSpec((tm, d), lambda i: (i, 0)),
                  pl.BlockSpec((1, d), lambda i: (0, 0))],
        out_specs=pl.BlockSpec((tm, d), lambda i: (i, 0)),
        compiler_params=_params("parallel"),
        name="out_proj",
    )(ya, yb, w, h, g.reshape(1, d))


def _rel_bias_tile(rel_bias, tq):
    rows = 2 * A_PAD + tq
    period = rows + tq
    k = np.arange(period)
    c_minus_r = np.where(k < rows, k, k - period)
    line = rel_bias.astype(F32)[:, np.clip(A_PAD - c_minus_r, -REL_CLIP, REL_CLIP) + REL_CLIP] * LOG2E
    flat = jnp.tile(line, (1, tq))[:, :tq * (period - 1)]
    tile = flat.reshape(rel_bias.shape[0], tq, period - 1)[:, :, :rows]
    qc = np.arange(tq)[:, None] >> CHUNK_SHIFT
    kc = np.arange(rows)[None, :] >> CHUNK_SHIFT
    in_band = (kc >= qc) & (kc <= qc + A_LEFT_CHUNKS)
    return jnp.swapaxes(jnp.where(in_band, tile, NEG_INF), 1, 2)


def _even_layer(h, w_in, rel_bias, w_out, norm_pre, norm_post):
    s = h.shape[0]
    u = _rmsnorm(h, norm_pre)
    b = BRANCH
    log2_scale = HEAD_DIM ** -0.5 * LOG2E
    w_rows = jnp.concatenate([w_in[:, :b] * log2_scale, w_in[:, b:2 * b],
                              w_in[:, 4 * b:5 * b] * log2_scale, w_in[:, 5 * b:6 * b]], axis=1).astype(BF16)
    w_g = jnp.concatenate([w_in[:, 3 * b:4 * b], w_in[:, 7 * b:]], axis=1).astype(BF16)
    w_vt = jnp.concatenate([w_in[:, 2 * b:3 * b], w_in[:, 6 * b:7 * b]], axis=1).T.astype(BF16)
    rows = _matmul(u, w_rows, BF16)
    gates = _matmul(u, w_g, F32)
    vt = _matmul_t(w_vt, u, BF16)
    tq_a = min(256, s)
    y_a = _mixer_a(rows, 0, N_HEADS, vt, 0, gates, 0, _rel_bias_tile(rel_bias, tq_a), tq_a)
    y_b = _mixer_b(rows, 2 * N_HEADS, 3 * N_HEADS, vt, N_HEADS, gates, N_HEADS, min(512, s), min(256, s))
    return _out_proj(y_a, y_b, w_out.astype(BF16), h, norm_post)


def _odd_layer(h, positions, w_in, forget_bias, lq1, lk1, lq2, lk2, subln, w_out, norm_pre, norm_post,
               lambda_init):
    s, d = h.shape
    u = _rmsnorm(h, norm_pre)
    b = BRANCH
    f0 = 4 * b
    d0 = f0 + N_HEADS
    v0 = d0 + 4 * N_HEADS * D_QK
    w_rows = jnp.concatenate([w_in[:, :b] * (HEAD_DIM ** -0.5 * LOG2E), w_in[:, b:2 * b]],
                             axis=1).astype(BF16)
    w_g = jnp.concatenate([w_in[:, 3 * b:4 * b], w_in[:, v0 + b:]], axis=1).astype(BF16)
    w_vt = jnp.concatenate([w_in[:, 2 * b:3 * b], w_in[:, v0:v0 + b]], axis=1).T.astype(BF16)
    wd = w_in[:, d0:v0].reshape(d, 4, N_HEADS, D_QK)
    w_qk = jnp.concatenate([
        jnp.concatenate([wd[:, 0], wd[:, 1]], axis=2).reshape(d, b) * (D_QK ** -0.5 * LOG2E),
        jnp.concatenate([wd[:, 2], wd[:, 3]], axis=2).reshape(d, b),
    ], axis=1).astype(BF16)
    w_f = jnp.pad(w_in[:, f0:d0], ((0, 0), (0, HEAD_DIM - N_HEADS))).astype(BF16)
    b_f = jnp.pad(forget_bias.astype(F32), (0, HEAD_DIM - N_HEADS)).reshape(1, HEAD_DIM)

    rows = _matmul(u, w_rows, BF16)
    gates = _matmul(u, w_g, F32)
    vt = _matmul_t(w_vt, u, BF16)
    qk_d = _matmul(u, w_qk, F32)

    qx, kx = _decay_features(u, w_f, b_f)
    tq = tk = min(512, s)
    y_c = _mixer_c(rows, 0, N_HEADS, qx, kx, vt, 0, gates, 0, min(2 * tq, s), tk)

    lane = jnp.arange(128) % D_QK
    inv_freq = ROPE_THETA ** (-jnp.arange(0, ROPE_DIM, 2, dtype=F32) / ROPE_DIM)
    invf = jnp.where(lane < ROPE_DIM, inv_freq[lane % (ROPE_DIM // 2)], 0.0).astype(F32).reshape(1, 128)
    qk_rot = _rope(qk_d, positions.reshape(s, 1), invf)
    lam_vecs = jnp.stack([lq1, lk1, lq2, lk2]).astype(F32)
    y_d = _mixer_d(qk_rot, 0, N_HEADS, vt, N_HEADS, gates, N_HEADS, lam_vecs, subln, lambda_init, tq, tk)
    return _out_proj(y_c, y_d, w_out.astype(BF16), h, norm_post)


def kernel(x, positions, even_w_in, even_rel_bias, even_w_out, even_norm_pre, even_norm_post,
           odd_w_in, odd_forget_bias, odd_lambda_q1, odd_lambda_k1, odd_lambda_q2, odd_lambda_k2,
           odd_subln, odd_w_out, odd_norm_pre, odd_norm_post):
    assert x.shape[0] == 1
    h = x[0]
    depth = even_w_in.shape[0] + odd_w_in.shape[0]
    for layer in range(depth):
        i = layer // 2
        if layer % 2 == 0:
            h = _even_layer(h, even_w_in[i], even_rel_bias[i], even_w_out[i],
                            even_norm_pre[i], even_norm_post[i])
        else:
            lambda_init = 0.8 - 0.6 * math.exp(-0.3 * layer)
            h = _odd_layer(h, positions, odd_w_in[i], odd_forget_bias[i],
                           odd_lambda_q1[i], odd_lambda_k1[i], odd_lambda_q2[i], odd_lambda_k2[i],
                           odd_subln[i], odd_w_out[i], odd_norm_pre[i], odd_norm_post[i], lambda_init)
    return h[None]
```

```python
import functools
import math

import jax
import jax.numpy as jnp
import numpy as np
from jax import lax
from jax.experimental import pallas as pl
from jax.experimental.pallas import tpu as pltpu

F32 = jnp.float32
BF16 = jnp.bfloat16

HEAD_DIM = 128
N_HEADS = 8
BRANCH = N_HEADS * HEAD_DIM
CHUNK = 64
CHUNK_SHIFT = 6
A_LEFT_CHUNKS = 8
A_PAD = A_LEFT_CHUNKS * CHUNK
REL_CLIP = 128
D_QK = 64
ROPE_THETA = 500000.0
ROPE_DIM = 16
EPS = 1e-6
NEG_INF = float("-inf")
LOG2E = math.log2(math.e)
BF16_SUBLANES = 16
VMEM_LIMIT_BYTES = 48 * 1024 * 1024

_NT = (((1,), (1,)), ((), ()))


def _params(*semantics):
    return pltpu.CompilerParams(dimension_semantics=semantics, vmem_limit_bytes=VMEM_LIMIT_BYTES)


def _silu(g):
    return g * jax.nn.sigmoid(g)


def _log_sigmoid(z):
    return jnp.minimum(z, 0.0) - jnp.log1p(jnp.exp(-jnp.abs(z)))


def _rmsnorm_kernel(x_ref, g_ref, o_ref):
    x = x_ref[...]
    ms = jnp.mean(x * x, axis=-1, keepdims=True)
    o_ref[...] = (x * lax.rsqrt(ms + EPS) * g_ref[...]).astype(o_ref.dtype)


def _rmsnorm(x, g):
    s, d = x.shape
    tm = min(512, s)
    return pl.pallas_call(
        _rmsnorm_kernel,
        out_shape=jax.ShapeDtypeStruct((s, d), BF16),
        grid=(s // tm,),
        in_specs=[pl.BlockSpec((tm, d), lambda i: (i, 0)),
                  pl.BlockSpec((1, d), lambda i: (0, 0))],
        out_specs=pl.BlockSpec((tm, d), lambda i: (i, 0)),
        compiler_params=_params("parallel"),
        name="rmsnorm",
    )(x, g.reshape(1, d))


def _matmul_kernel(x_ref, w_ref, o_ref):
    o_ref[...] = jnp.dot(x_ref[...], w_ref[...], preferred_element_type=F32).astype(o_ref.dtype)


def _matmul(x, w, out_dtype):
    m, k = x.shape
    n = w.shape[1]
    tm = min(1024, m)
    tn = 1024 if n % 1024 == 0 else n
    return pl.pallas_call(
        _matmul_kernel,
        out_shape=jax.ShapeDtypeStruct((m, n), out_dtype),
        grid=(n // tn, m // tm),
        in_specs=[pl.BlockSpec((tm, k), lambda j, i: (i, 0)),
                  pl.BlockSpec((k, tn), lambda j, i: (0, j))],
        out_specs=pl.BlockSpec((tm, tn), lambda j, i: (i, j)),
        compiler_params=_params("parallel", "parallel"),
        name="in_proj",
    )(x, w)


def _matmul_t_kernel(w_ref, x_ref, o_ref):
    o_ref[...] = lax.dot_general(w_ref[...], x_ref[...], _NT,
                                 preferred_element_type=F32).astype(o_ref.dtype)


def _matmul_t(w_t, x, out_dtype):
    n, k = w_t.shape
    m = x.shape[0]
    tm = min(1024, m)
    tn = 1024 if n % 1024 == 0 else n
    return pl.pallas_call(
        _matmul_t_kernel,
        out_shape=jax.ShapeDtypeStruct((n, m), out_dtype),
        grid=(n // tn, m // tm),
        in_specs=[pl.BlockSpec((tn, k), lambda j, i: (j, 0)),
                  pl.BlockSpec((tm, k), lambda j, i: (i, 0))],
        out_specs=pl.BlockSpec((tn, tm), lambda j, i: (j, i)),
        compiler_params=_params("parallel", "parallel"),
        name="in_proj_t",
    )(w_t, x)


def _mixer_a_kernel(q_ref, k_ref, vt_ref, g_ref, b_ref, o_ref, *, tq, band):
    ones = _ones_rows(band)
    for sub in range(q_ref.shape[0] // tq):
        rows = slice(sub * tq, (sub + 1) * tq)
        first = (pl.program_id(1) * (q_ref.shape[0] // tq) + sub) * tq - A_PAD
        start = pl.multiple_of(jnp.maximum(first, 0), tq)
        cut = pl.multiple_of(start - first, tq)
        s = (lax.dot_general(k_ref[pl.ds(start, band), :], q_ref[rows, :], _NT, preferred_element_type=F32)
             + b_ref[0, pl.ds(cut, band), :])
        m = jnp.max(s, axis=0, keepdims=True)
        p = jnp.exp2(s - m).astype(BF16)
        v_aug = jnp.concatenate([vt_ref[:, pl.ds(start, band)], ones], axis=0)
        acc = jnp.dot(v_aug, p, preferred_element_type=F32)
        o = (acc[:HEAD_DIM] / acc[HEAD_DIM:HEAD_DIM + 1]).T
        o_ref[rows, :] = (o * _silu(g_ref[rows, :])).astype(o_ref.dtype)


def _mixer_a(qk, q_col, k_col, vt, v_row, g_arr, g_col, bias_tile, tq):
    s = qk.shape[0]
    assert A_PAD % tq == 0
    band = A_PAD + tq
    tstep = min(4 * tq, s)
    kern = functools.partial(_mixer_a_kernel, tq=tq, band=band)
    return pl.pallas_call(
        kern,
        out_shape=jax.ShapeDtypeStruct((s, BRANCH), BF16),
        grid=(N_HEADS, s // tstep),
        in_specs=[pl.BlockSpec((tstep, HEAD_DIM), lambda h, i: (i, q_col + h)),
                  pl.BlockSpec((s, HEAD_DIM), lambda h, i: (0, k_col + h)),
                  pl.BlockSpec((HEAD_DIM, s), lambda h, i: (v_row + h, 0)),
                  pl.BlockSpec((tstep, HEAD_DIM), lambda h, i: (i, g_col + h)),
                  pl.BlockSpec((1, band + A_PAD, tq), lambda h, i: (h, 0, 0))],
        out_specs=pl.BlockSpec((tstep, HEAD_DIM), lambda h, i: (i, h)),
        compiler_params=_params("parallel", "parallel"),
        name="mixer_a",
    )(qk, qk, vt, g_arr, bias_tile)


def _key_query_index(tk, tq, key_off):
    key = lax.broadcasted_iota(jnp.int32, (tk, tq), 0) + key_off
    query = lax.broadcasted_iota(jnp.int32, (tk, tq), 1)
    return key, query


def _softmax_step(s, masked_out, v_aug, m_sc, acc_sc):
    if masked_out is not None:
        s = jnp.where(masked_out, NEG_INF, s)
    m_old = m_sc[...]
    m_new = jnp.maximum(m_old, jnp.max(s, axis=0, keepdims=True))
    p = jnp.exp2(s - m_new).astype(BF16)
    alpha = jnp.exp2(m_old - m_new)
    acc_sc[...] = alpha * acc_sc[...] + jnp.dot(v_aug, p, preferred_element_type=F32)
    m_sc[...] = m_new


def _init_softmax_state(m_sc, acc_sc):
    m_sc[...] = jnp.full_like(m_sc, NEG_INF)
    acc_sc[...] = jnp.zeros_like(acc_sc)


def _normalised_rows(acc_sc):
    acc = acc_sc[...]
    return (acc[:HEAD_DIM] / acc[HEAD_DIM:HEAD_DIM + 1]).T


def _ones_rows(width):
    return jnp.ones((BF16_SUBLANES, width), BF16)


def _pipelined_sweep(head, count, rest_block, scores_into, consume, s_bufs,
                     accumulate=None, w_bufs=(None, None), count_is_even=False, steps_per_body=4):
    n_head = len(head)
    last_head = head[-1][0]
    defer_accumulate = accumulate is not None

    def step(par, block, mask, prev_block, next_block):
        if next_block is not None:
            scores_into(s_bufs[1 - par], next_block)
        consume(s_bufs[par], w_bufs[par], block, mask)
        if defer_accumulate and prev_block is not None:
            accumulate(w_bufs[1 - par], prev_block)

    def rest_prev(n):
        return jnp.where(n == 0, last_head, rest_block(n - 1))

    scores_into(s_bufs[0], head[0][0])
    for g, (block, mask) in enumerate(head):
        step(g % 2, block, mask, head[g - 1][0] if g else None,
             head[g + 1][0] if g + 1 < n_head else rest_block(0))

    def body_of(steps):
        def body(_, first):
            for u in range(steps):
                n = first + u
                step((n_head + u) % 2, rest_block(n), None,
                     rest_block(n - 1) if u else rest_prev(n), rest_block(n + 1))
            return first + steps
        return body

    done = lax.fori_loop(0, count // steps_per_body, body_of(steps_per_body), 0)
    if steps_per_body > 2:
        lax.fori_loop(0, (count - done) // 2, body_of(2), done)
    last_block = rest_prev(count)
    if count_is_even:
        if defer_accumulate:
            accumulate(w_bufs[(n_head - 1) % 2], last_block)
        return

    @pl.when(count % 2 == 1)
    def _():
        n = count - 1
        step(n_head % 2, rest_block(n), None, rest_prev(n), None)
        if defer_accumulate:
            accumulate(w_bufs[n_head % 2], last_block)

    if defer_accumulate:
        @pl.when(count % 2 == 0)
        def _():
            accumulate(w_bufs[(n_head - 1) % 2], last_block)


def _mixer_b_kernel(q_ref, k_ref, vt_ref, g_ref, o_ref, acc_sc, run_sc, s0_sc, s1_sc,
                    w0_sc, c0_sc, w1_sc, c1_sc, *, tq, tk):
    per_q = tq // tk
    assert tq == per_q * tk and per_q % 2 == 0
    i = pl.program_id(1)
    q = q_ref[...]
    kk = lax.broadcasted_iota(jnp.int32, (tk + BF16_SUBLANES, tk), 0)
    jj = lax.broadcasted_iota(jnp.int32, (tk + BF16_SUBLANES, tk), 1)
    suffix = jnp.where((jj > kk) | (kk >= tk), 1.0, 0.0).astype(BF16)
    acc_sc[...] = jnp.zeros_like(acc_sc)
    run_sc[...] = jnp.zeros_like(run_sc)

    def scores_into(buf, j):
        off = pl.multiple_of(j * tk, tk)
        buf[...] = lax.dot_general(k_ref[pl.ds(off, tk), :], q, _NT, preferred_element_type=F32)

    def weights_from(buf, w_buf, j, key_off):
        w_ref, carried_ref = w_buf
        z = buf[...]
        neg_abs = lax.bitcast_convert_type(
            lax.bitcast_convert_type(z, jnp.uint32) | jnp.uint32(0x80000000), F32)
        log_beta = jnp.minimum(z, 0.0) - jnp.log(1.0 + jnp.exp2(neg_abs)) * LOG2E
        log_keep = log_beta - z
        if key_off is not None:
            key, query = _key_query_index(tk, tq, key_off)
            strict = key < query
            log_keep = jnp.where(strict, log_keep, 0.0)
        later = jnp.dot(suffix, log_keep.astype(BF16), preferred_element_type=F32)
        w = jnp.exp2(log_beta + later[:tk])
        if key_off is not None:
            w = jnp.where(strict, w, 0.0)
        w_ref[...] = w.astype(BF16)
        carried_ref[...] = jnp.exp2(run_sc[...])
        run_sc[...] += later[tk:tk + 1]

    def accumulate(w_buf, j):
        w_ref, carried_ref = w_buf
        off = pl.multiple_of(j * tk, tk)
        acc_sc[...] += carried_ref[...] * jnp.dot(vt_ref[:, pl.ds(off, tk)], w_ref[...],
                                                  preferred_element_type=F32)

    def below(n):
        return jnp.clip(per_q * i - 1 - n, 0, per_q * i)

    on_diagonal = [(per_q * i + d, d * tk) for d in reversed(range(per_q))]
    _pipelined_sweep(on_diagonal, per_q * i, below, scores_into, weights_from,
                     (s0_sc, s1_sc), accumulate, ((w0_sc, c0_sc), (w1_sc, c1_sc)), count_is_even=True)
    o_ref[...] = (acc_sc[...].T * _silu(g_ref[...])).astype(o_ref.dtype)


def _mixer_b(qk, q_col, k_col, vt, v_row, g_arr, g_col, tq, tk):
    s = qk.shape[0]
    kern = functools.partial(_mixer_b_kernel, tq=tq, tk=tk)
    return pl.pallas_call(
        kern,
        out_shape=jax.ShapeDtypeStruct((s, BRANCH), BF16),
        grid=(N_HEADS, s // tq),
        in_specs=[pl.BlockSpec((tq, HEAD_DIM), lambda h, i: (i, q_col + h)),
                  pl.BlockSpec((s, HEAD_DIM), lambda h, i: (0, k_col + h)),
                  pl.BlockSpec((HEAD_DIM, s), lambda h, i: (v_row + h, 0)),
                  pl.BlockSpec((tq, HEAD_DIM), lambda h, i: (i, g_col + h))],
        out_specs=pl.BlockSpec((tq, HEAD_DIM), lambda h, i: (i, h)),
        scratch_shapes=[pltpu.VMEM((HEAD_DIM, tq), F32), pltpu.VMEM((1, tq), F32),
                        pltpu.VMEM((tk, tq), F32), pltpu.VMEM((tk, tq), F32),
                        pltpu.VMEM((tk, tq), BF16), pltpu.VMEM((1, tq), F32),
                        pltpu.VMEM((tk, tq), BF16), pltpu.VMEM((1, tq), F32)],
        compiler_params=_params("parallel", "parallel"),
        name="mixer_b",
    )(qk, qk, vt, g_arr)


def _decay_features_kernel(u_ref, wf_ref, b_ref, place_ref, ones_ref, qx_ref, kx_ref, carry_sc):
    tm = u_ref.shape[0]

    @pl.when(pl.program_id(0) == 0)
    def _():
        carry_sc[...] = jnp.zeros_like(carry_sc)

    log_f = _log_sigmoid(jnp.dot(u_ref[...], wf_ref[...], preferred_element_type=F32) + b_ref[...])
    r = lax.broadcasted_iota(jnp.int32, (tm, tm), 0)
    c = lax.broadcasted_iota(jnp.int32, (tm, tm), 1)
    upto = jnp.where(c <= r, 1.0, 0.0).astype(F32)
    cum = jnp.dot(upto, log_f, preferred_element_type=F32,
                  precision=lax.Precision.HIGHEST) + carry_sc[...]
    carry_sc[...] = cum[tm - 1:tm, :]

    c2 = cum * LOG2E
    hi = c2.astype(BF16)
    r1 = c2 - hi.astype(F32)
    mid = r1.astype(BF16)
    lo = (r1 - mid.astype(F32)).astype(BF16)
    placed = jnp.dot(jnp.concatenate([hi, mid, lo], axis=1), place_ref[...],
                     preferred_element_type=F32) + ones_ref[...]
    qx_ref[...] = placed[:, :BRANCH].astype(BF16)
    kx_ref[...] = placed[:, BRANCH:].astype(BF16)


def _decay_placement():
    place = np.zeros((3 * HEAD_DIM, 2 * BRANCH), np.float32)
    ones = np.zeros((1, 2 * BRANCH), np.float32)
    for head in range(N_HEADS):
        for term in range(3):
            place[term * HEAD_DIM + head, head * HEAD_DIM + term] = 1.0
            place[term * HEAD_DIM + head, BRANCH + head * HEAD_DIM + 3 + term] = -1.0
            ones[0, head * HEAD_DIM + 3 + term] = 1.0
            ones[0, BRANCH + head * HEAD_DIM + term] = 1.0
    return jnp.asarray(place, BF16), jnp.asarray(ones, F32)


def _decay_features(u, wf, bias):
    s, d = u.shape
    tm = min(512, s)
    place, ones = _decay_placement()
    out = jax.ShapeDtypeStruct((s, BRANCH), BF16)
    spec = pl.BlockSpec((tm, BRANCH), lambda i: (i, 0))
    return pl.pallas_call(
        _decay_features_kernel,
        out_shape=(out, out),
        grid=(s // tm,),
        in_specs=[pl.BlockSpec((tm, d), lambda i: (i, 0)),
                  pl.BlockSpec((d, HEAD_DIM), lambda i: (0, 0)),
                  pl.BlockSpec((1, HEAD_DIM), lambda i: (0, 0)),
                  pl.BlockSpec(place.shape, lambda i: (0, 0)),
                  pl.BlockSpec(ones.shape, lambda i: (0, 0))],
        out_specs=(spec, spec),
        scratch_shapes=[pltpu.VMEM((1, HEAD_DIM), F32)],
        compiler_params=_params("arbitrary"),
        name="decay_features",
    )(u, wf, bias, place, ones)


def _causal_sweep(i, tq, tk, masked_out_fn, scores_into, consume, s_bufs):
    per_q = tq // tk
    assert tq == per_q * tk and per_q in (1, 2)
    head = [(per_q * i + d, masked_out_fn(*_key_query_index(tk, tq, d * tk))) for d in range(per_q)]
    _pipelined_sweep(head, per_q * i, lambda n: jnp.clip(n, 0, per_q * i), scores_into, consume, s_bufs,
                     count_is_even=per_q == 2, steps_per_body=4 // per_q)


def _mixer_c_kernel(q_ref, qx_ref, k_ref, kx_ref, vt_ref, g_ref, o_ref, m_sc, acc_sc, s0_sc, s1_sc,
                    *, tq, tk):
    q = jnp.concatenate([q_ref[...], qx_ref[...]], axis=1)
    ones = _ones_rows(tk)
    _init_softmax_state(m_sc, acc_sc)

    def scores_into(buf, j):
        off = pl.multiple_of(j * tk, tk)
        k = jnp.concatenate([k_ref[pl.ds(off, tk), :], kx_ref[pl.ds(off, tk), :]], axis=1)
        buf[...] = lax.dot_general(k, q, _NT, preferred_element_type=F32)

    def consume(buf, _, j, masked_out):
        off = pl.multiple_of(j * tk, tk)
        v_aug = jnp.concatenate([vt_ref[:, pl.ds(off, tk)], ones], axis=0)
        _softmax_step(buf[...], masked_out, v_aug, m_sc, acc_sc)

    _causal_sweep(pl.program_id(1), tq, tk, lambda key, query: key > query, scores_into, consume,
                  (s0_sc, s1_sc))
    o_ref[...] = (_normalised_rows(acc_sc) * _silu(g_ref[...])).astype(o_ref.dtype)


def _mixer_c(qk, q_col, k_col, qx, kx, vt, v_row, g_arr, g_col, tq, tk):
    s = qk.shape[0]
    return pl.pallas_call(
        functools.partial(_mixer_c_kernel, tq=tq, tk=tk),
        out_shape=jax.ShapeDtypeStruct((s, BRANCH), BF16),
        grid=(N_HEADS, s // tq),
        in_specs=[pl.BlockSpec((tq, HEAD_DIM), lambda h, i: (i, q_col + h)),
                  pl.BlockSpec((tq, HEAD_DIM), lambda h, i: (i, h)),
                  pl.BlockSpec((s, HEAD_DIM), lambda h, i: (0, k_col + h)),
                  pl.BlockSpec((s, HEAD_DIM), lambda h, i: (0, h)),
                  pl.BlockSpec((HEAD_DIM, s), lambda h, i: (v_row + h, 0)),
                  pl.BlockSpec((tq, HEAD_DIM), lambda h, i: (i, g_col + h))],
        out_specs=pl.BlockSpec((tq, HEAD_DIM), lambda h, i: (i, h)),
        scratch_shapes=[pltpu.VMEM((1, tq), F32),
                        pltpu.VMEM((HEAD_DIM + BF16_SUBLANES, tq), F32),
                        pltpu.VMEM((tk, tq), F32), pltpu.VMEM((tk, tq), F32)],
        compiler_params=_params("parallel", "parallel"),
        name="mixer_c",
    )(qk, qx, qk, kx, vt, g_arr)


def _rope_kernel(x_ref, pos_ref, invf_ref, place_ref, o_ref, *, groups):
    ang = invf_ref[...] * pos_ref[...].astype(F32)
    place = place_ref[...]
    cos = jnp.dot(place, jnp.cos(ang), preferred_element_type=F32, precision=lax.Precision.HIGHEST).T
    sin = jnp.dot(place, jnp.sin(ang), preferred_element_type=F32, precision=lax.Precision.HIGHEST).T
    lane = lax.broadcasted_iota(jnp.int32, cos.shape, 1)
    first_half = (lane & (D_QK - 1)) < ROPE_DIM // 2
    for gidx in range(groups):
        x = x_ref[:, gidx * 128:(gidx + 1) * 128]
        partner = jnp.where(first_half,
                            -pltpu.roll(x, 128 - ROPE_DIM // 2, 1),
                            pltpu.roll(x, ROPE_DIM // 2, 1))
        o_ref[:, gidx * 128:(gidx + 1) * 128] = (x * cos + partner * sin).astype(o_ref.dtype)


def _rope(x, positions):
    s, n = x.shape
    tm = min(512, s)
    n_freq = ROPE_DIM // 2
    inv_freq = ROPE_THETA ** (-jnp.arange(0, ROPE_DIM, 2, dtype=F32) / ROPE_DIM)
    invf = jnp.pad(inv_freq, (0, n_freq)).reshape(2 * n_freq, 1)
    slot = np.arange(128) % D_QK
    place = np.zeros((128, 2 * n_freq), np.float32)
    place[np.arange(128), np.where(slot < ROPE_DIM, slot % n_freq, n_freq)] = 1.0
    return pl.pallas_call(
        functools.partial(_rope_kernel, groups=n // 128),
        out_shape=jax.ShapeDtypeStruct((s, n), BF16),
        grid=(s // tm,),
        in_specs=[pl.BlockSpec((tm, n), lambda i: (i, 0)),
                  pl.BlockSpec((1, tm), lambda i: (0, i)),
                  pl.BlockSpec(invf.shape, lambda i: (0, 0)),
                  pl.BlockSpec(place.shape, lambda i: (0, 0))],
        out_specs=pl.BlockSpec((tm, n), lambda i: (i, 0)),
        compiler_params=_params("parallel"),
        name="rope",
    )(x, positions, invf, jnp.asarray(place))


def _mixer_d_kernel(q_ref, k_ref, vt_ref, g_ref, lam_ref, sub_ref, o_ref,
                    m1_sc, a1_sc, m2_sc, a2_sc, sa1_sc, sa2_sc, sb1_sc, sb2_sc, *, tq, tk, lambda_init):
    q = q_ref[...]
    lane = lax.broadcasted_iota(jnp.int32, q.shape, 1)
    zero = jnp.zeros_like(q)
    q1 = jnp.where(lane < D_QK, q, zero)
    q2 = jnp.where(lane >= D_QK, q, zero)
    ones = _ones_rows(tk)
    _init_softmax_state(m1_sc, a1_sc)
    _init_softmax_state(m2_sc, a2_sc)

    def scores_into(bufs, j):
        off = pl.multiple_of(j * tk, tk)
        k = k_ref[pl.ds(off, tk), :]
        for qm, buf in zip((q1, q2), bufs):
            buf[...] = lax.dot_general(k, qm, _NT, preferred_element_type=F32)

    def consume(bufs, _, j, masked_out):
        off = pl.multiple_of(j * tk, tk)
        v_aug = jnp.concatenate([vt_ref[:, pl.ds(off, tk)], ones], axis=0)
        for buf, m_sc, a_sc in zip(bufs, (m1_sc, m2_sc), (a1_sc, a2_sc)):
            _softmax_step(buf[...], masked_out, v_aug, m_sc, a_sc)

    _causal_sweep(pl.program_id(1), tq, tk,
                  lambda key, query: (key >> CHUNK_SHIFT) > (query >> CHUNK_SHIFT),
                  scores_into, consume, ((sa1_sc, sa2_sc), (sb1_sc, sb2_sc)))

    lv = lam_ref[...]
    lam = (jnp.exp(jnp.sum(lv[0:1] * lv[1:2], axis=1, keepdims=True))
           - jnp.exp(jnp.sum(lv[2:3] * lv[3:4], axis=1, keepdims=True)) + lambda_init)
    o = _normalised_rows(a1_sc) - lam * _normalised_rows(a2_sc)
    ms = jnp.mean(o * o, axis=-1, keepdims=True)
    y = o * lax.rsqrt(ms + EPS) * sub_ref[...] * (1.0 - lambda_init)
    o_ref[...] = (y * _silu(g_ref[...])).astype(o_ref.dtype)


def _mixer_d(qk_arr, q_col, k_col, vt, v_row, g_arr, g_col, lam_vecs, subln, lambda_init, tq, tk):
    s = qk_arr.shape[0]
    kern = functools.partial(_mixer_d_kernel, tq=tq, tk=tk, lambda_init=lambda_init)
    row_stat = pltpu.VMEM((1, tq), F32)
    acc = pltpu.VMEM((HEAD_DIM + BF16_SUBLANES, tq), F32)
    return pl.pallas_call(
        kern,
        out_shape=jax.ShapeDtypeStruct((s, BRANCH), BF16),
        grid=(N_HEADS, s // tq),
        in_specs=[pl.BlockSpec((tq, HEAD_DIM), lambda h, i: (i, q_col + h)),
                  pl.BlockSpec((s, HEAD_DIM), lambda h, i: (0, k_col + h)),
                  pl.BlockSpec((HEAD_DIM, s), lambda h, i: (v_row + h, 0)),
                  pl.BlockSpec((tq, HEAD_DIM), lambda h, i: (i, g_col + h)),
                  pl.BlockSpec((4, D_QK), lambda h, i: (0, 0)),
                  pl.BlockSpec((1, HEAD_DIM), lambda h, i: (0, 0))],
        out_specs=pl.BlockSpec((tq, HEAD_DIM), lambda h, i: (i, h)),
        scratch_shapes=[row_stat, acc, row_stat, acc] + [pltpu.VMEM((tk, tq), F32)] * 4,
        compiler_params=_params("parallel", "parallel"),
        name="mixer_d",
    )(qk_arr, qk_arr, vt, g_arr, lam_vecs, subln.reshape(1, HEAD_DIM))


def _out_proj_kernel(ya_ref, yb_ref, w_ref, h_ref, g_ref, o_ref):
    half = ya_ref.shape[1]
    y = (jnp.dot(ya_ref[...], w_ref[:half, :], preferred_element_type=F32)
         + jnp.dot(yb_ref[...], w_ref[half:, :], preferred_element_type=F32))
    ms = jnp.mean(y * y, axis=-1, keepdims=True)
    o_ref[...] = h_ref[...] + y * lax.rsqrt(ms + EPS) * g_ref[...]


def _out_proj(ya, yb, w, h, g):
    s, d = h.shape
    tm = min(512, s)
    return pl.pallas_call(
        _out_proj_kernel,
        out_shape=jax.ShapeDtypeStruct((s, d), F32),
        grid=(s // tm,),
        in_specs=[pl.BlockSpec((tm, BRANCH), lambda i: (i, 0)),
                  pl.BlockSpec((tm, BRANCH), lambda i: (i, 0)),
                  pl.BlockSpec((2 * BRANCH, d), lambda i: (0, 0)),
                  pl.BlockSpec((tm, d), lambda i: (i, 0)),
                  pl.BlockSpec((1, d), lambda i: (0, 0))],
        out_specs=pl.BlockSpec((tm, d), lambda i: (i, 0)),
        compiler_params=_params("parallel"),
        name="out_proj",
    )(ya, yb, w, h, g.reshape(1, d))


def _rel_bias_tile(rel_bias, tq):
    rows = 2 * A_PAD + tq
    period = rows + tq
    k = np.arange(period)
    c_minus_r = np.where(k < rows, k, k - period)
    line = rel_bias.astype(F32)[:, np.clip(A_PAD - c_minus_r, -REL_CLIP, REL_CLIP) + REL_CLIP] * LOG2E
    flat = jnp.tile(line, (1, tq))[:, :tq * (period - 1)]
    tile = flat.reshape(rel_bias.shape[0], tq, period - 1)[:, :, :rows]
    qc = np.arange(tq)[:, None] >> CHUNK_SHIFT
    kc = np.arange(rows)[None, :] >> CHUNK_SHIFT
    in_band = (kc >= qc) & (kc <= qc + A_LEFT_CHUNKS)
    return jnp.swapaxes(jnp.where(in_band, tile, NEG_INF), 1, 2)


def _even_layer(h, w_in, rel_bias, w_out, norm_pre, norm_post):
    s = h.shape[0]
    u = _rmsnorm(h, norm_pre)
    b = BRANCH
    log2_scale = HEAD_DIM ** -0.5 * LOG2E
    w_rows = jnp.concatenate([w_in[:, :b] * log2_scale, w_in[:, b:2 * b],
                              w_in[:, 4 * b:5 * b] * log2_scale, w_in[:, 5 * b:6 * b]], axis=1).astype(BF16)
    w_g = jnp.concatenate([w_in[:, 3 * b:4 * b], w_in[:, 7 * b:]], axis=1).astype(BF16)
    w_vt = jnp.concatenate([w_in[:, 2 * b:3 * b], w_in[:, 6 * b:7 * b]], axis=1).T.astype(BF16)
    rows = _matmul(u, w_rows, BF16)
    gates = _matmul(u, w_g, F32)
    vt = _matmul_t(w_vt, u, BF16)
    tq_a = min(256, s)
    y_a = _mixer_a(rows, 0, N_HEADS, vt, 0, gates, 0, _rel_bias_tile(rel_bias, tq_a), tq_a)
    y_b = _mixer_b(rows, 2 * N_HEADS, 3 * N_HEADS, vt, N_HEADS, gates, N_HEADS, min(512, s), min(256, s))
    return _out_proj(y_a, y_b, w_out.astype(BF16), h, norm_post)


def _odd_layer(h, positions, w_in, forget_bias, lq1, lk1, lq2, lk2, subln, w_out, norm_pre, norm_post,
               lambda_init):
    s, d = h.shape
    u = _rmsnorm(h, norm_pre)
    b = BRANCH
    f0 = 4 * b
    d0 = f0 + N_HEADS
    v0 = d0 + 4 * N_HEADS * D_QK
    w_rows = jnp.concatenate([w_in[:, :b] * (HEAD_DIM ** -0.5 * LOG2E), w_in[:, b:2 * b]],
                             axis=1).astype(BF16)
    w_g = jnp.concatenate([w_in[:, 3 * b:4 * b], w_in[:, v0 + b:]], axis=1).astype(BF16)
    w_vt = jnp.concatenate([w_in[:, 2 * b:3 * b], w_in[:, v0:v0 + b]], axis=1).T.astype(BF16)
    wd = w_in[:, d0:v0].reshape(d, 4, N_HEADS, D_QK)
    w_qk = jnp.concatenate([
        jnp.concatenate([wd[:, 0], wd[:, 1]], axis=2).reshape(d, b) * (D_QK ** -0.5 * LOG2E),
        jnp.concatenate([wd[:, 2], wd[:, 3]], axis=2).reshape(d, b),
    ], axis=1).astype(BF16)
    w_f = jnp.pad(w_in[:, f0:d0], ((0, 0), (0, HEAD_DIM - N_HEADS))).astype(BF16)
    b_f = jnp.pad(forget_bias.astype(F32), (0, HEAD_DIM - N_HEADS)).reshape(1, HEAD_DIM)

    rows = _matmul(u, w_rows, BF16)
    gates = _matmul(u, w_g, F32)
    vt = _matmul_t(w_vt, u, BF16)
    qk_d = _matmul(u, w_qk, F32)

    qx, kx = _decay_features(u, w_f, b_f)
    tq = tk = min(512, s)
    y_c = _mixer_c(rows, 0, N_HEADS, qx, kx, vt, 0, gates, 0, tq, tk)

    qk_rot = _rope(qk_d, positions)
    lam_vecs = jnp.stack([lq1, lk1, lq2, lk2]).astype(F32)
    y_d = _mixer_d(qk_rot, 0, N_HEADS, vt, N_HEADS, gates, N_HEADS, lam_vecs, subln, lambda_init, tq, tk)
    return _out_proj(y_c, y_d, w_out.astype(BF16), h, norm_post)


def kernel(x, positions, even_w_in, even_rel_bias, even_w_out, even_norm_pre, even_norm_post,
           odd_w_in, odd_forget_bias, odd_lambda_q1, odd_lambda_k1, odd_lambda_q2, odd_lambda_k2,
           odd_subln, odd_w_out, odd_norm_pre, odd_norm_post):
    assert x.shape[0] == 1
    h = x[0]
    depth = even_w_in.shape[0] + odd_w_in.shape[0]
    for layer in range(depth):
        i = layer // 2
        if layer % 2 == 0:
            h = _even_layer(h, even_w_in[i], even_rel_bias[i], even_w_out[i],
                            even_norm_pre[i], even_norm_post[i])
        else:
            lambda_init = 0.8 - 0.6 * math.exp(-0.3 * layer)
            h = _odd_layer(h, positions, odd_w_in[i], odd_forget_bias[i],
                           odd_lambda_q1[i], odd_lambda_k1[i], odd_lambda_q2[i], odd_lambda_k2[i],
                           odd_subln[i], odd_w_out[i], odd_norm_pre[i], odd_norm_post[i], lambda_init)
    return h[None]
```

```python
import functools
import math

import jax
import jax.numpy as jnp
import numpy as np
from jax import lax
from jax.experimental import pallas as pl
from jax.experimental.pallas import tpu as pltpu

F32 = jnp.float32
BF16 = jnp.bfloat16

HEAD_DIM = 128
N_HEADS = 8
BRANCH = N_HEADS * HEAD_DIM
CHUNK = 64
CHUNK_SHIFT = 6
A_LEFT_CHUNKS = 8
A_PAD = A_LEFT_CHUNKS * CHUNK
REL_CLIP = 128
D_QK = 64
ROPE_THETA = 500000.0
ROPE_DIM = 16
EPS = 1e-6
NEG_INF = float("-inf")
LOG2E = math.log2(math.e)
BF16_SUBLANES = 16
VMEM_LIMIT_BYTES = 48 * 1024 * 1024

_NT = (((1,), (1,)), ((), ()))


def _params(*semantics):
    return pltpu.CompilerParams(dimension_semantics=semantics, vmem_limit_bytes=VMEM_LIMIT_BYTES)


def _silu(g):
    return g * jax.nn.sigmoid(g)


def _log_sigmoid(z):
    return jnp.minimum(z, 0.0) - jnp.log1p(jnp.exp(-jnp.abs(z)))


def _rmsnorm_kernel(x_ref, g_ref, o_ref):
    x = x_ref[...]
    ms = jnp.mean(x * x, axis=-1, keepdims=True)
    o_ref[...] = (x * lax.rsqrt(ms + EPS) * g_ref[...]).astype(o_ref.dtype)


def _rmsnorm(x, g):
    s, d = x.shape
    tm = min(512, s)
    return pl.pallas_call(
        _rmsnorm_kernel,
        out_shape=jax.ShapeDtypeStruct((s, d), BF16),
        grid=(s // tm,),
        in_specs=[pl.BlockSpec((tm, d), lambda i: (i, 0)),
                  pl.BlockSpec((1, d), lambda i: (0, 0))],
        out_specs=pl.BlockSpec((tm, d), lambda i: (i, 0)),
        compiler_params=_params("parallel"),
        name="rmsnorm",
    )(x, g.reshape(1, d))


def _matmul_kernel(x_ref, w_ref, o_ref):
    o_ref[...] = jnp.dot(x_ref[...], w_ref[...], preferred_element_type=F32).astype(o_ref.dtype)


def _matmul(x, w, out_dtype):
    m, k = x.shape
    n = w.shape[1]
    tm = min(1024, m)
    tn = 1024 if n % 1024 == 0 else n
    return pl.pallas_call(
        _matmul_kernel,
        out_shape=jax.ShapeDtypeStruct((m, n), out_dtype),
        grid=(n // tn, m // tm),
        in_specs=[pl.BlockSpec((tm, k), lambda j, i: (i, 0)),
                  pl.BlockSpec((k, tn), lambda j, i: (0, j))],
        out_specs=pl.BlockSpec((tm, tn), lambda j, i: (i, j)),
        compiler_params=_params("parallel", "parallel"),
        name="in_proj",
    )(x, w)


def _matmul_t_kernel(w_ref, x_ref, o_ref):
    o_ref[...] = lax.dot_general(w_ref[...], x_ref[...], _NT,
                                 preferred_element_type=F32).astype(o_ref.dtype)


def _matmul_t(w_t, x, out_dtype):
    n, k = w_t.shape
    m = x.shape[0]
    tm = min(1024, m)
    tn = 1024 if n % 1024 == 0 else n
    return pl.pallas_call(
        _matmul_t_kernel,
        out_shape=jax.ShapeDtypeStruct((n, m), out_dtype),
        grid=(n // tn, m // tm),
        in_specs=[pl.BlockSpec((tn, k), lambda j, i: (j, 0)),
                  pl.BlockSpec((tm, k), lambda j, i: (i, 0))],
        out_specs=pl.BlockSpec((tn, tm), lambda j, i: (j, i)),
        compiler_params=_params("parallel", "parallel"),
        name="in_proj_t",
    )(w_t, x)


def _mixer_a_kernel(q_ref, k_ref, vt_ref, g_ref, b_ref, o_ref, *, tq, band):
    ones = _ones_rows(band)
    for sub in range(q_ref.shape[0] // tq):
        rows = slice(sub * tq, (sub + 1) * tq)
        first = (pl.program_id(1) * (q_ref.shape[0] // tq) + sub) * tq - A_PAD
        start = pl.multiple_of(jnp.maximum(first, 0), tq)
        cut = pl.multiple_of(start - first, tq)
        s = (lax.dot_general(k_ref[pl.ds(start, band), :], q_ref[rows, :], _NT, preferred_element_type=F32)
             + b_ref[0, pl.ds(cut, band), :])
        m = jnp.max(s, axis=0, keepdims=True)
        p = jnp.exp2(s - m).astype(BF16)
        v_aug = jnp.concatenate([vt_ref[:, pl.ds(start, band)], ones], axis=0)
        acc = jnp.dot(v_aug, p, preferred_element_type=F32)
        o = (acc[:HEAD_DIM] / acc[HEAD_DIM:HEAD_DIM + 1]).T
        o_ref[rows, :] = (o * _silu(g_ref[rows, :])).astype(o_ref.dtype)


def _mixer_a(qk, q_col, k_col, vt, v_row, g_arr, g_col, bias_tile, tq):
    s = qk.shape[0]
    assert A_PAD % tq == 0
    band = A_PAD + tq
    tstep = min(4 * tq, s)
    kern = functools.partial(_mixer_a_kernel, tq=tq, band=band)
    return pl.pallas_call(
        kern,
        out_shape=jax.ShapeDtypeStruct((s, BRANCH), BF16),
        grid=(N_HEADS, s // tstep),
        in_specs=[pl.BlockSpec((tstep, HEAD_DIM), lambda h, i: (i, q_col + h)),
                  pl.BlockSpec((s, HEAD_DIM), lambda h, i: (0, k_col + h)),
                  pl.BlockSpec((HEAD_DIM, s), lambda h, i: (v_row + h, 0)),
                  pl.BlockSpec((tstep, HEAD_DIM), lambda h, i: (i, g_col + h)),
                  pl.BlockSpec((1, band + A_PAD, tq), lambda h, i: (h, 0, 0))],
        out_specs=pl.BlockSpec((tstep, HEAD_DIM), lambda h, i: (i, h)),
        compiler_params=_params("parallel", "parallel"),
        name="mixer_a",
    )(qk, qk, vt, g_arr, bias_tile)


def _key_query_index(tk, tq, key_off):
    key = lax.broadcasted_iota(jnp.int32, (tk, tq), 0) + key_off
    query = lax.broadcasted_iota(jnp.int32, (tk, tq), 1)
    return key, query


def _softmax_step(s, masked_out, v_aug, m_sc, acc_sc):
    if masked_out is not None:
        s = jnp.where(masked_out, NEG_INF, s)
    m_old = m_sc[...]
    m_new = jnp.maximum(m_old, jnp.max(s, axis=0, keepdims=True))
    p = jnp.exp2(s - m_new).astype(BF16)
    alpha = jnp.exp2(m_old - m_new)
    acc_sc[...] = alpha * acc_sc[...] + jnp.dot(v_aug, p, preferred_element_type=F32)
    m_sc[...] = m_new


def _init_softmax_state(m_sc, acc_sc):
    m_sc[...] = jnp.full_like(m_sc, NEG_INF)
    acc_sc[...] = jnp.zeros_like(acc_sc)


def _normalised_rows(acc_sc):
    acc = acc_sc[...]
    return (acc[:HEAD_DIM] / acc[HEAD_DIM:HEAD_DIM + 1]).T


def _ones_rows(width):
    return jnp.ones((BF16_SUBLANES, width), BF16)


def _pipelined_sweep(head, count, rest_block, scores_into, consume, s_bufs,
                     accumulate=None, w_bufs=(None, None), count_is_even=False, steps_per_body=None):
    n_head = len(head)
    n_s = len(s_bufs)
    lookahead = n_s - 1
    last_head = head[-1][0]
    defer_accumulate = accumulate is not None
    assert count_is_even or not defer_accumulate
    unit = 2 if count_is_even else 1
    period = math.lcm(n_s, 2) if defer_accumulate else n_s
    steps = steps_per_body or period
    assert steps % period == 0 and steps % unit == 0

    def block_at(g):
        return head[g][0] if g < n_head else rest_block(g - n_head)

    def step(g_mod, block, mask, prev_block, next_block):
        scores_into(s_bufs[(g_mod + lookahead) % n_s], next_block)
        consume(s_bufs[g_mod % n_s], w_bufs[g_mod % 2], block, mask)
        if defer_accumulate and prev_block is not None:
            accumulate(w_bufs[(g_mod - 1) % 2], prev_block)

    def rest_step(first, u):
        n = first + u
        prev_block = rest_block(n - 1) if u else jnp.where(n == 0, last_head, rest_block(n - 1))
        step(n_head + u, rest_block(n), None, prev_block, rest_block(n + lookahead))

    for g in range(lookahead):
        scores_into(s_bufs[g % n_s], block_at(g))
    for g, (block, mask) in enumerate(head):
        step(g, block, mask, head[g - 1][0] if g else None, block_at(g + lookahead))

    def body(_, first):
        for u in range(steps):
            rest_step(first, u)
        return first + steps

    done = lax.fori_loop(0, count // steps, body, 0)
    for t in range(0, steps - unit, unit):
        @pl.when(count - done > t)
        def _():
            for u in range(t, t + unit):
                rest_step(done, u)

    if defer_accumulate:
        accumulate(w_bufs[(n_head - 1) % 2], jnp.where(count == 0, last_head, rest_block(count - 1)))


def _mixer_b_kernel(q_ref, k_ref, vt_ref, g_ref, o_ref, acc_sc, run_sc, s0_sc, s1_sc,
                    w0_sc, c0_sc, w1_sc, c1_sc, *, tq, tk):
    per_q = tq // tk
    assert tq == per_q * tk and per_q % 2 == 0
    i = pl.program_id(1)
    q = q_ref[...]
    kk = lax.broadcasted_iota(jnp.int32, (tk + BF16_SUBLANES, tk), 0)
    jj = lax.broadcasted_iota(jnp.int32, (tk + BF16_SUBLANES, tk), 1)
    suffix = jnp.where((jj > kk) | (kk >= tk), 1.0, 0.0).astype(BF16)
    acc_sc[...] = jnp.zeros_like(acc_sc)
    run_sc[...] = jnp.zeros_like(run_sc)

    def scores_into(buf, j):
        off = pl.multiple_of(j * tk, tk)
        buf[...] = lax.dot_general(k_ref[pl.ds(off, tk), :], q, _NT, preferred_element_type=F32)

    def weights_from(buf, w_buf, j, key_off):
        w_ref, carried_ref = w_buf
        z = buf[...]
        neg_abs = lax.bitcast_convert_type(
            lax.bitcast_convert_type(z, jnp.uint32) | jnp.uint32(0x80000000), F32)
        log_beta = jnp.minimum(z, 0.0) - jnp.log(1.0 + jnp.exp2(neg_abs)) * LOG2E
        log_keep = log_beta - z
        if key_off is not None:
            key, query = _key_query_index(tk, tq, key_off)
            strict = key < query
            log_keep = jnp.where(strict, log_keep, 0.0)
        later = jnp.dot(suffix, log_keep.astype(BF16), preferred_element_type=F32)
        w = jnp.exp2(log_beta + later[:tk])
        if key_off is not None:
            w = jnp.where(strict, w, 0.0)
        w_ref[...] = w.astype(BF16)
        carried_ref[...] = jnp.exp2(run_sc[...])
        run_sc[...] += later[tk:tk + 1]

    def accumulate(w_buf, j):
        w_ref, carried_ref = w_buf
        off = pl.multiple_of(j * tk, tk)
        acc_sc[...] += carried_ref[...] * jnp.dot(vt_ref[:, pl.ds(off, tk)], w_ref[...],
                                                  preferred_element_type=F32)

    def below(n):
        return jnp.clip(per_q * i - 1 - n, 0, per_q * i)

    on_diagonal = [(per_q * i + d, d * tk) for d in reversed(range(per_q))]
    _pipelined_sweep(on_diagonal, per_q * i, below, scores_into, weights_from,
                     (s0_sc, s1_sc), accumulate, ((w0_sc, c0_sc), (w1_sc, c1_sc)), count_is_even=True,
                     steps_per_body=4)
    o_ref[...] = (acc_sc[...].T * _silu(g_ref[...])).astype(o_ref.dtype)


def _mixer_b(qk, q_col, k_col, vt, v_row, g_arr, g_col, tq, tk):
    s = qk.shape[0]
    kern = functools.partial(_mixer_b_kernel, tq=tq, tk=tk)
    return pl.pallas_call(
        kern,
        out_shape=jax.ShapeDtypeStruct((s, BRANCH), BF16),
        grid=(N_HEADS, s // tq),
        in_specs=[pl.BlockSpec((tq, HEAD_DIM), lambda h, i: (i, q_col + h)),
                  pl.BlockSpec((s, HEAD_DIM), lambda h, i: (0, k_col + h)),
                  pl.BlockSpec((HEAD_DIM, s), lambda h, i: (v_row + h, 0)),
                  pl.BlockSpec((tq, HEAD_DIM), lambda h, i: (i, g_col + h))],
        out_specs=pl.BlockSpec((tq, HEAD_DIM), lambda h, i: (i, h)),
        scratch_shapes=[pltpu.VMEM((HEAD_DIM, tq), F32), pltpu.VMEM((1, tq), F32),
                        pltpu.VMEM((tk, tq), F32), pltpu.VMEM((tk, tq), F32),
                        pltpu.VMEM((tk, tq), BF16), pltpu.VMEM((1, tq), F32),
                        pltpu.VMEM((tk, tq), BF16), pltpu.VMEM((1, tq), F32)],
        compiler_params=_params("parallel", "parallel"),
        name="mixer_b",
    )(qk, qk, vt, g_arr)


def _decay_features_kernel(u_ref, wf_ref, b_ref, place_ref, ones_ref, qx_ref, kx_ref, carry_sc):
    tm = u_ref.shape[0]

    @pl.when(pl.program_id(0) == 0)
    def _():
        carry_sc[...] = jnp.zeros_like(carry_sc)

    log_f = _log_sigmoid(jnp.dot(u_ref[...], wf_ref[...], preferred_element_type=F32) + b_ref[...])
    r = lax.broadcasted_iota(jnp.int32, (tm, tm), 0)
    c = lax.broadcasted_iota(jnp.int32, (tm, tm), 1)
    upto = jnp.where(c <= r, 1.0, 0.0).astype(F32)
    cum = jnp.dot(upto, log_f, preferred_element_type=F32,
                  precision=lax.Precision.HIGHEST) + carry_sc[...]
    carry_sc[...] = cum[tm - 1:tm, :]

    c2 = cum * LOG2E
    hi = c2.astype(BF16)
    r1 = c2 - hi.astype(F32)
    mid = r1.astype(BF16)
    lo = (r1 - mid.astype(F32)).astype(BF16)
    placed = jnp.dot(jnp.concatenate([hi, mid, lo], axis=1), place_ref[...],
                     preferred_element_type=F32) + ones_ref[...]
    qx_ref[...] = placed[:, :BRANCH].astype(BF16)
    kx_ref[...] = placed[:, BRANCH:].astype(BF16)


def _decay_placement():
    place = np.zeros((3 * HEAD_DIM, 2 * BRANCH), np.float32)
    ones = np.zeros((1, 2 * BRANCH), np.float32)
    for head in range(N_HEADS):
        for term in range(3):
            place[term * HEAD_DIM + head, head * HEAD_DIM + term] = 1.0
            place[term * HEAD_DIM + head, BRANCH + head * HEAD_DIM + 3 + term] = -1.0
            ones[0, head * HEAD_DIM + 3 + term] = 1.0
            ones[0, BRANCH + head * HEAD_DIM + term] = 1.0
    return jnp.asarray(place, BF16), jnp.asarray(ones, F32)


def _decay_features(u, wf, bias):
    s, d = u.shape
    tm = min(512, s)
    place, ones = _decay_placement()
    out = jax.ShapeDtypeStruct((s, BRANCH), BF16)
    spec = pl.BlockSpec((tm, BRANCH), lambda i: (i, 0))
    return pl.pallas_call(
        _decay_features_kernel,
        out_shape=(out, out),
        grid=(s // tm,),
        in_specs=[pl.BlockSpec((tm, d), lambda i: (i, 0)),
                  pl.BlockSpec((d, HEAD_DIM), lambda i: (0, 0)),
                  pl.BlockSpec((1, HEAD_DIM), lambda i: (0, 0)),
                  pl.BlockSpec(place.shape, lambda i: (0, 0)),
                  pl.BlockSpec(ones.shape, lambda i: (0, 0))],
        out_specs=(spec, spec),
        scratch_shapes=[pltpu.VMEM((1, HEAD_DIM), F32)],
        compiler_params=_params("arbitrary"),
        name="decay_features",
    )(u, wf, bias, place, ones)


def _causal_sweep(i, tq, tk, masked_out_fn, scores_into, consume, s_bufs):
    per_q = tq // tk
    assert tq == per_q * tk and per_q in (1, 2)
    head = [(per_q * i + d, masked_out_fn(*_key_query_index(tk, tq, d * tk))) for d in range(per_q)]
    _pipelined_sweep(head, per_q * i, lambda n: jnp.clip(n, 0, per_q * i), scores_into, consume, s_bufs,
                     count_is_even=per_q == 2, steps_per_body=4)


def _mixer_c_kernel(q_ref, qx_ref, k_ref, kx_ref, vt_ref, g_ref, o_ref, m_sc, acc_sc, s0_sc, s1_sc,
                    *, tq, tk):
    q = jnp.concatenate([q_ref[...], qx_ref[...]], axis=1)
    ones = _ones_rows(tk)
    _init_softmax_state(m_sc, acc_sc)

    def scores_into(buf, j):
        off = pl.multiple_of(j * tk, tk)
        k = jnp.concatenate([k_ref[pl.ds(off, tk), :], kx_ref[pl.ds(off, tk), :]], axis=1)
        buf[...] = lax.dot_general(k, q, _NT, preferred_element_type=F32)

    def consume(buf, _, j, masked_out):
        off = pl.multiple_of(j * tk, tk)
        v_aug = jnp.concatenate([vt_ref[:, pl.ds(off, tk)], ones], axis=0)
        _softmax_step(buf[...], masked_out, v_aug, m_sc, acc_sc)

    _causal_sweep(pl.program_id(1), tq, tk, lambda key, query: key > query, scores_into, consume,
                  (s0_sc, s1_sc))
    o_ref[...] = (_normalised_rows(acc_sc) * _silu(g_ref[...])).astype(o_ref.dtype)


def _mixer_c(qk, q_col, k_col, qx, kx, vt, v_row, g_arr, g_col, tq, tk):
    s = qk.shape[0]
    return pl.pallas_call(
        functools.partial(_mixer_c_kernel, tq=tq, tk=tk),
        out_shape=jax.ShapeDtypeStruct((s, BRANCH), BF16),
        grid=(N_HEADS, s // tq),
        in_specs=[pl.BlockSpec((tq, HEAD_DIM), lambda h, i: (i, q_col + h)),
                  pl.BlockSpec((tq, HEAD_DIM), lambda h, i: (i, h)),
                  pl.BlockSpec((s, HEAD_DIM), lambda h, i: (0, k_col + h)),
                  pl.BlockSpec((s, HEAD_DIM), lambda h, i: (0, h)),
                  pl.BlockSpec((HEAD_DIM, s), lambda h, i: (v_row + h, 0)),
                  pl.BlockSpec((tq, HEAD_DIM), lambda h, i: (i, g_col + h))],
        out_specs=pl.BlockSpec((tq, HEAD_DIM), lambda h, i: (i, h)),
        scratch_shapes=[pltpu.VMEM((1, tq), F32),
                        pltpu.VMEM((HEAD_DIM + BF16_SUBLANES, tq), F32),
                        pltpu.VMEM((tk, tq), F32), pltpu.VMEM((tk, tq), F32)],
        compiler_params=_params("parallel", "parallel"),
        name="mixer_c",
    )(qk, qx, qk, kx, vt, g_arr)


def _rope_kernel(x_ref, pos_ref, invf_ref, place_ref, o_ref, *, groups):
    ang = invf_ref[...] * pos_ref[...].astype(F32)
    place = place_ref[...]
    cos = jnp.dot(place, jnp.cos(ang), preferred_element_type=F32, precision=lax.Precision.HIGHEST).T
    sin = jnp.dot(place, jnp.sin(ang), preferred_element_type=F32, precision=lax.Precision.HIGHEST).T
    lane = lax.broadcasted_iota(jnp.int32, cos.shape, 1)
    first_half = (lane & (D_QK - 1)) < ROPE_DIM // 2
    for gidx in range(groups):
        x = x_ref[:, gidx * 128:(gidx + 1) * 128]
        partner = jnp.where(first_half,
                            -pltpu.roll(x, 128 - ROPE_DIM // 2, 1),
                            pltpu.roll(x, ROPE_DIM // 2, 1))
        o_ref[:, gidx * 128:(gidx + 1) * 128] = (x * cos + partner * sin).astype(o_ref.dtype)


def _rope(x, positions):
    s, n = x.shape
    tm = min(512, s)
    n_freq = ROPE_DIM // 2
    inv_freq = ROPE_THETA ** (-jnp.arange(0, ROPE_DIM, 2, dtype=F32) / ROPE_DIM)
    invf = jnp.pad(inv_freq, (0, n_freq)).reshape(2 * n_freq, 1)
    slot = np.arange(128) % D_QK
    place = np.zeros((128, 2 * n_freq), np.float32)
    place[np.arange(128), np.where(slot < ROPE_DIM, slot % n_freq, n_freq)] = 1.0
    return pl.pallas_call(
        functools.partial(_rope_kernel, groups=n // 128),
        out_shape=jax.ShapeDtypeStruct((s, n), BF16),
        grid=(s // tm,),
        in_specs=[pl.BlockSpec((tm, n), lambda i: (i, 0)),
                  pl.BlockSpec((1, tm), lambda i: (0, i)),
                  pl.BlockSpec(invf.shape, lambda i: (0, 0)),
                  pl.BlockSpec(place.shape, lambda i: (0, 0))],
        out_specs=pl.BlockSpec((tm, n), lambda i: (i, 0)),
        compiler_params=_params("parallel"),
        name="rope",
    )(x, positions, invf, jnp.asarray(place))


def _mixer_d_kernel(q_ref, k_ref, vt_ref, g_ref, lam_ref, sub_ref, o_ref,
                    m1_sc, a1_sc, m2_sc, a2_sc, sa1_sc, sa2_sc, sb1_sc, sb2_sc, *, tq, tk, lambda_init):
    q = q_ref[...]
    lane = lax.broadcasted_iota(jnp.int32, q.shape, 1)
    zero = jnp.zeros_like(q)
    q1 = jnp.where(lane < D_QK, q, zero)
    q2 = jnp.where(lane >= D_QK, q, zero)
    ones = _ones_rows(tk)
    _init_softmax_state(m1_sc, a1_sc)
    _init_softmax_state(m2_sc, a2_sc)

    def scores_into(bufs, j):
        off = pl.multiple_of(j * tk, tk)
        k = k_ref[pl.ds(off, tk), :]
        for qm, buf in zip((q1, q2), bufs):
            buf[...] = lax.dot_general(k, qm, _NT, preferred_element_type=F32)

    def consume(bufs, _, j, masked_out):
        off = pl.multiple_of(j * tk, tk)
        v_aug = jnp.concatenate([vt_ref[:, pl.ds(off, tk)], ones], axis=0)
        for buf, m_sc, a_sc in zip(bufs, (m1_sc, m2_sc), (a1_sc, a2_sc)):
            _softmax_step(buf[...], masked_out, v_aug, m_sc, a_sc)

    _causal_sweep(pl.program_id(1), tq, tk,
                  lambda key, query: (key >> CHUNK_SHIFT) > (query >> CHUNK_SHIFT),
                  scores_into, consume, ((sa1_sc, sa2_sc), (sb1_sc, sb2_sc)))

    lv = lam_ref[...]
    lam = (jnp.exp(jnp.sum(lv[0:1] * lv[1:2], axis=1, keepdims=True))
           - jnp.exp(jnp.sum(lv[2:3] * lv[3:4], axis=1, keepdims=True)) + lambda_init)
    o = _normalised_rows(a1_sc) - lam * _normalised_rows(a2_sc)
    ms = jnp.mean(o * o, axis=-1, keepdims=True)
    y = o * lax.rsqrt(ms + EPS) * sub_ref[...] * (1.0 - lambda_init)
    o_ref[...] = (y * _silu(g_ref[...])).astype(o_ref.dtype)


def _mixer_d(qk_arr, q_col, k_col, vt, v_row, g_arr, g_col, lam_vecs, subln, lambda_init, tq, tk):
    s = qk_arr.shape[0]
    kern = functools.partial(_mixer_d_kernel, tq=tq, tk=tk, lambda_init=lambda_init)
    row_stat = pltpu.VMEM((1, tq), F32)
    acc = pltpu.VMEM((HEAD_DIM + BF16_SUBLANES, tq), F32)
    return pl.pallas_call(
        kern,
        out_shape=jax.ShapeDtypeStruct((s, BRANCH), BF16),
        grid=(N_HEADS, s // tq),
        in_specs=[pl.BlockSpec((tq, HEAD_DIM), lambda h, i: (i, q_col + h)),
                  pl.BlockSpec((s, HEAD_DIM), lambda h, i: (0, k_col + h)),
                  pl.BlockSpec((HEAD_DIM, s), lambda h, i: (v_row + h, 0)),
                  pl.BlockSpec((tq, HEAD_DIM), lambda h, i: (i, g_col + h)),
                  pl.BlockSpec((4, D_QK), lambda h, i: (0, 0)),
                  pl.BlockSpec((1, HEAD_DIM), lambda h, i: (0, 0))],
        out_specs=pl.BlockSpec((tq, HEAD_DIM), lambda h, i: (i, h)),
        scratch_shapes=[row_stat, acc, row_stat, acc] + [pltpu.VMEM((tk, tq), F32)] * 4,
        compiler_params=_params("parallel", "parallel"),
        name="mixer_d",
    )(qk_arr, qk_arr, vt, g_arr, lam_vecs, subln.reshape(1, HEAD_DIM))


def _out_proj_kernel(ya_ref, yb_ref, w_ref, h_ref, g_ref, o_ref):
    half = ya_ref.shape[1]
    y = (jnp.dot(ya_ref[...], w_ref[:half, :], preferred_element_type=F32)
         + jnp.dot(yb_ref[...], w_ref[half:, :], preferred_element_type=F32))
    ms = jnp.mean(y * y, axis=-1, keepdims=True)
    o_ref[...] = h_ref[...] + y * lax.rsqrt(ms + EPS) * g_ref[...]


def _out_proj(ya, yb, w, h, g):
    s, d = h.shape
    tm = min(512, s)
    return pl.pallas_call(
        _out_proj_kernel,
        out_shape=jax.ShapeDtypeStruct((s, d), F32),
        grid=(s // tm,),
        in_specs=[pl.BlockSpec((tm, BRANCH), lambda i: (i, 0)),
                  pl.BlockSpec((tm, BRANCH), lambda i: (i, 0)),
                  pl.BlockSpec((2 * BRANCH, d), lambda i: (0, 0)),
                  pl.BlockSpec((tm, d), lambda i: (i, 0)),
                  pl.BlockSpec((1, d), lambda i: (0, 0))],
        out_specs=pl.BlockSpec((tm, d), lambda i: (i, 0)),
        compiler_params=_params("parallel"),
        name="out_proj",
    )(ya, yb, w, h, g.reshape(1, d))


def _rel_bias_tile(rel_bias, tq):
    rows = 2 * A_PAD + tq
    period = rows + tq
    k = np.arange(period)
    c_minus_r = np.where(k < rows, k, k - period)
    line = rel_bias.astype(F32)[:, np.clip(A_PAD - c_minus_r, -REL_CLIP, REL_CLIP) + REL_CLIP] * LOG2E
    flat = jnp.tile(line, (1, tq))[:, :tq * (period - 1)]
    tile = flat.reshape(rel_bias.shape[0], tq, period - 1)[:, :, :rows]
    qc = np.arange(tq)[:, None] >> CHUNK_SHIFT
    kc = np.arange(rows)[None, :] >> CHUNK_SHIFT
    in_band = (kc >= qc) & (kc <= qc + A_LEFT_CHUNKS)
    return jnp.swapaxes(jnp.where(in_band, tile, NEG_INF), 1, 2)


def _even_layer(h, w_in, rel_bias, w_out, norm_pre, norm_post):
    s = h.shape[0]
    u = _rmsnorm(h, norm_pre)
    b = BRANCH
    log2_scale = HEAD_DIM ** -0.5 * LOG2E
    w_rows = jnp.concatenate([w_in[:, :b] * log2_scale, w_in[:, b:2 * b],
                              w_in[:, 4 * b:5 * b] * log2_scale, w_in[:, 5 * b:6 * b]], axis=1).astype(BF16)
    w_g = jnp.concatenate([w_in[:, 3 * b:4 * b], w_in[:, 7 * b:]], axis=1).astype(BF16)
    w_vt = jnp.concatenate([w_in[:, 2 * b:3 * b], w_in[:, 6 * b:7 * b]], axis=1).astype(BF16).T
    rows = _matmul(u, w_rows, BF16)
    gates = _matmul(u, w_g, F32)
    vt = _matmul_t(w_vt, u, BF16)
    tq_a = min(256, s)
    y_a = _mixer_a(rows, 0, N_HEADS, vt, 0, gates, 0, _rel_bias_tile(rel_bias, tq_a), tq_a)
    y_b = _mixer_b(rows, 2 * N_HEADS, 3 * N_HEADS, vt, N_HEADS, gates, N_HEADS, min(512, s), min(256, s))
    return _out_proj(y_a, y_b, w_out.astype(BF16), h, norm_post)


def _odd_layer(h, positions, w_in, forget_bias, lq1, lk1, lq2, lk2, subln, w_out, norm_pre, norm_post,
               lambda_init):
    s, d = h.shape
    u = _rmsnorm(h, norm_pre)
    b = BRANCH
    f0 = 4 * b
    d0 = f0 + N_HEADS
    v0 = d0 + 4 * N_HEADS * D_QK
    w_rows = jnp.concatenate([w_in[:, :b] * (HEAD_DIM ** -0.5 * LOG2E), w_in[:, b:2 * b]],
                             axis=1).astype(BF16)
    w_g = jnp.concatenate([w_in[:, 3 * b:4 * b], w_in[:, v0 + b:]], axis=1).astype(BF16)
    w_vt = jnp.concatenate([w_in[:, 2 * b:3 * b], w_in[:, v0:v0 + b]], axis=1).astype(BF16).T
    wd = w_in[:, d0:v0].reshape(d, 4, N_HEADS, D_QK)
    w_qk = jnp.concatenate([
        jnp.concatenate([wd[:, 0], wd[:, 1]], axis=2).reshape(d, b) * (D_QK ** -0.5 * LOG2E),
        jnp.concatenate([wd[:, 2], wd[:, 3]], axis=2).reshape(d, b),
    ], axis=1).astype(BF16)
    w_f = jnp.pad(w_in[:, f0:d0], ((0, 0), (0, HEAD_DIM - N_HEADS))).astype(BF16)
    b_f = jnp.pad(forget_bias.astype(F32), (0, HEAD_DIM - N_HEADS)).reshape(1, HEAD_DIM)

    rows = _matmul(u, w_rows, BF16)
    gates = _matmul(u, w_g, F32)
    vt = _matmul_t(w_vt, u, BF16)
    qk_d = _matmul(u, w_qk, F32)

    qx, kx = _decay_features(u, w_f, b_f)
    tq = tk = min(512, s)
    y_c = _mixer_c(rows, 0, N_HEADS, qx, kx, vt, 0, gates, 0, tq, tk)

    qk_rot = _rope(qk_d, positions)
    lam_vecs = jnp.stack([lq1, lk1, lq2, lk2]).astype(F32)
    y_d = _mixer_d(qk_rot, 0, N_HEADS, vt, N_HEADS, gates, N_HEADS, lam_vecs, subln, lambda_init, tq, tk)
    return _out_proj(y_c, y_d, w_out.astype(BF16), h, norm_post)


def kernel(x, positions, even_w_in, even_rel_bias, even_w_out, even_norm_pre, even_norm_post,
           odd_w_in, odd_forget_bias, odd_lambda_q1, odd_lambda_k1, odd_lambda_q2, odd_lambda_k2,
           odd_subln, odd_w_out, odd_norm_pre, odd_norm_post):
    assert x.shape[0] == 1
    h = x[0]
    depth = even_w_in.shape[0] + odd_w_in.shape[0]
    for layer in range(depth):
        i = layer // 2
        if layer % 2 == 0:
            h = _even_layer(h, even_w_in[i], even_rel_bias[i], even_w_out[i],
                            even_norm_pre[i], even_norm_post[i])
        else:
            lambda_init = 0.8 - 0.6 * math.exp(-0.3 * layer)
            h = _odd_layer(h, positions, odd_w_in[i], odd_forget_bias[i],
                           odd_lambda_q1[i], odd_lambda_k1[i], odd_lambda_q2[i], odd_lambda_k2[i],
                           odd_subln[i], odd_w_out[i], odd_norm_pre[i], odd_norm_post[i], lambda_init)
    return h[None]
```

```python
import functools
import math

import jax
import jax.numpy as jnp
import numpy as np
from jax import lax
from jax.experimental import pallas as pl
from jax.experimental.pallas import tpu as pltpu

F32 = jnp.float32
BF16 = jnp.bfloat16

HEAD_DIM = 128
N_HEADS = 8
BRANCH = N_HEADS * HEAD_DIM
CHUNK = 64
CHUNK_SHIFT = 6
A_LEFT_CHUNKS = 8
A_PAD = A_LEFT_CHUNKS * CHUNK
REL_CLIP = 128
D_QK = 64
ROPE_THETA = 500000.0
ROPE_DIM = 16
EPS = 1e-6
NEG_INF = float("-inf")
LOG2E = math.log2(math.e)
BF16_SUBLANES = 16
VMEM_LIMIT_BYTES = 48 * 1024 * 1024

_NT = (((1,), (1,)), ((), ()))


def _params(*semantics):
    return pltpu.CompilerParams(dimension_semantics=semantics, vmem_limit_bytes=VMEM_LIMIT_BYTES)


def _silu(g):
    return g * jax.nn.sigmoid(g)


def _log_sigmoid(z):
    return jnp.minimum(z, 0.0) - jnp.log1p(jnp.exp(-jnp.abs(z)))


def _rmsnorm_kernel(x_ref, g_ref, o_ref):
    x = x_ref[...]
    ms = jnp.mean(x * x, axis=-1, keepdims=True)
    o_ref[...] = (x * lax.rsqrt(ms + EPS) * g_ref[...]).astype(o_ref.dtype)


def _rmsnorm(x, g):
    s, d = x.shape
    tm = min(512, s)
    return pl.pallas_call(
        _rmsnorm_kernel,
        out_shape=jax.ShapeDtypeStruct((s, d), BF16),
        grid=(s // tm,),
        in_specs=[pl.BlockSpec((tm, d), lambda i: (i, 0)),
                  pl.BlockSpec((1, d), lambda i: (0, 0))],
        out_specs=pl.BlockSpec((tm, d), lambda i: (i, 0)),
        compiler_params=_params("parallel"),
        name="rmsnorm",
    )(x, g.reshape(1, d))


def _matmul_kernel(x_ref, w_ref, o_ref):
    o_ref[...] = jnp.dot(x_ref[...], w_ref[...], preferred_element_type=F32).astype(o_ref.dtype)


def _matmul(x, w, out_dtype):
    m, k = x.shape
    n = w.shape[1]
    tm = min(1024, m)
    tn = 1024 if n % 1024 == 0 else n
    return pl.pallas_call(
        _matmul_kernel,
        out_shape=jax.ShapeDtypeStruct((m, n), out_dtype),
        grid=(n // tn, m // tm),
        in_specs=[pl.BlockSpec((tm, k), lambda j, i: (i, 0)),
                  pl.BlockSpec((k, tn), lambda j, i: (0, j))],
        out_specs=pl.BlockSpec((tm, tn), lambda j, i: (i, j)),
        compiler_params=_params("parallel", "parallel"),
        name="in_proj",
    )(x, w)


def _matmul_t_kernel(w_ref, x_ref, o_ref):
    o_ref[...] = lax.dot_general(w_ref[...], x_ref[...], _NT,
                                 preferred_element_type=F32).astype(o_ref.dtype)


def _matmul_t(w_t, x, out_dtype):
    n, k = w_t.shape
    m = x.shape[0]
    tm = min(1024, m)
    tn = 1024 if n % 1024 == 0 else n
    return pl.pallas_call(
        _matmul_t_kernel,
        out_shape=jax.ShapeDtypeStruct((n, m), out_dtype),
        grid=(n // tn, m // tm),
        in_specs=[pl.BlockSpec((tn, k), lambda j, i: (j, 0)),
                  pl.BlockSpec((tm, k), lambda j, i: (i, 0))],
        out_specs=pl.BlockSpec((tn, tm), lambda j, i: (j, i)),
        compiler_params=_params("parallel", "parallel"),
        name="in_proj_t",
    )(w_t, x)


def _mixer_a_kernel(q_ref, k_ref, vt_ref, g_ref, b_ref, o_ref, *, tq, band):
    ones = _ones_rows(band)
    for sub in range(q_ref.shape[0] // tq):
        rows = slice(sub * tq, (sub + 1) * tq)
        first = (pl.program_id(1) * (q_ref.shape[0] // tq) + sub) * tq - A_PAD
        start = pl.multiple_of(jnp.maximum(first, 0), tq)
        cut = pl.multiple_of(start - first, tq)
        s = (lax.dot_general(k_ref[pl.ds(start, band), :], q_ref[rows, :], _NT, preferred_element_type=F32)
             + b_ref[0, pl.ds(cut, band), :])
        m = jnp.max(s, axis=0, keepdims=True)
        p = jnp.exp2(s - m).astype(BF16)
        v_aug = jnp.concatenate([vt_ref[:, pl.ds(start, band)], ones], axis=0)
        acc = jnp.dot(v_aug, p, preferred_element_type=F32)
        o = (acc[:HEAD_DIM] / acc[HEAD_DIM:HEAD_DIM + 1]).T
        o_ref[rows, :] = (o * _silu(g_ref[rows, :])).astype(o_ref.dtype)


def _mixer_a(qk, q_col, k_col, vt, v_row, g_arr, g_col, bias_tile, tq):
    s = qk.shape[0]
    assert A_PAD % tq == 0
    band = A_PAD + tq
    tstep = min(4 * tq, s)
    kern = functools.partial(_mixer_a_kernel, tq=tq, band=band)
    return pl.pallas_call(
        kern,
        out_shape=jax.ShapeDtypeStruct((s, BRANCH), BF16),
        grid=(N_HEADS, s // tstep),
        in_specs=[pl.BlockSpec((tstep, HEAD_DIM), lambda h, i: (i, q_col + h)),
                  pl.BlockSpec((s, HEAD_DIM), lambda h, i: (0, k_col + h)),
                  pl.BlockSpec((HEAD_DIM, s), lambda h, i: (v_row + h, 0)),
                  pl.BlockSpec((tstep, HEAD_DIM), lambda h, i: (i, g_col + h)),
                  pl.BlockSpec((1, band + A_PAD, tq), lambda h, i: (h, 0, 0))],
        out_specs=pl.BlockSpec((tstep, HEAD_DIM), lambda h, i: (i, h)),
        compiler_params=_params("parallel", "parallel"),
        name="mixer_a",
    )(qk, qk, vt, g_arr, bias_tile)


def _key_query_index(tk, tq, key_off):
    key = lax.broadcasted_iota(jnp.int32, (tk, tq), 0) + key_off
    query = lax.broadcasted_iota(jnp.int32, (tk, tq), 1)
    return key, query


def _softmax_step(s, masked_out, v_aug, m_sc, acc_sc):
    if masked_out is not None:
        s = jnp.where(masked_out, NEG_INF, s)
    m_old = m_sc[...]
    m_new = jnp.maximum(m_old, jnp.max(s, axis=0, keepdims=True))
    p = jnp.exp2(s - m_new).astype(BF16)
    alpha = jnp.exp2(m_old - m_new)
    acc_sc[...] = alpha * acc_sc[...] + jnp.dot(v_aug, p, preferred_element_type=F32)
    m_sc[...] = m_new


def _init_softmax_state(m_sc, acc_sc):
    m_sc[...] = jnp.full_like(m_sc, NEG_INF)
    acc_sc[...] = jnp.zeros_like(acc_sc)


def _normalised_rows(acc_sc):
    acc = acc_sc[...]
    return (acc[:HEAD_DIM] / acc[HEAD_DIM:HEAD_DIM + 1]).T


def _ones_rows(width):
    return jnp.ones((BF16_SUBLANES, width), BF16)


def _pipelined_sweep(head, count, rest_block, scores_into, consume, s_bufs,
                     accumulate=None, w_bufs=(None, None), count_is_even=False, steps_per_body=4):
    n_head = len(head)
    last_head = head[-1][0]
    defer_accumulate = accumulate is not None

    def step(par, block, mask, prev_block, next_block):
        if next_block is not None:
            scores_into(s_bufs[1 - par], next_block)
        consume(s_bufs[par], w_bufs[par], block, mask)
        if defer_accumulate and prev_block is not None:
            accumulate(w_bufs[1 - par], prev_block)

    def rest_prev(n):
        return jnp.where(n == 0, last_head, rest_block(n - 1))

    scores_into(s_bufs[0], head[0][0])
    for g, (block, mask) in enumerate(head):
        step(g % 2, block, mask, head[g - 1][0] if g else None,
             head[g + 1][0] if g + 1 < n_head else rest_block(0))

    def body_of(steps):
        def body(_, first):
            for u in range(steps):
                n = first + u
                step((n_head + u) % 2, rest_block(n), None,
                     rest_block(n - 1) if u else rest_prev(n), rest_block(n + 1))
            return first + steps
        return body

    done = 0
    steps = steps_per_body
    while steps >= 2:
        done = lax.fori_loop(0, (count - done) // steps, body_of(steps), done)
        steps //= 2
    last_block = rest_prev(count)
    if count_is_even:
        if defer_accumulate:
            accumulate(w_bufs[(n_head - 1) % 2], last_block)
        return

    @pl.when(count % 2 == 1)
    def _():
        n = count - 1
        step(n_head % 2, rest_block(n), None, rest_prev(n), None)
        if defer_accumulate:
            accumulate(w_bufs[n_head % 2], last_block)

    if defer_accumulate:
        @pl.when(count % 2 == 0)
        def _():
            accumulate(w_bufs[(n_head - 1) % 2], last_block)


def _mixer_b_kernel(q_ref, k_ref, vt_ref, g_ref, o_ref, acc_sc, run_sc, s0_sc, s1_sc,
                    w0_sc, c0_sc, w1_sc, c1_sc, *, tq, tk):
    per_q = tq // tk
    assert tq == per_q * tk and per_q % 2 == 0
    i = pl.program_id(1)
    q = q_ref[...]
    kk = lax.broadcasted_iota(jnp.int32, (tk + BF16_SUBLANES, tk), 0)
    jj = lax.broadcasted_iota(jnp.int32, (tk + BF16_SUBLANES, tk), 1)
    suffix = jnp.where((jj > kk) | (kk >= tk), 1.0, 0.0).astype(BF16)
    acc_sc[...] = jnp.zeros_like(acc_sc)
    run_sc[...] = jnp.zeros_like(run_sc)

    def scores_into(buf, j):
        off = pl.multiple_of(j * tk, tk)
        buf[...] = lax.dot_general(k_ref[pl.ds(off, tk), :], q, _NT, preferred_element_type=F32)

    def weights_from(buf, w_buf, j, key_off):
        w_ref, carried_ref = w_buf
        z = buf[...]
        neg_abs = lax.bitcast_convert_type(
            lax.bitcast_convert_type(z, jnp.uint32) | jnp.uint32(0x80000000), F32)
        log_beta = jnp.minimum(z, 0.0) - jnp.log(1.0 + jnp.exp2(neg_abs)) * LOG2E
        log_keep = log_beta - z
        if key_off is not None:
            key, query = _key_query_index(tk, tq, key_off)
            strict = key < query
            log_keep = jnp.where(strict, log_keep, 0.0)
        later = jnp.dot(suffix, log_keep.astype(BF16), preferred_element_type=F32)
        w = jnp.exp2(log_beta + later[:tk])
        if key_off is not None:
            w = jnp.where(strict, w, 0.0)
        w_ref[...] = w.astype(BF16)
        carried_ref[...] = jnp.exp2(run_sc[...])
        run_sc[...] += later[tk:tk + 1]

    def accumulate(w_buf, j):
        w_ref, carried_ref = w_buf
        off = pl.multiple_of(j * tk, tk)
        acc_sc[...] += carried_ref[...] * jnp.dot(vt_ref[:, pl.ds(off, tk)], w_ref[...],
                                                  preferred_element_type=F32)

    def below(n):
        return jnp.clip(per_q * i - 1 - n, 0, per_q * i)

    on_diagonal = [(per_q * i + d, d * tk) for d in reversed(range(per_q))]
    _pipelined_sweep(on_diagonal, per_q * i, below, scores_into, weights_from,
                     (s0_sc, s1_sc), accumulate, ((w0_sc, c0_sc), (w1_sc, c1_sc)), count_is_even=True,
                     steps_per_body=8)
    o_ref[...] = (acc_sc[...].T * _silu(g_ref[...])).astype(o_ref.dtype)


def _mixer_b(qk, q_col, k_col, vt, v_row, g_arr, g_col, tq, tk):
    s = qk.shape[0]
    kern = functools.partial(_mixer_b_kernel, tq=tq, tk=tk)
    return pl.pallas_call(
        kern,
        out_shape=jax.ShapeDtypeStruct((s, BRANCH), BF16),
        grid=(N_HEADS, s // tq),
        in_specs=[pl.BlockSpec((tq, HEAD_DIM), lambda h, i: (i, q_col + h)),
                  pl.BlockSpec((s, HEAD_DIM), lambda h, i: (0, k_col + h)),
                  pl.BlockSpec((HEAD_DIM, s), lambda h, i: (v_row + h, 0)),
                  pl.BlockSpec((tq, HEAD_DIM), lambda h, i: (i, g_col + h))],
        out_specs=pl.BlockSpec((tq, HEAD_DIM), lambda h, i: (i, h)),
        scratch_shapes=[pltpu.VMEM((HEAD_DIM, tq), F32), pltpu.VMEM((1, tq), F32),
                        pltpu.VMEM((tk, tq), F32), pltpu.VMEM((tk, tq), F32),
                        pltpu.VMEM((tk, tq), BF16), pltpu.VMEM((1, tq), F32),
                        pltpu.VMEM((tk, tq), BF16), pltpu.VMEM((1, tq), F32)],
        compiler_params=_params("parallel", "parallel"),
        name="mixer_b",
    )(qk, qk, vt, g_arr)


def _decay_features_kernel(u_ref, wf_ref, b_ref, place_ref, ones_ref, qx_ref, kx_ref, carry_sc):
    tm = u_ref.shape[0]

    @pl.when(pl.program_id(0) == 0)
    def _():
        carry_sc[...] = jnp.zeros_like(carry_sc)

    log_f = _log_sigmoid(jnp.dot(u_ref[...], wf_ref[...], preferred_element_type=F32) + b_ref[...])
    r = lax.broadcasted_iota(jnp.int32, (tm, tm), 0)
    c = lax.broadcasted_iota(jnp.int32, (tm, tm), 1)
    upto = jnp.where(c <= r, 1.0, 0.0).astype(F32)
    cum = jnp.dot(upto, log_f, preferred_element_type=F32,
                  precision=lax.Precision.HIGHEST) + carry_sc[...]
    carry_sc[...] = cum[tm - 1:tm, :]

    c2 = cum * LOG2E
    hi = c2.astype(BF16)
    r1 = c2 - hi.astype(F32)
    mid = r1.astype(BF16)
    lo = (r1 - mid.astype(F32)).astype(BF16)
    placed = jnp.dot(jnp.concatenate([hi, mid, lo], axis=1), place_ref[...],
                     preferred_element_type=F32) + ones_ref[...]
    qx_ref[...] = placed[:, :BRANCH].astype(BF16)
    kx_ref[...] = placed[:, BRANCH:].astype(BF16)


def _decay_placement():
    place = np.zeros((3 * HEAD_DIM, 2 * BRANCH), np.float32)
    ones = np.zeros((1, 2 * BRANCH), np.float32)
    for head in range(N_HEADS):
        for term in range(3):
            place[term * HEAD_DIM + head, head * HEAD_DIM + term] = 1.0
            place[term * HEAD_DIM + head, BRANCH + head * HEAD_DIM + 3 + term] = -1.0
            ones[0, head * HEAD_DIM + 3 + term] = 1.0
            ones[0, BRANCH + head * HEAD_DIM + term] = 1.0
    return jnp.asarray(place, BF16), jnp.asarray(ones, F32)


def _decay_features(u, wf, bias):
    s, d = u.shape
    tm = min(512, s)
    place, ones = _decay_placement()
    out = jax.ShapeDtypeStruct((s, BRANCH), BF16)
    spec = pl.BlockSpec((tm, BRANCH), lambda i: (i, 0))
    return pl.pallas_call(
        _decay_features_kernel,
        out_shape=(out, out),
        grid=(s // tm,),
        in_specs=[pl.BlockSpec((tm, d), lambda i: (i, 0)),
                  pl.BlockSpec((d, HEAD_DIM), lambda i: (0, 0)),
                  pl.BlockSpec((1, HEAD_DIM), lambda i: (0, 0)),
                  pl.BlockSpec(place.shape, lambda i: (0, 0)),
                  pl.BlockSpec(ones.shape, lambda i: (0, 0))],
        out_specs=(spec, spec),
        scratch_shapes=[pltpu.VMEM((1, HEAD_DIM), F32)],
        compiler_params=_params("arbitrary"),
        name="decay_features",
    )(u, wf, bias, place, ones)


def _causal_sweep(i, tq, tk, masked_out_fn, scores_into, consume, s_bufs):
    per_q = tq // tk
    assert tq == per_q * tk and per_q in (1, 2)
    head = [(per_q * i + d, masked_out_fn(*_key_query_index(tk, tq, d * tk))) for d in range(per_q)]
    _pipelined_sweep(head, per_q * i, lambda n: jnp.clip(n, 0, per_q * i), scores_into, consume, s_bufs,
                     count_is_even=per_q == 2, steps_per_body=8 // per_q)


def _mixer_c_kernel(q_ref, qx_ref, k_ref, kx_ref, vt_ref, g_ref, o_ref, m_sc, acc_sc, s0_sc, s1_sc,
                    *, tq, tk):
    q = jnp.concatenate([q_ref[...], qx_ref[...]], axis=1)
    ones = _ones_rows(tk)
    _init_softmax_state(m_sc, acc_sc)

    def scores_into(buf, j):
        off = pl.multiple_of(j * tk, tk)
        k = jnp.concatenate([k_ref[pl.ds(off, tk), :], kx_ref[pl.ds(off, tk), :]], axis=1)
        buf[...] = lax.dot_general(k, q, _NT, preferred_element_type=F32)

    def consume(buf, _, j, masked_out):
        off = pl.multiple_of(j * tk, tk)
        v_aug = jnp.concatenate([vt_ref[:, pl.ds(off, tk)], ones], axis=0)
        _softmax_step(buf[...], masked_out, v_aug, m_sc, acc_sc)

    _causal_sweep(pl.program_id(1), tq, tk, lambda key, query: key > query, scores_into, consume,
                  (s0_sc, s1_sc))
    o_ref[...] = (_normalised_rows(acc_sc) * _silu(g_ref[...])).astype(o_ref.dtype)


def _mixer_c(qk, q_col, k_col, qx, kx, vt, v_row, g_arr, g_col, tq, tk):
    s = qk.shape[0]
    return pl.pallas_call(
        functools.partial(_mixer_c_kernel, tq=tq, tk=tk),
        out_shape=jax.ShapeDtypeStruct((s, BRANCH), BF16),
        grid=(N_HEADS, s // tq),
        in_specs=[pl.BlockSpec((tq, HEAD_DIM), lambda h, i: (i, q_col + h)),
                  pl.BlockSpec((tq, HEAD_DIM), lambda h, i: (i, h)),
                  pl.BlockSpec((s, HEAD_DIM), lambda h, i: (0, k_col + h)),
                  pl.BlockSpec((s, HEAD_DIM), lambda h, i: (0, h)),
                  pl.BlockSpec((HEAD_DIM, s), lambda h, i: (v_row + h, 0)),
                  pl.BlockSpec((tq, HEAD_DIM), lambda h, i: (i, g_col + h))],
        out_specs=pl.BlockSpec((tq, HEAD_DIM), lambda h, i: (i, h)),
        scratch_shapes=[pltpu.VMEM((1, tq), F32),
                        pltpu.VMEM((HEAD_DIM + BF16_SUBLANES, tq), F32),
                        pltpu.VMEM((tk, tq), F32), pltpu.VMEM((tk, tq), F32)],
        compiler_params=_params("parallel", "parallel"),
        name="mixer_c",
    )(qk, qx, qk, kx, vt, g_arr)


def _rope_kernel(x_ref, pos_ref, invf_ref, place_ref, o_ref, *, groups):
    ang = invf_ref[...] * pos_ref[...].astype(F32)
    place = place_ref[...]
    cos = jnp.dot(place, jnp.cos(ang), preferred_element_type=F32, precision=lax.Precision.HIGHEST).T
    sin = jnp.dot(place, jnp.sin(ang), preferred_element_type=F32, precision=lax.Precision.HIGHEST).T
    lane = lax.broadcasted_iota(jnp.int32, cos.shape, 1)
    first_half = (lane & (D_QK - 1)) < ROPE_DIM // 2
    for gidx in range(groups):
        x = x_ref[:, gidx * 128:(gidx + 1) * 128]
        partner = jnp.where(first_half,
                            -pltpu.roll(x, 128 - ROPE_DIM // 2, 1),
                            pltpu.roll(x, ROPE_DIM // 2, 1))
        o_ref[:, gidx * 128:(gidx + 1) * 128] = (x * cos + partner * sin).astype(o_ref.dtype)


def _rope(x, positions):
    s, n = x.shape
    tm = min(512, s)
    n_freq = ROPE_DIM // 2
    inv_freq = ROPE_THETA ** (-jnp.arange(0, ROPE_DIM, 2, dtype=F32) / ROPE_DIM)
    invf = jnp.pad(inv_freq, (0, n_freq)).reshape(2 * n_freq, 1)
    slot = np.arange(128) % D_QK
    place = np.zeros((128, 2 * n_freq), np.float32)
    place[np.arange(128), np.where(slot < ROPE_DIM, slot % n_freq, n_freq)] = 1.0
    return pl.pallas_call(
        functools.partial(_rope_kernel, groups=n // 128),
        out_shape=jax.ShapeDtypeStruct((s, n), BF16),
        grid=(s // tm,),
        in_specs=[pl.BlockSpec((tm, n), lambda i: (i, 0)),
                  pl.BlockSpec((1, tm), lambda i: (0, i)),
                  pl.BlockSpec(invf.shape, lambda i: (0, 0)),
                  pl.BlockSpec(place.shape, lambda i: (0, 0))],
        out_specs=pl.BlockSpec((tm, n), lambda i: (i, 0)),
        compiler_params=_params("parallel"),
        name="rope",
    )(x, positions, invf, jnp.asarray(place))


def _mixer_d_kernel(q_ref, k_ref, vt_ref, g_ref, lam_ref, sub_ref, o_ref,
                    m1_sc, a1_sc, m2_sc, a2_sc, sa1_sc, sa2_sc, sb1_sc, sb2_sc, *, tq, tk, lambda_init):
    q = q_ref[...]
    lane = lax.broadcasted_iota(jnp.int32, q.shape, 1)
    zero = jnp.zeros_like(q)
    q1 = jnp.where(lane < D_QK, q, zero)
    q2 = jnp.where(lane >= D_QK, q, zero)
    ones = _ones_rows(tk)
    _init_softmax_state(m1_sc, a1_sc)
    _init_softmax_state(m2_sc, a2_sc)

    def scores_into(bufs, j):
        off = pl.multiple_of(j * tk, tk)
        k = k_ref[pl.ds(off, tk), :]
        for qm, buf in zip((q1, q2), bufs):
            buf[...] = lax.dot_general(k, qm, _NT, preferred_element_type=F32)

    def consume(bufs, _, j, masked_out):
        off = pl.multiple_of(j * tk, tk)
        v_aug = jnp.concatenate([vt_ref[:, pl.ds(off, tk)], ones], axis=0)
        for buf, m_sc, a_sc in zip(bufs, (m1_sc, m2_sc), (a1_sc, a2_sc)):
            _softmax_step(buf[...], masked_out, v_aug, m_sc, a_sc)

    _causal_sweep(pl.program_id(1), tq, tk,
                  lambda key, query: (key >> CHUNK_SHIFT) > (query >> CHUNK_SHIFT),
                  scores_into, consume, ((sa1_sc, sa2_sc), (sb1_sc, sb2_sc)))

    lv = lam_ref[...]
    lam = (jnp.exp(jnp.sum(lv[0:1] * lv[1:2], axis=1, keepdims=True))
           - jnp.exp(jnp.sum(lv[2:3] * lv[3:4], axis=1, keepdims=True)) + lambda_init)
    o = _normalised_rows(a1_sc) - lam * _normalised_rows(a2_sc)
    ms = jnp.mean(o * o, axis=-1, keepdims=True)
    y = o * lax.rsqrt(ms + EPS) * sub_ref[...] * (1.0 - lambda_init)
    o_ref[...] = (y * _silu(g_ref[...])).astype(o_ref.dtype)


def _mixer_d(qk_arr, q_col, k_col, vt, v_row, g_arr, g_col, lam_vecs, subln, lambda_init, tq, tk):
    s = qk_arr.shape[0]
    kern = functools.partial(_mixer_d_kernel, tq=tq, tk=tk, lambda_init=lambda_init)
    row_stat = pltpu.VMEM((1, tq), F32)
    acc = pltpu.VMEM((HEAD_DIM + BF16_SUBLANES, tq), F32)
    return pl.pallas_call(
        kern,
        out_shape=jax.ShapeDtypeStruct((s, BRANCH), BF16),
        grid=(N_HEADS, s // tq),
        in_specs=[pl.BlockSpec((tq, HEAD_DIM), lambda h, i: (i, q_col + h)),
                  pl.BlockSpec((s, HEAD_DIM), lambda h, i: (0, k_col + h)),
                  pl.BlockSpec((HEAD_DIM, s), lambda h, i: (v_row + h, 0)),
                  pl.BlockSpec((tq, HEAD_DIM), lambda h, i: (i, g_col + h)),
                  pl.BlockSpec((4, D_QK), lambda h, i: (0, 0)),
                  pl.BlockSpec((1, HEAD_DIM), lambda h, i: (0, 0))],
        out_specs=pl.BlockSpec((tq, HEAD_DIM), lambda h, i: (i, h)),
        scratch_shapes=[row_stat, acc, row_stat, acc] + [pltpu.VMEM((tk, tq), F32)] * 4,
        compiler_params=_params("parallel", "parallel"),
        name="mixer_d",
    )(qk_arr, qk_arr, vt, g_arr, lam_vecs, subln.reshape(1, HEAD_DIM))


def _out_proj_kernel(ya_ref, yb_ref, w_ref, h_ref, g_ref, o_ref):
    half = ya_ref.shape[1]
    y = (jnp.dot(ya_ref[...], w_ref[:half, :], preferred_element_type=F32)
         + jnp.dot(yb_ref[...], w_ref[half:, :], preferred_element_type=F32))
    ms = jnp.mean(y * y, axis=-1, keepdims=True)
    o_ref[...] = h_ref[...] + y * lax.rsqrt(ms + EPS) * g_ref[...]


def _out_proj(ya, yb, w, h, g):
    s, d = h.shape
    tm = min(512, s)
    return pl.pallas_call(
        _out_proj_kernel,
        out_shape=jax.ShapeDtypeStruct((s, d), F32),
        grid=(s // tm,),
        in_specs=[pl.BlockSpec((tm, BRANCH), lambda i: (i, 0)),
                  pl.BlockSpec((tm, BRANCH), lambda i: (i, 0)),
                  pl.BlockSpec((2 * BRANCH, d), lambda i: (0, 0)),
                  pl.BlockSpec((tm, d), lambda i: (i, 0)),
                  pl.BlockSpec((1, d), lambda i: (0, 0))],
        out_specs=pl.BlockSpec((tm, d), lambda i: (i, 0)),
        compiler_params=_params("parallel"),
        name="out_proj",
    )(ya, yb, w, h, g.reshape(1, d))


def _rel_bias_tile(rel_bias, tq):
    rows = 2 * A_PAD + tq
    period = rows + tq
    k = np.arange(period)
    c_minus_r = np.where(k < rows, k, k - period)
    line = rel_bias.astype(F32)[:, np.clip(A_PAD - c_minus_r, -REL_CLIP, REL_CLIP) + REL_CLIP] * LOG2E
    flat = jnp.tile(line, (1, tq))[:, :tq * (period - 1)]
    tile = flat.reshape(rel_bias.shape[0], tq, period - 1)[:, :, :rows]
    qc = np.arange(tq)[:, None] >> CHUNK_SHIFT
    kc = np.arange(rows)[None, :] >> CHUNK_SHIFT
    in_band = (kc >= qc) & (kc <= qc + A_LEFT_CHUNKS)
    return jnp.swapaxes(jnp.where(in_band, tile, NEG_INF), 1, 2)


def _even_layer(h, w_in, rel_bias, w_out, norm_pre, norm_post):
    s = h.shape[0]
    u = _rmsnorm(h, norm_pre)
    b = BRANCH
    log2_scale = HEAD_DIM ** -0.5 * LOG2E
    w_rows = jnp.concatenate([w_in[:, :b] * log2_scale, w_in[:, b:2 * b],
                              w_in[:, 4 * b:5 * b] * log2_scale, w_in[:, 5 * b:6 * b]], axis=1).astype(BF16)
    w_g = jnp.concatenate([w_in[:, 3 * b:4 * b], w_in[:, 7 * b:]], axis=1).astype(BF16)
    w_vt = jnp.concatenate([w_in[:, 2 * b:3 * b], w_in[:, 6 * b:7 * b]], axis=1).T.astype(BF16)
    rows = _matmul(u, w_rows, BF16)
    gates = _matmul(u, w_g, F32)
    vt = _matmul_t(w_vt, u, BF16)
    tq_a = min(256, s)
    y_a = _mixer_a(rows, 0, N_HEADS, vt, 0, gates, 0, _rel_bias_tile(rel_bias, tq_a), tq_a)
    y_b = _mixer_b(rows, 2 * N_HEADS, 3 * N_HEADS, vt, N_HEADS, gates, N_HEADS, min(512, s), min(256, s))
    return _out_proj(y_a, y_b, w_out.astype(BF16), h, norm_post)


def _odd_layer(h, positions, w_in, forget_bias, lq1, lk1, lq2, lk2, subln, w_out, norm_pre, norm_post,
               lambda_init):
    s, d = h.shape
    u = _rmsnorm(h, norm_pre)
    b = BRANCH
    f0 = 4 * b
    d0 = f0 + N_HEADS
    v0 = d0 + 4 * N_HEADS * D_QK
    w_rows = jnp.concatenate([w_in[:, :b] * (HEAD_DIM ** -0.5 * LOG2E), w_in[:, b:2 * b]],
                             axis=1).astype(BF16)
    w_g = jnp.concatenate([w_in[:, 3 * b:4 * b], w_in[:, v0 + b:]], axis=1).astype(BF16)
    w_vt = jnp.concatenate([w_in[:, 2 * b:3 * b], w_in[:, v0:v0 + b]], axis=1).T.astype(BF16)
    wd = w_in[:, d0:v0].reshape(d, 4, N_HEADS, D_QK)
    w_qk = jnp.concatenate([
        jnp.concatenate([wd[:, 0], wd[:, 1]], axis=2).reshape(d, b) * (D_QK ** -0.5 * LOG2E),
        jnp.concatenate([wd[:, 2], wd[:, 3]], axis=2).reshape(d, b),
    ], axis=1).astype(BF16)
    w_f = jnp.pad(w_in[:, f0:d0], ((0, 0), (0, HEAD_DIM - N_HEADS))).astype(BF16)
    b_f = jnp.pad(forget_bias.astype(F32), (0, HEAD_DIM - N_HEADS)).reshape(1, HEAD_DIM)

    rows = _matmul(u, w_rows, BF16)
    gates = _matmul(u, w_g, F32)
    vt = _matmul_t(w_vt, u, BF16)
    qk_d = _matmul(u, w_qk, F32)

    qx, kx = _decay_features(u, w_f, b_f)
    tq = tk = min(512, s)
    y_c = _mixer_c(rows, 0, N_HEADS, qx, kx, vt, 0, gates, 0, tq, tk)

    qk_rot = _rope(qk_d, positions)
    lam_vecs = jnp.stack([lq1, lk1, lq2, lk2]).astype(F32)
    y_d = _mixer_d(qk_rot, 0, N_HEADS, vt, N_HEADS, gates, N_HEADS, lam_vecs, subln, lambda_init, tq, tk)
    return _out_proj(y_c, y_d, w_out.astype(BF16), h, norm_post)


def kernel(x, positions, even_w_in, even_rel_bias, even_w_out, even_norm_pre, even_norm_post,
           odd_w_in, odd_forget_bias, odd_lambda_q1, odd_lambda_k1, odd_lambda_q2, odd_lambda_k2,
           odd_subln, odd_w_out, odd_norm_pre, odd_norm_post):
    assert x.shape[0] == 1
    h = x[0]
    depth = even_w_in.shape[0] + odd_w_in.shape[0]
    for layer in range(depth):
        i = layer // 2
        if layer % 2 == 0:
            h = _even_layer(h, even_w_in[i], even_rel_bias[i], even_w_out[i],
                            even_norm_pre[i], even_norm_post[i])
        else:
            lambda_init = 0.8 - 0.6 * math.exp(-0.3 * layer)
            h = _odd_layer(h, positions, odd_w_in[i], odd_forget_bias[i],
                           odd_lambda_q1[i], odd_lambda_k1[i], odd_lambda_q2[i], odd_lambda_k2[i],
                           odd_subln[i], odd_w_out[i], odd_norm_pre[i], odd_norm_post[i], lambda_init)
    return h[None]
```

```python
import functools
import math

import jax
import jax.numpy as jnp
import numpy as np
from jax import lax
from jax.experimental import pallas as pl
from jax.experimental.pallas import tpu as pltpu

F32 = jnp.float32
BF16 = jnp.bfloat16

HEAD_DIM = 128
N_HEADS = 8
BRANCH = N_HEADS * HEAD_DIM
CHUNK = 64
CHUNK_SHIFT = 6
A_LEFT_CHUNKS = 8
A_PAD = A_LEFT_CHUNKS * CHUNK
REL_CLIP = 128
D_QK = 64
ROPE_THETA = 500000.0
ROPE_DIM = 16
EPS = 1e-6
NEG_INF = float("-inf")
LOG2E = math.log2(math.e)
BF16_SUBLANES = 16
VMEM_LIMIT_BYTES = 48 * 1024 * 1024

_NT = (((1,), (1,)), ((), ()))


def _params(*semantics):
    return pltpu.CompilerParams(dimension_semantics=semantics, vmem_limit_bytes=VMEM_LIMIT_BYTES)


def _silu(g):
    return g * jax.nn.sigmoid(g)


def _log_sigmoid(z):
    return jnp.minimum(z, 0.0) - jnp.log1p(jnp.exp(-jnp.abs(z)))


def _rmsnorm_kernel(x_ref, g_ref, o_ref):
    x = x_ref[...]
    ms = jnp.mean(x * x, axis=-1, keepdims=True)
    o_ref[...] = (x * lax.rsqrt(ms + EPS) * g_ref[...]).astype(o_ref.dtype)


def _rmsnorm(x, g):
    s, d = x.shape
    tm = min(512, s)
    return pl.pallas_call(
        _rmsnorm_kernel,
        out_shape=jax.ShapeDtypeStruct((s, d), BF16),
        grid=(s // tm,),
        in_specs=[pl.BlockSpec((tm, d), lambda i: (i, 0)),
                  pl.BlockSpec((1, d), lambda i: (0, 0))],
        out_specs=pl.BlockSpec((tm, d), lambda i: (i, 0)),
        compiler_params=_params("parallel"),
        name="rmsnorm",
    )(x, g.reshape(1, d))


def _matmul_kernel(x_ref, w_ref, o_ref):
    o_ref[...] = jnp.dot(x_ref[...], w_ref[...], preferred_element_type=F32).astype(o_ref.dtype)


def _matmul(x, w, out_dtype):
    m, k = x.shape
    n = w.shape[1]
    tm = min(1024, m)
    tn = 1024 if n % 1024 == 0 else n
    return pl.pallas_call(
        _matmul_kernel,
        out_shape=jax.ShapeDtypeStruct((m, n), out_dtype),
        grid=(n // tn, m // tm),
        in_specs=[pl.BlockSpec((tm, k), lambda j, i: (i, 0)),
                  pl.BlockSpec((k, tn), lambda j, i: (0, j))],
        out_specs=pl.BlockSpec((tm, tn), lambda j, i: (i, j)),
        compiler_params=_params("parallel", "parallel"),
        name="in_proj",
    )(x, w)


def _matmul_t_kernel(w_ref, x_ref, o_ref):
    o_ref[...] = lax.dot_general(w_ref[...], x_ref[...], _NT,
                                 preferred_element_type=F32).astype(o_ref.dtype)


def _matmul_t(w_t, x, out_dtype):
    n, k = w_t.shape
    m = x.shape[0]
    tm = min(1024, m)
    tn = 1024 if n % 1024 == 0 else n
    return pl.pallas_call(
        _matmul_t_kernel,
        out_shape=jax.ShapeDtypeStruct((n, m), out_dtype),
        grid=(n // tn, m // tm),
        in_specs=[pl.BlockSpec((tn, k), lambda j, i: (j, 0)),
                  pl.BlockSpec((tm, k), lambda j, i: (i, 0))],
        out_specs=pl.BlockSpec((tn, tm), lambda j, i: (j, i)),
        compiler_params=_params("parallel", "parallel"),
        name="in_proj_t",
    )(w_t, x)


def _mixer_a_kernel(q_ref, k_ref, vt_ref, g_ref, b_ref, o_ref, *, tq, band):
    ones = _ones_rows(band)
    for sub in range(q_ref.shape[0] // tq):
        rows = slice(sub * tq, (sub + 1) * tq)
        first = (pl.program_id(1) * (q_ref.shape[0] // tq) + sub) * tq - A_PAD
        start = pl.multiple_of(jnp.maximum(first, 0), tq)
        cut = pl.multiple_of(start - first, tq)
        s = (lax.dot_general(k_ref[pl.ds(start, band), :], q_ref[rows, :], _NT, preferred_element_type=F32)
             + b_ref[0, pl.ds(cut, band), :])
        m = jnp.max(s, axis=0, keepdims=True)
        p = jnp.exp2(s - m).astype(BF16)
        v_aug = jnp.concatenate([vt_ref[:, pl.ds(start, band)], ones], axis=0)
        acc = jnp.dot(v_aug, p, preferred_element_type=F32)
        o = (acc[:HEAD_DIM] / acc[HEAD_DIM:HEAD_DIM + 1]).T
        o_ref[rows, :] = (o * _silu(g_ref[rows, :])).astype(o_ref.dtype)


def _mixer_a(qk, q_col, k_col, vt, v_row, g_arr, g_col, bias_tile, tq):
    s = qk.shape[0]
    assert A_PAD % tq == 0
    band = A_PAD + tq
    tstep = min(4 * tq, s)
    kern = functools.partial(_mixer_a_kernel, tq=tq, band=band)
    return pl.pallas_call(
        kern,
        out_shape=jax.ShapeDtypeStruct((s, BRANCH), BF16),
        grid=(N_HEADS, s // tstep),
        in_specs=[pl.BlockSpec((tstep, HEAD_DIM), lambda h, i: (i, q_col + h)),
                  pl.BlockSpec((s, HEAD_DIM), lambda h, i: (0, k_col + h)),
                  pl.BlockSpec((HEAD_DIM, s), lambda h, i: (v_row + h, 0)),
                  pl.BlockSpec((tstep, HEAD_DIM), lambda h, i: (i, g_col + h)),
                  pl.BlockSpec((1, band + A_PAD, tq), lambda h, i: (h, 0, 0))],
        out_specs=pl.BlockSpec((tstep, HEAD_DIM), lambda h, i: (i, h)),
        compiler_params=_params("parallel", "parallel"),
        name="mixer_a",
    )(qk, qk, vt, g_arr, bias_tile)


def _key_query_index(tk, tq, key_off):
    key = lax.broadcasted_iota(jnp.int32, (tk, tq), 0) + key_off
    query = lax.broadcasted_iota(jnp.int32, (tk, tq), 1)
    return key, query


def _softmax_step(s, masked_out, v_aug, m_sc, acc_sc):
    if masked_out is not None:
        s = jnp.where(masked_out, NEG_INF, s)
    m_old = m_sc[...]
    m_new = jnp.maximum(m_old, jnp.max(s, axis=0, keepdims=True))
    p = jnp.exp2(s - m_new).astype(BF16)
    alpha = jnp.exp2(m_old - m_new)
    acc_sc[...] = alpha * acc_sc[...] + jnp.dot(v_aug, p, preferred_element_type=F32)
    m_sc[...] = m_new


def _init_softmax_state(m_sc, acc_sc):
    m_sc[...] = jnp.full_like(m_sc, NEG_INF)
    acc_sc[...] = jnp.zeros_like(acc_sc)


def _normalised_rows(acc_sc):
    acc = acc_sc[...]
    return (acc[:HEAD_DIM] / acc[HEAD_DIM:HEAD_DIM + 1]).T


def _ones_rows(width):
    return jnp.ones((BF16_SUBLANES, width), BF16)


def _pipelined_sweep(head, count, rest_block, scores_into, consume, s_bufs,
                     accumulate=None, w_bufs=(None, None), count_is_even=False, steps_per_body=4):
    n_head = len(head)
    last_head = head[-1][0]
    defer_accumulate = accumulate is not None

    def step(par, block, mask, prev_block, next_block):
        if next_block is not None:
            scores_into(s_bufs[1 - par], next_block)
        consume(s_bufs[par], w_bufs[par], block, mask)
        if defer_accumulate and prev_block is not None:
            accumulate(w_bufs[1 - par], prev_block)

    def rest_prev(n):
        return jnp.where(n == 0, last_head, rest_block(n - 1))

    scores_into(s_bufs[0], head[0][0])
    for g, (block, mask) in enumerate(head):
        step(g % 2, block, mask, head[g - 1][0] if g else None,
             head[g + 1][0] if g + 1 < n_head else rest_block(0))

    def body_of(steps):
        def body(_, first):
            for u in range(steps):
                n = first + u
                step((n_head + u) % 2, rest_block(n), None,
                     rest_block(n - 1) if u else rest_prev(n), rest_block(n + 1))
            return first + steps
        return body

    done = 0
    steps = steps_per_body
    while steps >= 2:
        done = lax.fori_loop(0, (count - done) // steps, body_of(steps), done)
        steps //= 2
    last_block = rest_prev(count)
    if count_is_even:
        if defer_accumulate:
            accumulate(w_bufs[(n_head - 1) % 2], last_block)
        return

    @pl.when(count % 2 == 1)
    def _():
        n = count - 1
        step(n_head % 2, rest_block(n), None, rest_prev(n), None)
        if defer_accumulate:
            accumulate(w_bufs[n_head % 2], last_block)

    if defer_accumulate:
        @pl.when(count % 2 == 0)
        def _():
            accumulate(w_bufs[(n_head - 1) % 2], last_block)


def _mixer_b_kernel(q_ref, k_ref, vt_ref, g_ref, o_ref, acc_sc, run_sc, s0_sc, s1_sc,
                    w0_sc, c0_sc, w1_sc, c1_sc, *, tq, tk):
    per_q = tq // tk
    assert tq == per_q * tk and per_q % 2 == 0
    i = pl.program_id(1)
    q = q_ref[...]
    kk = lax.broadcasted_iota(jnp.int32, (tk + BF16_SUBLANES, tk), 0)
    jj = lax.broadcasted_iota(jnp.int32, (tk + BF16_SUBLANES, tk), 1)
    suffix = jnp.where((jj > kk) | (kk >= tk), 1.0, 0.0).astype(BF16)
    acc_sc[...] = jnp.zeros_like(acc_sc)
    run_sc[...] = jnp.zeros_like(run_sc)

    def scores_into(buf, j):
        off = pl.multiple_of(j * tk, tk)
        buf[...] = lax.dot_general(k_ref[pl.ds(off, tk), :], q, _NT, preferred_element_type=F32)

    def weights_from(buf, w_buf, j, key_off):
        w_ref, carried_ref = w_buf
        z = buf[...]
        neg_abs = lax.bitcast_convert_type(
            lax.bitcast_convert_type(z, jnp.uint32) | jnp.uint32(0x80000000), F32)
        log_beta = jnp.minimum(z, 0.0) - jnp.log(1.0 + jnp.exp2(neg_abs)) * LOG2E
        log_keep = log_beta - z
        if key_off is not None:
            key, query = _key_query_index(tk, tq, key_off)
            strict = key < query
            log_keep = jnp.where(strict, log_keep, 0.0)
        later = jnp.dot(suffix, log_keep.astype(BF16), preferred_element_type=F32)
        w = jnp.exp2(log_beta + later[:tk])
        if key_off is not None:
            w = jnp.where(strict, w, 0.0)
        w_ref[...] = w.astype(BF16)
        carried_ref[...] = jnp.exp2(run_sc[...])
        run_sc[...] += later[tk:tk + 1]

    def accumulate(w_buf, j):
        w_ref, carried_ref = w_buf
        off = pl.multiple_of(j * tk, tk)
        acc_sc[...] += carried_ref[...] * jnp.dot(vt_ref[:, pl.ds(off, tk)], w_ref[...],
                                                  preferred_element_type=F32)

    def below(n):
        return jnp.clip(per_q * i - 1 - n, 0, per_q * i)

    on_diagonal = [(per_q * i + d, d * tk) for d in reversed(range(per_q))]
    _pipelined_sweep(on_diagonal, per_q * i, below, scores_into, weights_from,
                     (s0_sc, s1_sc), accumulate, ((w0_sc, c0_sc), (w1_sc, c1_sc)), count_is_even=True,
                     steps_per_body=8)
    o_ref[...] = (acc_sc[...].T * _silu(g_ref[...])).astype(o_ref.dtype)


def _mixer_b(qk, q_col, k_col, vt, v_row, g_arr, g_col, tq, tk):
    s = qk.shape[0]
    kern = functools.partial(_mixer_b_kernel, tq=tq, tk=tk)
    return pl.pallas_call(
        kern,
        out_shape=jax.ShapeDtypeStruct((s, BRANCH), BF16),
        grid=(N_HEADS, s // tq),
        in_specs=[pl.BlockSpec((tq, HEAD_DIM), lambda h, i: (i, q_col + h)),
                  pl.BlockSpec((s, HEAD_DIM), lambda h, i: (0, k_col + h)),
                  pl.BlockSpec((HEAD_DIM, s), lambda h, i: (v_row + h, 0)),
                  pl.BlockSpec((tq, HEAD_DIM), lambda h, i: (i, g_col + h))],
        out_specs=pl.BlockSpec((tq, HEAD_DIM), lambda h, i: (i, h)),
        scratch_shapes=[pltpu.VMEM((HEAD_DIM, tq), F32), pltpu.VMEM((1, tq), F32),
                        pltpu.VMEM((tk, tq), F32), pltpu.VMEM((tk, tq), F32),
                        pltpu.VMEM((tk, tq), BF16), pltpu.VMEM((1, tq), F32),
                        pltpu.VMEM((tk, tq), BF16), pltpu.VMEM((1, tq), F32)],
        compiler_params=_params("parallel", "parallel"),
        name="mixer_b",
    )(qk, qk, vt, g_arr)


def _decay_features_kernel(u_ref, wf_ref, b_ref, place_ref, ones_ref, qx_ref, kx_ref, carry_sc):
    tm = u_ref.shape[0]

    @pl.when(pl.program_id(0) == 0)
    def _():
        carry_sc[...] = jnp.zeros_like(carry_sc)

    log_f = _log_sigmoid(jnp.dot(u_ref[...], wf_ref[...], preferred_element_type=F32) + b_ref[...])
    r = lax.broadcasted_iota(jnp.int32, (tm, tm), 0)
    c = lax.broadcasted_iota(jnp.int32, (tm, tm), 1)
    upto = jnp.where(c <= r, 1.0, 0.0).astype(F32)
    cum = jnp.dot(upto, log_f, preferred_element_type=F32,
                  precision=lax.Precision.HIGHEST) + carry_sc[...]
    carry_sc[...] = cum[tm - 1:tm, :]

    c2 = cum * LOG2E
    hi = c2.astype(BF16)
    r1 = c2 - hi.astype(F32)
    mid = r1.astype(BF16)
    lo = (r1 - mid.astype(F32)).astype(BF16)
    placed = jnp.dot(jnp.concatenate([hi, mid, lo], axis=1), place_ref[...],
                     preferred_element_type=F32) + ones_ref[...]
    qx_ref[...] = placed[:, :BRANCH].astype(BF16)
    kx_ref[...] = placed[:, BRANCH:].astype(BF16)


def _decay_placement():
    place = np.zeros((3 * HEAD_DIM, 2 * BRANCH), np.float32)
    ones = np.zeros((1, 2 * BRANCH), np.float32)
    for head in range(N_HEADS):
        for term in range(3):
            place[term * HEAD_DIM + head, head * HEAD_DIM + term] = 1.0
            place[term * HEAD_DIM + head, BRANCH + head * HEAD_DIM + 3 + term] = -1.0
            ones[0, head * HEAD_DIM + 3 + term] = 1.0
            ones[0, BRANCH + head * HEAD_DIM + term] = 1.0
    return jnp.asarray(place, BF16), jnp.asarray(ones, F32)


def _decay_features(u, wf, bias):
    s, d = u.shape
    tm = min(512, s)
    place, ones = _decay_placement()
    out = jax.ShapeDtypeStruct((s, BRANCH), BF16)
    spec = pl.BlockSpec((tm, BRANCH), lambda i: (i, 0))
    return pl.pallas_call(
        _decay_features_kernel,
        out_shape=(out, out),
        grid=(s // tm,),
        in_specs=[pl.BlockSpec((tm, d), lambda i: (i, 0)),
                  pl.BlockSpec((d, HEAD_DIM), lambda i: (0, 0)),
                  pl.BlockSpec((1, HEAD_DIM), lambda i: (0, 0)),
                  pl.BlockSpec(place.shape, lambda i: (0, 0)),
                  pl.BlockSpec(ones.shape, lambda i: (0, 0))],
        out_specs=(spec, spec),
        scratch_shapes=[pltpu.VMEM((1, HEAD_DIM), F32)],
        compiler_params=_params("arbitrary"),
        name="decay_features",
    )(u, wf, bias, place, ones)


def _causal_sweep(i, tq, tk, masked_out_fn, scores_into, consume, s_bufs):
    per_q = tq // tk
    assert tq == per_q * tk and per_q in (1, 2)
    head = [(per_q * i + d, masked_out_fn(*_key_query_index(tk, tq, d * tk))) for d in range(per_q)]
    _pipelined_sweep(head, per_q * i, lambda n: jnp.clip(n, 0, per_q * i), scores_into, consume, s_bufs,
                     count_is_even=per_q == 2, steps_per_body=16 // per_q)


def _mixer_c_kernel(q_ref, qx_ref, k_ref, kx_ref, vt_ref, g_ref, o_ref, m_sc, acc_sc, s0_sc, s1_sc,
                    *, tq, tk):
    q = jnp.concatenate([q_ref[...], qx_ref[...]], axis=1)
    ones = _ones_rows(tk)
    _init_softmax_state(m_sc, acc_sc)

    def scores_into(buf, j):
        off = pl.multiple_of(j * tk, tk)
        k = jnp.concatenate([k_ref[pl.ds(off, tk), :], kx_ref[pl.ds(off, tk), :]], axis=1)
        buf[...] = lax.dot_general(k, q, _NT, preferred_element_type=F32)

    def consume(buf, _, j, masked_out):
        off = pl.multiple_of(j * tk, tk)
        v_aug = jnp.concatenate([vt_ref[:, pl.ds(off, tk)], ones], axis=0)
        _softmax_step(buf[...], masked_out, v_aug, m_sc, acc_sc)

    _causal_sweep(pl.program_id(1), tq, tk, lambda key, query: key > query, scores_into, consume,
                  (s0_sc, s1_sc))
    o_ref[...] = (_normalised_rows(acc_sc) * _silu(g_ref[...])).astype(o_ref.dtype)


def _mixer_c(qk, q_col, k_col, qx, kx, vt, v_row, g_arr, g_col, tq, tk):
    s = qk.shape[0]
    return pl.pallas_call(
        functools.partial(_mixer_c_kernel, tq=tq, tk=tk),
        out_shape=jax.ShapeDtypeStruct((s, BRANCH), BF16),
        grid=(N_HEADS, s // tq),
        in_specs=[pl.BlockSpec((tq, HEAD_DIM), lambda h, i: (i, q_col + h)),
                  pl.BlockSpec((tq, HEAD_DIM), lambda h, i: (i, h)),
                  pl.BlockSpec((s, HEAD_DIM), lambda h, i: (0, k_col + h)),
                  pl.BlockSpec((s, HEAD_DIM), lambda h, i: (0, h)),
                  pl.BlockSpec((HEAD_DIM, s), lambda h, i: (v_row + h, 0)),
                  pl.BlockSpec((tq, HEAD_DIM), lambda h, i: (i, g_col + h))],
        out_specs=pl.BlockSpec((tq, HEAD_DIM), lambda h, i: (i, h)),
        scratch_shapes=[pltpu.VMEM((1, tq), F32),
                        pltpu.VMEM((HEAD_DIM + BF16_SUBLANES, tq), F32),
                        pltpu.VMEM((tk, tq), F32), pltpu.VMEM((tk, tq), F32)],
        compiler_params=_params("parallel", "parallel"),
        name="mixer_c",
    )(qk, qx, qk, kx, vt, g_arr)


def _rope_kernel(x_ref, pos_ref, invf_ref, place_ref, o_ref, *, groups):
    ang = invf_ref[...] * pos_ref[...].astype(F32)
    place = place_ref[...]
    cos = jnp.dot(place, jnp.cos(ang), preferred_element_type=F32, precision=lax.Precision.HIGHEST).T
    sin = jnp.dot(place, jnp.sin(ang), preferred_element_type=F32, precision=lax.Precision.HIGHEST).T
    lane = lax.broadcasted_iota(jnp.int32, cos.shape, 1)
    first_half = (lane & (D_QK - 1)) < ROPE_DIM // 2
    for gidx in range(groups):
        x = x_ref[:, gidx * 128:(gidx + 1) * 128]
        partner = jnp.where(first_half,
                            -pltpu.roll(x, 128 - ROPE_DIM // 2, 1),
                            pltpu.roll(x, ROPE_DIM // 2, 1))
        o_ref[:, gidx * 128:(gidx + 1) * 128] = (x * cos + partner * sin).astype(o_ref.dtype)


def _rope(x, positions):
    s, n = x.shape
    tm = min(512, s)
    n_freq = ROPE_DIM // 2
    inv_freq = ROPE_THETA ** (-jnp.arange(0, ROPE_DIM, 2, dtype=F32) / ROPE_DIM)
    invf = jnp.pad(inv_freq, (0, n_freq)).reshape(2 * n_freq, 1)
    slot = np.arange(128) % D_QK
    place = np.zeros((128, 2 * n_freq), np.float32)
    place[np.arange(128), np.where(slot < ROPE_DIM, slot % n_freq, n_freq)] = 1.0
    return pl.pallas_call(
        functools.partial(_rope_kernel, groups=n // 128),
        out_shape=jax.ShapeDtypeStruct((s, n), BF16),
        grid=(s // tm,),
        in_specs=[pl.BlockSpec((tm, n), lambda i: (i, 0)),
                  pl.BlockSpec((1, tm), lambda i: (0, i)),
                  pl.BlockSpec(invf.shape, lambda i: (0, 0)),
                  pl.BlockSpec(place.shape, lambda i: (0, 0))],
        out_specs=pl.BlockSpec((tm, n), lambda i: (i, 0)),
        compiler_params=_params("parallel"),
        name="rope",
    )(x, positions, invf, jnp.asarray(place))


def _mixer_d_kernel(q_ref, k_ref, vt_ref, g_ref, lam_ref, sub_ref, o_ref,
                    m1_sc, a1_sc, m2_sc, a2_sc, sa1_sc, sa2_sc, sb1_sc, sb2_sc, *, tq, tk, lambda_init):
    q = q_ref[...]
    lane = lax.broadcasted_iota(jnp.int32, q.shape, 1)
    zero = jnp.zeros_like(q)
    q1 = jnp.where(lane < D_QK, q, zero)
    q2 = jnp.where(lane >= D_QK, q, zero)
    ones = _ones_rows(tk)
    _init_softmax_state(m1_sc, a1_sc)
    _init_softmax_state(m2_sc, a2_sc)

    def scores_into(bufs, j):
        off = pl.multiple_of(j * tk, tk)
        k = k_ref[pl.ds(off, tk), :]
        for qm, buf in zip((q1, q2), bufs):
            buf[...] = lax.dot_general(k, qm, _NT, preferred_element_type=F32)

    def consume(bufs, _, j, masked_out):
        off = pl.multiple_of(j * tk, tk)
        v_aug = jnp.concatenate([vt_ref[:, pl.ds(off, tk)], ones], axis=0)
        for buf, m_sc, a_sc in zip(bufs, (m1_sc, m2_sc), (a1_sc, a2_sc)):
            _softmax_step(buf[...], masked_out, v_aug, m_sc, a_sc)

    _causal_sweep(pl.program_id(1), tq, tk,
                  lambda key, query: (key >> CHUNK_SHIFT) > (query >> CHUNK_SHIFT),
                  scores_into, consume, ((sa1_sc, sa2_sc), (sb1_sc, sb2_sc)))

    lv = lam_ref[...]
    lam = (jnp.exp(jnp.sum(lv[0:1] * lv[1:2], axis=1, keepdims=True))
           - jnp.exp(jnp.sum(lv[2:3] * lv[3:4], axis=1, keepdims=True)) + lambda_init)
    o = _normalised_rows(a1_sc) - lam * _normalised_rows(a2_sc)
    ms = jnp.mean(o * o, axis=-1, keepdims=True)
    y = o * lax.rsqrt(ms + EPS) * sub_ref[...] * (1.0 - lambda_init)
    o_ref[...] = (y * _silu(g_ref[...])).astype(o_ref.dtype)


def _mixer_d(qk_arr, q_col, k_col, vt, v_row, g_arr, g_col, lam_vecs, subln, lambda_init, tq, tk):
    s = qk_arr.shape[0]
    kern = functools.partial(_mixer_d_kernel, tq=tq, tk=tk, lambda_init=lambda_init)
    row_stat = pltpu.VMEM((1, tq), F32)
    acc = pltpu.VMEM((HEAD_DIM + BF16_SUBLANES, tq), F32)
    return pl.pallas_call(
        kern,
        out_shape=jax.ShapeDtypeStruct((s, BRANCH), BF16),
        grid=(N_HEADS, s // tq),
        in_specs=[pl.BlockSpec((tq, HEAD_DIM), lambda h, i: (i, q_col + h)),
                  pl.BlockSpec((s, HEAD_DIM), lambda h, i: (0, k_col + h)),
                  pl.BlockSpec((HEAD_DIM, s), lambda h, i: (v_row + h, 0)),
                  pl.BlockSpec((tq, HEAD_DIM), lambda h, i: (i, g_col + h)),
                  pl.BlockSpec((4, D_QK), lambda h, i: (0, 0)),
                  pl.BlockSpec((1, HEAD_DIM), lambda h, i: (0, 0))],
        out_specs=pl.BlockSpec((tq, HEAD_DIM), lambda h, i: (i, h)),
        scratch_shapes=[row_stat, acc, row_stat, acc] + [pltpu.VMEM((tk, tq), F32)] * 4,
        compiler_params=_params("parallel", "parallel"),
        name="mixer_d",
    )(qk_arr, qk_arr, vt, g_arr, lam_vecs, subln.reshape(1, HEAD_DIM))


def _out_proj_kernel(ya_ref, yb_ref, w_ref, h_ref, g_ref, o_ref):
    half = ya_ref.shape[1]
    y = (jnp.dot(ya_ref[...], w_ref[:half, :], preferred_element_type=F32)
         + jnp.dot(yb_ref[...], w_ref[half:, :], preferred_element_type=F32))
    ms = jnp.mean(y * y, axis=-1, keepdims=True)
    o_ref[...] = h_ref[...] + y * lax.rsqrt(ms + EPS) * g_ref[...]


def _out_proj(ya, yb, w, h, g):
    s, d = h.shape
    tm = min(512, s)
    return pl.pallas_call(
        _out_proj_kernel,
        out_shape=jax.ShapeDtypeStruct((s, d), F32),
        grid=(s // tm,),
        in_specs=[pl.BlockSpec((tm, BRANCH), lambda i: (i, 0)),
                  pl.BlockSpec((tm, BRANCH), lambda i: (i, 0)),
                  pl.BlockSpec((2 * BRANCH, d), lambda i: (0, 0)),
                  pl.BlockSpec((tm, d), lambda i: (i, 0)),
                  pl.BlockSpec((1, d), lambda i: (0, 0))],
        out_specs=pl.BlockSpec((tm, d), lambda i: (i, 0)),
        compiler_params=_params("parallel"),
        name="out_proj",
    )(ya, yb, w, h, g.reshape(1, d))


def _rel_bias_tile(rel_bias, tq):
    rows = 2 * A_PAD + tq
    period = rows + tq
    k = np.arange(period)
    c_minus_r = np.where(k < rows, k, k - period)
    line = rel_bias.astype(F32)[:, np.clip(A_PAD - c_minus_r, -REL_CLIP, REL_CLIP) + REL_CLIP] * LOG2E
    flat = jnp.tile(line, (1, tq))[:, :tq * (period - 1)]
    tile = flat.reshape(rel_bias.shape[0], tq, period - 1)[:, :, :rows]
    qc = np.arange(tq)[:, None] >> CHUNK_SHIFT
    kc = np.arange(rows)[None, :] >> CHUNK_SHIFT
    in_band = (kc >= qc) & (kc <= qc + A_LEFT_CHUNKS)
    return jnp.swapaxes(jnp.where(in_band, tile, NEG_INF), 1, 2)


def _even_layer(h, w_in, rel_bias, w_out, norm_pre, norm_post):
    s = h.shape[0]
    u = _rmsnorm(h, norm_pre)
    b = BRANCH
    log2_scale = HEAD_DIM ** -0.5 * LOG2E
    w_rows = jnp.concatenate([w_in[:, :b] * log2_scale, w_in[:, b:2 * b],
                              w_in[:, 4 * b:5 * b] * log2_scale, w_in[:, 5 * b:6 * b]], axis=1).astype(BF16)
    w_g = jnp.concatenate([w_in[:, 3 * b:4 * b], w_in[:, 7 * b:]], axis=1).astype(BF16)
    w_vt = jnp.concatenate([w_in[:, 2 * b:3 * b], w_in[:, 6 * b:7 * b]], axis=1).T.astype(BF16)
    rows = _matmul(u, w_rows, BF16)
    gates = _matmul(u, w_g, F32)
    vt = _matmul_t(w_vt, u, BF16)
    tq_a = min(256, s)
    y_a = _mixer_a(rows, 0, N_HEADS, vt, 0, gates, 0, _rel_bias_tile(rel_bias, tq_a), tq_a)
    y_b = _mixer_b(rows, 2 * N_HEADS, 3 * N_HEADS, vt, N_HEADS, gates, N_HEADS, min(512, s), min(256, s))
    return _out_proj(y_a, y_b, w_out.astype(BF16), h, norm_post)


def _odd_layer(h, positions, w_in, forget_bias, lq1, lk1, lq2, lk2, subln, w_out, norm_pre, norm_post,
               lambda_init):
    s, d = h.shape
    u = _rmsnorm(h, norm_pre)
    b = BRANCH
    f0 = 4 * b
    d0 = f0 + N_HEADS
    v0 = d0 + 4 * N_HEADS * D_QK
    w_rows = jnp.concatenate([w_in[:, :b] * (HEAD_DIM ** -0.5 * LOG2E), w_in[:, b:2 * b]],
                             axis=1).astype(BF16)
    w_g = jnp.concatenate([w_in[:, 3 * b:4 * b], w_in[:, v0 + b:]], axis=1).astype(BF16)
    w_vt = jnp.concatenate([w_in[:, 2 * b:3 * b], w_in[:, v0:v0 + b]], axis=1).T.astype(BF16)
    wd = w_in[:, d0:v0].reshape(d, 4, N_HEADS, D_QK)
    w_qk = jnp.concatenate([
        jnp.concatenate([wd[:, 0], wd[:, 1]], axis=2).reshape(d, b) * (D_QK ** -0.5 * LOG2E),
        jnp.concatenate([wd[:, 2], wd[:, 3]], axis=2).reshape(d, b),
    ], axis=1).astype(BF16)
    w_f = jnp.pad(w_in[:, f0:d0], ((0, 0), (0, HEAD_DIM - N_HEADS))).astype(BF16)
    b_f = jnp.pad(forget_bias.astype(F32), (0, HEAD_DIM - N_HEADS)).reshape(1, HEAD_DIM)

    rows = _matmul(u, w_rows, BF16)
    gates = _matmul(u, w_g, F32)
    vt = _matmul_t(w_vt, u, BF16)
    qk_d = _matmul(u, w_qk, F32)

    qx, kx = _decay_features(u, w_f, b_f)
    tq = tk = min(512, s)
    y_c = _mixer_c(rows, 0, N_HEADS, qx, kx, vt, 0, gates, 0, tq, tk)

    qk_rot = _rope(qk_d, positions)
    lam_vecs = jnp.stack([lq1, lk1, lq2, lk2]).astype(F32)
    y_d = _mixer_d(qk_rot, 0, N_HEADS, vt, N_HEADS, gates, N_HEADS, lam_vecs, subln, lambda_init, tq, tk)
    return _out_proj(y_c, y_d, w_out.astype(BF16), h, norm_post)


def kernel(x, positions, even_w_in, even_rel_bias, even_w_out, even_norm_pre, even_norm_post,
           odd_w_in, odd_forget_bias, odd_lambda_q1, odd_lambda_k1, odd_lambda_q2, odd_lambda_k2,
           odd_subln, odd_w_out, odd_norm_pre, odd_norm_post):
    assert x.shape[0] == 1
    h = x[0]
    depth = even_w_in.shape[0] + odd_w_in.shape[0]
    for layer in range(depth):
        i = layer // 2
        if layer % 2 == 0:
            h = _even_layer(h, even_w_in[i], even_rel_bias[i], even_w_out[i],
                            even_norm_pre[i], even_norm_post[i])
        else:
            lambda_init = 0.8 - 0.6 * math.exp(-0.3 * layer)
            h = _odd_layer(h, positions, odd_w_in[i], odd_forget_bias[i],
                           odd_lambda_q1[i], odd_lambda_k1[i], odd_lambda_q2[i], odd_lambda_k2[i],
                           odd_subln[i], odd_w_out[i], odd_norm_pre[i], odd_norm_post[i], lambda_init)
    return h[None]
```

```python
import functools
import math

import jax
import jax.numpy as jnp
import numpy as np
from jax import lax
from jax.experimental import pallas as pl
from jax.experimental.pallas import tpu as pltpu

F32 = jnp.float32
BF16 = jnp.bfloat16

HEAD_DIM = 128
N_HEADS = 8
BRANCH = N_HEADS * HEAD_DIM
CHUNK = 64
CHUNK_SHIFT = 6
A_LEFT_CHUNKS = 8
A_PAD = A_LEFT_CHUNKS * CHUNK
REL_CLIP = 128
D_QK = 64
ROPE_THETA = 500000.0
ROPE_DIM = 16
EPS = 1e-6
NEG_INF = float("-inf")
LOG2E = math.log2(math.e)
BF16_SUBLANES = 16
VMEM_LIMIT_BYTES = 48 * 1024 * 1024

_NT = (((1,), (1,)), ((), ()))


def _params(*semantics):
    return pltpu.CompilerParams(dimension_semantics=semantics, vmem_limit_bytes=VMEM_LIMIT_BYTES)


def _silu(g):
    return g * jax.nn.sigmoid(g)


def _log_sigmoid(z):
    return jnp.minimum(z, 0.0) - jnp.log1p(jnp.exp(-jnp.abs(z)))


def _rmsnorm_kernel(x_ref, g_ref, o_ref):
    x = x_ref[...]
    ms = jnp.mean(x * x, axis=-1, keepdims=True)
    o_ref[...] = (x * lax.rsqrt(ms + EPS) * g_ref[...]).astype(o_ref.dtype)


def _rmsnorm(x, g):
    s, d = x.shape
    tm = min(512, s)
    return pl.pallas_call(
        _rmsnorm_kernel,
        out_shape=jax.ShapeDtypeStruct((s, d), BF16),
        grid=(s // tm,),
        in_specs=[pl.BlockSpec((tm, d), lambda i: (i, 0)),
                  pl.BlockSpec((1, d), lambda i: (0, 0))],
        out_specs=pl.BlockSpec((tm, d), lambda i: (i, 0)),
        compiler_params=_params("parallel"),
        name="rmsnorm",
    )(x, g.reshape(1, d))


def _matmul_kernel(x_ref, w_ref, o_ref):
    o_ref[...] = jnp.dot(x_ref[...], w_ref[...], preferred_element_type=F32).astype(o_ref.dtype)


def _matmul(x, w, out_dtype):
    m, k = x.shape
    n = w.shape[1]
    tm = min(1024, m)
    tn = 2048 if n % 2048 == 0 else n
    return pl.pallas_call(
        _matmul_kernel,
        out_shape=jax.ShapeDtypeStruct((m, n), out_dtype),
        grid=(n // tn, m // tm),
        in_specs=[pl.BlockSpec((tm, k), lambda j, i: (i, 0)),
                  pl.BlockSpec((k, tn), lambda j, i: (0, j))],
        out_specs=pl.BlockSpec((tm, tn), lambda j, i: (i, j)),
        compiler_params=_params("parallel", "parallel"),
        name="in_proj",
    )(x, w)


def _matmul_t_kernel(w_ref, x_ref, o_ref):
    o_ref[...] = lax.dot_general(w_ref[...], x_ref[...], _NT,
                                 preferred_element_type=F32).astype(o_ref.dtype)


def _matmul_t(w_t, x, out_dtype):
    n, k = w_t.shape
    m = x.shape[0]
    tm = min(1024, m)
    tn = 1024 if n % 1024 == 0 else n
    return pl.pallas_call(
        _matmul_t_kernel,
        out_shape=jax.ShapeDtypeStruct((n, m), out_dtype),
        grid=(n // tn, m // tm),
        in_specs=[pl.BlockSpec((tn, k), lambda j, i: (j, 0)),
                  pl.BlockSpec((tm, k), lambda j, i: (i, 0))],
        out_specs=pl.BlockSpec((tn, tm), lambda j, i: (j, i)),
        compiler_params=_params("parallel", "parallel"),
        name="in_proj_t",
    )(w_t, x)


def _mixer_a_kernel(q_ref, k_ref, vt_ref, g_ref, b_ref, o_ref, *, tq, band):
    ones = _ones_rows(band)
    for sub in range(q_ref.shape[0] // tq):
        rows = slice(sub * tq, (sub + 1) * tq)
        first = (pl.program_id(1) * (q_ref.shape[0] // tq) + sub) * tq - A_PAD
        start = pl.multiple_of(jnp.maximum(first, 0), tq)
        cut = pl.multiple_of(start - first, tq)
        s = (lax.dot_general(k_ref[pl.ds(start, band), :], q_ref[rows, :], _NT, preferred_element_type=F32)
             + b_ref[0, pl.ds(cut, band), :])
        m = jnp.max(s, axis=0, keepdims=True)
        p = jnp.exp2(s - m).astype(BF16)
        v_aug = jnp.concatenate([vt_ref[:, pl.ds(start, band)], ones], axis=0)
        acc = jnp.dot(v_aug, p, preferred_element_type=F32)
        o = (acc[:HEAD_DIM] / acc[HEAD_DIM:HEAD_DIM + 1]).T
        o_ref[rows, :] = (o * _silu(g_ref[rows, :])).astype(o_ref.dtype)


def _mixer_a(qk, q_col, k_col, vt, v_row, g_arr, g_col, bias_tile, tq):
    s = qk.shape[0]
    assert A_PAD % tq == 0
    band = A_PAD + tq
    tstep = min(4 * tq, s)
    kern = functools.partial(_mixer_a_kernel, tq=tq, band=band)
    return pl.pallas_call(
        kern,
        out_shape=jax.ShapeDtypeStruct((s, BRANCH), BF16),
        grid=(N_HEADS, s // tstep),
        in_specs=[pl.BlockSpec((tstep, HEAD_DIM), lambda h, i: (i, q_col + h)),
                  pl.BlockSpec((s, HEAD_DIM), lambda h, i: (0, k_col + h)),
                  pl.BlockSpec((HEAD_DIM, s), lambda h, i: (v_row + h, 0)),
                  pl.BlockSpec((tstep, HEAD_DIM), lambda h, i: (i, g_col + h)),
                  pl.BlockSpec((1, band + A_PAD, tq), lambda h, i: (h, 0, 0))],
        out_specs=pl.BlockSpec((tstep, HEAD_DIM), lambda h, i: (i, h)),
        compiler_params=_params("parallel", "parallel"),
        name="mixer_a",
    )(qk, qk, vt, g_arr, bias_tile)


def _key_query_index(tk, tq, key_off):
    key = lax.broadcasted_iota(jnp.int32, (tk, tq), 0) + key_off
    query = lax.broadcasted_iota(jnp.int32, (tk, tq), 1)
    return key, query


def _softmax_step(s, masked_out, v_aug, m_sc, acc_sc):
    if masked_out is not None:
        s = jnp.where(masked_out, NEG_INF, s)
    m_old = m_sc[...]
    m_new = jnp.maximum(m_old, jnp.max(s, axis=0, keepdims=True))
    p = jnp.exp2(s - m_new).astype(BF16)
    alpha = jnp.exp2(m_old - m_new)
    acc_sc[...] = alpha * acc_sc[...] + jnp.dot(v_aug, p, preferred_element_type=F32)
    m_sc[...] = m_new


def _init_softmax_state(m_sc, acc_sc):
    m_sc[...] = jnp.full_like(m_sc, NEG_INF)
    acc_sc[...] = jnp.zeros_like(acc_sc)


def _normalised_rows(acc_sc):
    acc = acc_sc[...]
    return (acc[:HEAD_DIM] / acc[HEAD_DIM:HEAD_DIM + 1]).T


def _ones_rows(width):
    return jnp.ones((BF16_SUBLANES, width), BF16)


def _pipelined_sweep(head, count, rest_block, scores_into, consume, s_bufs,
                     accumulate=None, w_bufs=(None, None), count_is_even=False, steps_per_body=4):
    n_head = len(head)
    last_head = head[-1][0]
    defer_accumulate = accumulate is not None

    def step(par, block, mask, prev_block, next_block):
        if next_block is not None:
            scores_into(s_bufs[1 - par], next_block)
        consume(s_bufs[par], w_bufs[par], block, mask)
        if defer_accumulate and prev_block is not None:
            accumulate(w_bufs[1 - par], prev_block)

    def rest_prev(n):
        return jnp.where(n == 0, last_head, rest_block(n - 1))

    scores_into(s_bufs[0], head[0][0])
    for g, (block, mask) in enumerate(head):
        step(g % 2, block, mask, head[g - 1][0] if g else None,
             head[g + 1][0] if g + 1 < n_head else rest_block(0))

    def body_of(steps):
        def body(_, first):
            for u in range(steps):
                n = first + u
                step((n_head + u) % 2, rest_block(n), None,
                     rest_block(n - 1) if u else rest_prev(n), rest_block(n + 1))
            return first + steps
        return body

    done = 0
    steps = steps_per_body
    while steps >= 2:
        done = lax.fori_loop(0, (count - done) // steps, body_of(steps), done)
        steps //= 2
    last_block = rest_prev(count)
    if count_is_even:
        if defer_accumulate:
            accumulate(w_bufs[(n_head - 1) % 2], last_block)
        return

    @pl.when(count % 2 == 1)
    def _():
        n = count - 1
        step(n_head % 2, rest_block(n), None, rest_prev(n), None)
        if defer_accumulate:
            accumulate(w_bufs[n_head % 2], last_block)

    if defer_accumulate:
        @pl.when(count % 2 == 0)
        def _():
            accumulate(w_bufs[(n_head - 1) % 2], last_block)


def _mixer_b_kernel(q_ref, k_ref, vt_ref, g_ref, o_ref, acc_sc, run_sc, s0_sc, s1_sc,
                    w0_sc, c0_sc, w1_sc, c1_sc, *, tq, tk):
    per_q = tq // tk
    assert tq == per_q * tk and per_q % 2 == 0
    i = pl.program_id(1)
    q = q_ref[...]
    kk = lax.broadcasted_iota(jnp.int32, (tk + BF16_SUBLANES, tk), 0)
    jj = lax.broadcasted_iota(jnp.int32, (tk + BF16_SUBLANES, tk), 1)
    suffix = jnp.where((jj > kk) | (kk >= tk), 1.0, 0.0).astype(BF16)
    acc_sc[...] = jnp.zeros_like(acc_sc)
    run_sc[...] = jnp.zeros_like(run_sc)

    def scores_into(buf, j):
        off = pl.multiple_of(j * tk, tk)
        buf[...] = lax.dot_general(k_ref[pl.ds(off, tk), :], q, _NT, preferred_element_type=F32)

    def weights_from(buf, w_buf, j, key_off):
        w_ref, carried_ref = w_buf
        z = buf[...]
        neg_abs = lax.bitcast_convert_type(
            lax.bitcast_convert_type(z, jnp.uint32) | jnp.uint32(0x80000000), F32)
        log_beta = jnp.minimum(z, 0.0) - jnp.log(1.0 + jnp.exp2(neg_abs)) * LOG2E
        log_keep = log_beta - z
        if key_off is not None:
            key, query = _key_query_index(tk, tq, key_off)
            strict = key < query
            log_keep = jnp.where(strict, log_keep, 0.0)
        later = jnp.dot(suffix, log_keep.astype(BF16), preferred_element_type=F32)
        w = jnp.exp2(log_beta + later[:tk])
        if key_off is not None:
            w = jnp.where(strict, w, 0.0)
        w_ref[...] = w.astype(BF16)
        carried_ref[...] = jnp.exp2(run_sc[...])
        run_sc[...] += later[tk:tk + 1]

    def accumulate(w_buf, j):
        w_ref, carried_ref = w_buf
        off = pl.multiple_of(j * tk, tk)
        acc_sc[...] += carried_ref[...] * jnp.dot(vt_ref[:, pl.ds(off, tk)], w_ref[...],
                                                  preferred_element_type=F32)

    def below(n):
        return jnp.clip(per_q * i - 1 - n, 0, per_q * i)

    on_diagonal = [(per_q * i + d, d * tk) for d in reversed(range(per_q))]
    _pipelined_sweep(on_diagonal, per_q * i, below, scores_into, weights_from,
                     (s0_sc, s1_sc), accumulate, ((w0_sc, c0_sc), (w1_sc, c1_sc)), count_is_even=True,
                     steps_per_body=8)
    o_ref[...] = (acc_sc[...].T * _silu(g_ref[...])).astype(o_ref.dtype)


def _mixer_b(qk, q_col, k_col, vt, v_row, g_arr, g_col, tq, tk):
    s = qk.shape[0]
    kern = functools.partial(_mixer_b_kernel, tq=tq, tk=tk)
    return pl.pallas_call(
        kern,
        out_shape=jax.ShapeDtypeStruct((s, BRANCH), BF16),
        grid=(N_HEADS, s // tq),
        in_specs=[pl.BlockSpec((tq, HEAD_DIM), lambda h, i: (i, q_col + h)),
                  pl.BlockSpec((s, HEAD_DIM), lambda h, i: (0, k_col + h)),
                  pl.BlockSpec((HEAD_DIM, s), lambda h, i: (v_row + h, 0)),
                  pl.BlockSpec((tq, HEAD_DIM), lambda h, i: (i, g_col + h))],
        out_specs=pl.BlockSpec((tq, HEAD_DIM), lambda h, i: (i, h)),
        scratch_shapes=[pltpu.VMEM((HEAD_DIM, tq), F32), pltpu.VMEM((1, tq), F32),
                        pltpu.VMEM((tk, tq), F32), pltpu.VMEM((tk, tq), F32),
                        pltpu.VMEM((tk, tq), BF16), pltpu.VMEM((1, tq), F32),
                        pltpu.VMEM((tk, tq), BF16), pltpu.VMEM((1, tq), F32)],
        compiler_params=_params("parallel", "parallel"),
        name="mixer_b",
    )(qk, qk, vt, g_arr)


def _decay_features_kernel(u_ref, wf_ref, b_ref, place_ref, ones_ref, qx_ref, kx_ref, carry_sc):
    tm = u_ref.shape[0]

    @pl.when(pl.program_id(0) == 0)
    def _():
        carry_sc[...] = jnp.zeros_like(carry_sc)

    log_f = _log_sigmoid(jnp.dot(u_ref[...], wf_ref[...], preferred_element_type=F32) + b_ref[...])
    r = lax.broadcasted_iota(jnp.int32, (tm, tm), 0)
    c = lax.broadcasted_iota(jnp.int32, (tm, tm), 1)
    upto = jnp.where(c <= r, 1.0, 0.0).astype(F32)
    cum = jnp.dot(upto, log_f, preferred_element_type=F32,
                  precision=lax.Precision.HIGHEST) + carry_sc[...]
    carry_sc[...] = cum[tm - 1:tm, :]

    c2 = cum * LOG2E
    hi = c2.astype(BF16)
    r1 = c2 - hi.astype(F32)
    mid = r1.astype(BF16)
    lo = (r1 - mid.astype(F32)).astype(BF16)
    placed = jnp.dot(jnp.concatenate([hi, mid, lo], axis=1), place_ref[...],
                     preferred_element_type=F32) + ones_ref[...]
    qx_ref[...] = placed[:, :BRANCH].astype(BF16)
    kx_ref[...] = placed[:, BRANCH:].astype(BF16)


def _decay_placement():
    place = np.zeros((3 * HEAD_DIM, 2 * BRANCH), np.float32)
    ones = np.zeros((1, 2 * BRANCH), np.float32)
    for head in range(N_HEADS):
        for term in range(3):
            place[term * HEAD_DIM + head, head * HEAD_DIM + term] = 1.0
            place[term * HEAD_DIM + head, BRANCH + head * HEAD_DIM + 3 + term] = -1.0
            ones[0, head * HEAD_DIM + 3 + term] = 1.0
            ones[0, BRANCH + head * HEAD_DIM + term] = 1.0
    return jnp.asarray(place, BF16), jnp.asarray(ones, F32)


def _decay_features(u, wf, bias):
    s, d = u.shape
    tm = min(512, s)
    place, ones = _decay_placement()
    out = jax.ShapeDtypeStruct((s, BRANCH), BF16)
    spec = pl.BlockSpec((tm, BRANCH), lambda i: (i, 0))
    return pl.pallas_call(
        _decay_features_kernel,
        out_shape=(out, out),
        grid=(s // tm,),
        in_specs=[pl.BlockSpec((tm, d), lambda i: (i, 0)),
                  pl.BlockSpec((d, HEAD_DIM), lambda i: (0, 0)),
                  pl.BlockSpec((1, HEAD_DIM), lambda i: (0, 0)),
                  pl.BlockSpec(place.shape, lambda i: (0, 0)),
                  pl.BlockSpec(ones.shape, lambda i: (0, 0))],
        out_specs=(spec, spec),
        scratch_shapes=[pltpu.VMEM((1, HEAD_DIM), F32)],
        compiler_params=_params("arbitrary"),
        name="decay_features",
    )(u, wf, bias, place, ones)


def _causal_sweep(i, tq, tk, masked_out_fn, scores_into, consume, s_bufs):
    per_q = tq // tk
    assert tq == per_q * tk and per_q in (1, 2)
    head = [(per_q * i + d, masked_out_fn(*_key_query_index(tk, tq, d * tk))) for d in range(per_q)]
    _pipelined_sweep(head, per_q * i, lambda n: jnp.clip(n, 0, per_q * i), scores_into, consume, s_bufs,
                     count_is_even=per_q == 2, steps_per_body=16 // per_q)


def _mixer_c_kernel(q_ref, qx_ref, k_ref, kx_ref, vt_ref, g_ref, o_ref, m_sc, acc_sc, s0_sc, s1_sc,
                    *, tq, tk):
    q = jnp.concatenate([q_ref[...], qx_ref[...]], axis=1)
    ones = _ones_rows(tk)
    _init_softmax_state(m_sc, acc_sc)

    def scores_into(buf, j):
        off = pl.multiple_of(j * tk, tk)
        k = jnp.concatenate([k_ref[pl.ds(off, tk), :], kx_ref[pl.ds(off, tk), :]], axis=1)
        buf[...] = lax.dot_general(k, q, _NT, preferred_element_type=F32)

    def consume(buf, _, j, masked_out):
        off = pl.multiple_of(j * tk, tk)
        v_aug = jnp.concatenate([vt_ref[:, pl.ds(off, tk)], ones], axis=0)
        _softmax_step(buf[...], masked_out, v_aug, m_sc, acc_sc)

    _causal_sweep(pl.program_id(1), tq, tk, lambda key, query: key > query, scores_into, consume,
                  (s0_sc, s1_sc))
    o_ref[...] = (_normalised_rows(acc_sc) * _silu(g_ref[...])).astype(o_ref.dtype)


def _mixer_c(qk, q_col, k_col, qx, kx, vt, v_row, g_arr, g_col, tq, tk):
    s = qk.shape[0]
    return pl.pallas_call(
        functools.partial(_mixer_c_kernel, tq=tq, tk=tk),
        out_shape=jax.ShapeDtypeStruct((s, BRANCH), BF16),
        grid=(N_HEADS, s // tq),
        in_specs=[pl.BlockSpec((tq, HEAD_DIM), lambda h, i: (i, q_col + h)),
                  pl.BlockSpec((tq, HEAD_DIM), lambda h, i: (i, h)),
                  pl.BlockSpec((s, HEAD_DIM), lambda h, i: (0, k_col + h)),
                  pl.BlockSpec((s, HEAD_DIM), lambda h, i: (0, h)),
                  pl.BlockSpec((HEAD_DIM, s), lambda h, i: (v_row + h, 0)),
                  pl.BlockSpec((tq, HEAD_DIM), lambda h, i: (i, g_col + h))],
        out_specs=pl.BlockSpec((tq, HEAD_DIM), lambda h, i: (i, h)),
        scratch_shapes=[pltpu.VMEM((1, tq), F32),
                        pltpu.VMEM((HEAD_DIM + BF16_SUBLANES, tq), F32),
                        pltpu.VMEM((tk, tq), F32), pltpu.VMEM((tk, tq), F32)],
        compiler_params=_params("parallel", "parallel"),
        name="mixer_c",
    )(qk, qx, qk, kx, vt, g_arr)


def _rope_kernel(x_ref, pos_ref, invf_ref, place_ref, o_ref, *, groups):
    ang = invf_ref[...] * pos_ref[...].astype(F32)
    place = place_ref[...]
    cos = jnp.dot(place, jnp.cos(ang), preferred_element_type=F32, precision=lax.Precision.HIGHEST).T
    sin = jnp.dot(place, jnp.sin(ang), preferred_element_type=F32, precision=lax.Precision.HIGHEST).T
    lane = lax.broadcasted_iota(jnp.int32, cos.shape, 1)
    first_half = (lane & (D_QK - 1)) < ROPE_DIM // 2
    for gidx in range(groups):
        x = x_ref[:, gidx * 128:(gidx + 1) * 128]
        partner = jnp.where(first_half,
                            -pltpu.roll(x, 128 - ROPE_DIM // 2, 1),
                            pltpu.roll(x, ROPE_DIM // 2, 1))
        o_ref[:, gidx * 128:(gidx + 1) * 128] = (x * cos + partner * sin).astype(o_ref.dtype)


def _rope(x, positions):
    s, n = x.shape
    tm = min(512, s)
    n_freq = ROPE_DIM // 2
    inv_freq = ROPE_THETA ** (-jnp.arange(0, ROPE_DIM, 2, dtype=F32) / ROPE_DIM)
    invf = jnp.pad(inv_freq, (0, n_freq)).reshape(2 * n_freq, 1)
    slot = np.arange(128) % D_QK
    place = np.zeros((128, 2 * n_freq), np.float32)
    place[np.arange(128), np.where(slot < ROPE_DIM, slot % n_freq, n_freq)] = 1.0
    return pl.pallas_call(
        functools.partial(_rope_kernel, groups=n // 128),
        out_shape=jax.ShapeDtypeStruct((s, n), BF16),
        grid=(s // tm,),
        in_specs=[pl.BlockSpec((tm, n), lambda i: (i, 0)),
                  pl.BlockSpec((1, tm), lambda i: (0, i)),
                  pl.BlockSpec(invf.shape, lambda i: (0, 0)),
                  pl.BlockSpec(place.shape, lambda i: (0, 0))],
        out_specs=pl.BlockSpec((tm, n), lambda i: (i, 0)),
        compiler_params=_params("parallel"),
        name="rope",
    )(x, positions, invf, jnp.asarray(place))


def _mixer_d_kernel(q_ref, k_ref, vt_ref, g_ref, lam_ref, sub_ref, o_ref,
                    m1_sc, a1_sc, m2_sc, a2_sc, sa1_sc, sa2_sc, sb1_sc, sb2_sc, *, tq, tk, lambda_init):
    q = q_ref[...]
    lane = lax.broadcasted_iota(jnp.int32, q.shape, 1)
    zero = jnp.zeros_like(q)
    q1 = jnp.where(lane < D_QK, q, zero)
    q2 = jnp.where(lane >= D_QK, q, zero)
    ones = _ones_rows(tk)
    _init_softmax_state(m1_sc, a1_sc)
    _init_softmax_state(m2_sc, a2_sc)

    def scores_into(bufs, j):
        off = pl.multiple_of(j * tk, tk)
        k = k_ref[pl.ds(off, tk), :]
        for qm, buf in zip((q1, q2), bufs):
            buf[...] = lax.dot_general(k, qm, _NT, preferred_element_type=F32)

    def consume(bufs, _, j, masked_out):
        off = pl.multiple_of(j * tk, tk)
        v_aug = jnp.concatenate([vt_ref[:, pl.ds(off, tk)], ones], axis=0)
        for buf, m_sc, a_sc in zip(bufs, (m1_sc, m2_sc), (a1_sc, a2_sc)):
            _softmax_step(buf[...], masked_out, v_aug, m_sc, a_sc)

    _causal_sweep(pl.program_id(1), tq, tk,
                  lambda key, query: (key >> CHUNK_SHIFT) > (query >> CHUNK_SHIFT),
                  scores_into, consume, ((sa1_sc, sa2_sc), (sb1_sc, sb2_sc)))

    lv = lam_ref[...]
    lam = (jnp.exp(jnp.sum(lv[0:1] * lv[1:2], axis=1, keepdims=True))
           - jnp.exp(jnp.sum(lv[2:3] * lv[3:4], axis=1, keepdims=True)) + lambda_init)
    o = _normalised_rows(a1_sc) - lam * _normalised_rows(a2_sc)
    ms = jnp.mean(o * o, axis=-1, keepdims=True)
    y = o * lax.rsqrt(ms + EPS) * sub_ref[...] * (1.0 - lambda_init)
    o_ref[...] = (y * _silu(g_ref[...])).astype(o_ref.dtype)


def _mixer_d(qk_arr, q_col, k_col, vt, v_row, g_arr, g_col, lam_vecs, subln, lambda_init, tq, tk):
    s = qk_arr.shape[0]
    kern = functools.partial(_mixer_d_kernel, tq=tq, tk=tk, lambda_init=lambda_init)
    row_stat = pltpu.VMEM((1, tq), F32)
    acc = pltpu.VMEM((HEAD_DIM + BF16_SUBLANES, tq), F32)
    return pl.pallas_call(
        kern,
        out_shape=jax.ShapeDtypeStruct((s, BRANCH), BF16),
        grid=(N_HEADS, s // tq),
        in_specs=[pl.BlockSpec((tq, HEAD_DIM), lambda h, i: (i, q_col + h)),
                  pl.BlockSpec((s, HEAD_DIM), lambda h, i: (0, k_col + h)),
                  pl.BlockSpec((HEAD_DIM, s), lambda h, i: (v_row + h, 0)),
                  pl.BlockSpec((tq, HEAD_DIM), lambda h, i: (i, g_col + h)),
                  pl.BlockSpec((4, D_QK), lambda h, i: (0, 0)),
                  pl.BlockSpec((1, HEAD_DIM), lambda h, i: (0, 0))],
        out_specs=pl.BlockSpec((tq, HEAD_DIM), lambda h, i: (i, h)),
        scratch_shapes=[row_stat, acc, row_stat, acc] + [pltpu.VMEM((tk, tq), F32)] * 4,
        compiler_params=_params("parallel", "parallel"),
        name="mixer_d",
    )(qk_arr, qk_arr, vt, g_arr, lam_vecs, subln.reshape(1, HEAD_DIM))


def _out_proj_kernel(ya_ref, yb_ref, w_ref, h_ref, g_ref, o_ref):
    half = ya_ref.shape[1]
    y = (jnp.dot(ya_ref[...], w_ref[:half, :], preferred_element_type=F32)
         + jnp.dot(yb_ref[...], w_ref[half:, :], preferred_element_type=F32))
    ms = jnp.mean(y * y, axis=-1, keepdims=True)
    o_ref[...] = h_ref[...] + y * lax.rsqrt(ms + EPS) * g_ref[...]


def _out_proj(ya, yb, w, h, g):
    s, d = h.shape
    tm = min(512, s)
    return pl.pallas_call(
        _out_proj_kernel,
        out_shape=jax.ShapeDtypeStruct((s, d), F32),
        grid=(s // tm,),
        in_specs=[pl.BlockSpec((tm, BRANCH), lambda i: (i, 0)),
                  pl.BlockSpec((tm, BRANCH), lambda i: (i, 0)),
                  pl.BlockSpec((2 * BRANCH, d), lambda i: (0, 0)),
                  pl.BlockSpec((tm, d), lambda i: (i, 0)),
                  pl.BlockSpec((1, d), lambda i: (0, 0))],
        out_specs=pl.BlockSpec((tm, d), lambda i: (i, 0)),
        compiler_params=_params("parallel"),
        name="out_proj",
    )(ya, yb, w, h, g.reshape(1, d))


def _rel_bias_tile(rel_bias, tq):
    rows = 2 * A_PAD + tq
    period = rows + tq
    k = np.arange(period)
    c_minus_r = np.where(k < rows, k, k - period)
    line = rel_bias.astype(F32)[:, np.clip(A_PAD - c_minus_r, -REL_CLIP, REL_CLIP) + REL_CLIP] * LOG2E
    flat = jnp.tile(line, (1, tq))[:, :tq * (period - 1)]
    tile = flat.reshape(rel_bias.shape[0], tq, period - 1)[:, :, :rows]
    qc = np.arange(tq)[:, None] >> CHUNK_SHIFT
    kc = np.arange(rows)[None, :] >> CHUNK_SHIFT
    in_band = (kc >= qc) & (kc <= qc + A_LEFT_CHUNKS)
    return jnp.swapaxes(jnp.where(in_band, tile, NEG_INF), 1, 2)


def _even_layer(h, w_in, rel_bias, w_out, norm_pre, norm_post):
    s = h.shape[0]
    u = _rmsnorm(h, norm_pre)
    b = BRANCH
    log2_scale = HEAD_DIM ** -0.5 * LOG2E
    w_rows = jnp.concatenate([w_in[:, :b] * log2_scale, w_in[:, b:2 * b],
                              w_in[:, 4 * b:5 * b] * log2_scale, w_in[:, 5 * b:6 * b]], axis=1).astype(BF16)
    w_g = jnp.concatenate([w_in[:, 3 * b:4 * b], w_in[:, 7 * b:]], axis=1).astype(BF16)
    w_vt = jnp.concatenate([w_in[:, 2 * b:3 * b], w_in[:, 6 * b:7 * b]], axis=1).T.astype(BF16)
    rows = _matmul(u, w_rows, BF16)
    gates = _matmul(u, w_g, F32)
    vt = _matmul_t(w_vt, u, BF16)
    tq_a = min(256, s)
    y_a = _mixer_a(rows, 0, N_HEADS, vt, 0, gates, 0, _rel_bias_tile(rel_bias, tq_a), tq_a)
    y_b = _mixer_b(rows, 2 * N_HEADS, 3 * N_HEADS, vt, N_HEADS, gates, N_HEADS, min(512, s), min(256, s))
    return _out_proj(y_a, y_b, w_out.astype(BF16), h, norm_post)


def _odd_layer(h, positions, w_in, forget_bias, lq1, lk1, lq2, lk2, subln, w_out, norm_pre, norm_post,
               lambda_init):
    s, d = h.shape
    u = _rmsnorm(h, norm_pre)
    b = BRANCH
    f0 = 4 * b
    d0 = f0 + N_HEADS
    v0 = d0 + 4 * N_HEADS * D_QK
    w_rows = jnp.concatenate([w_in[:, :b] * (HEAD_DIM ** -0.5 * LOG2E), w_in[:, b:2 * b]],
                             axis=1).astype(BF16)
    w_g = jnp.concatenate([w_in[:, 3 * b:4 * b], w_in[:, v0 + b:]], axis=1).astype(BF16)
    w_vt = jnp.concatenate([w_in[:, 2 * b:3 * b], w_in[:, v0:v0 + b]], axis=1).T.astype(BF16)
    wd = w_in[:, d0:v0].reshape(d, 4, N_HEADS, D_QK)
    w_qk = jnp.concatenate([
        jnp.concatenate([wd[:, 0], wd[:, 1]], axis=2).reshape(d, b) * (D_QK ** -0.5 * LOG2E),
        jnp.concatenate([wd[:, 2], wd[:, 3]], axis=2).reshape(d, b),
    ], axis=1).astype(BF16)
    w_f = jnp.pad(w_in[:, f0:d0], ((0, 0), (0, HEAD_DIM - N_HEADS))).astype(BF16)
    b_f = jnp.pad(forget_bias.astype(F32), (0, HEAD_DIM - N_HEADS)).reshape(1, HEAD_DIM)

    rows = _matmul(u, w_rows, BF16)
    gates = _matmul(u, w_g, F32)
    vt = _matmul_t(w_vt, u, BF16)
    qk_d = _matmul(u, w_qk, F32)

    qx, kx = _decay_features(u, w_f, b_f)
    tq = tk = min(512, s)
    y_c = _mixer_c(rows, 0, N_HEADS, qx, kx, vt, 0, gates, 0, tq, tk)

    qk_rot = _rope(qk_d, positions)
    lam_vecs = jnp.stack([lq1, lk1, lq2, lk2]).astype(F32)
    y_d = _mixer_d(qk_rot, 0, N_HEADS, vt, N_HEADS, gates, N_HEADS, lam_vecs, subln, lambda_init, tq, tk)
    return _out_proj(y_c, y_d, w_out.astype(BF16), h, norm_post)


def kernel(x, positions, even_w_in, even_rel_bias, even_w_out, even_norm_pre, even_norm_post,
           odd_w_in, odd_forget_bias, odd_lambda_q1, odd_lambda_k1, odd_lambda_q2, odd_lambda_k2,
           odd_subln, odd_w_out, odd_norm_pre, odd_norm_post):
    assert x.shape[0] == 1
    h = x[0]
    depth = even_w_in.shape[0] + odd_w_in.shape[0]
    for layer in range(depth):
        i = layer // 2
        if layer % 2 == 0:
            h = _even_layer(h, even_w_in[i], even_rel_bias[i], even_w_out[i],
                            even_norm_pre[i], even_norm_post[i])
        else:
            lambda_init = 0.8 - 0.6 * math.exp(-0.3 * layer)
            h = _odd_layer(h, positions, odd_w_in[i], odd_forget_bias[i],
                           odd_lambda_q1[i], odd_lambda_k1[i], odd_lambda_q2[i], odd_lambda_k2[i],
                           odd_subln[i], odd_w_out[i], odd_norm_pre[i], odd_norm_post[i], lambda_init)
    return h[None]
```

```python
import functools
import math

import jax
import jax.numpy as jnp
import numpy as np
from jax import lax
from jax.experimental import pallas as pl
from jax.experimental.pallas import tpu as pltpu

F32 = jnp.float32
BF16 = jnp.bfloat16

HEAD_DIM = 128
N_HEADS = 8
BRANCH = N_HEADS * HEAD_DIM
CHUNK = 64
CHUNK_SHIFT = 6
A_LEFT_CHUNKS = 8
A_PAD = A_LEFT_CHUNKS * CHUNK
REL_CLIP = 128
D_QK = 64
ROPE_THETA = 500000.0
ROPE_DIM = 16
EPS = 1e-6
NEG_INF = float("-inf")
LOG2E = math.log2(math.e)
BF16_SUBLANES = 16
VMEM_LIMIT_BYTES = 48 * 1024 * 1024

_NT = (((1,), (1,)), ((), ()))


def _params(*semantics):
    return pltpu.CompilerParams(dimension_semantics=semantics, vmem_limit_bytes=VMEM_LIMIT_BYTES)


def _silu(g):
    return g * jax.nn.sigmoid(g)


def _log_sigmoid(z):
    return jnp.minimum(z, 0.0) - jnp.log1p(jnp.exp(-jnp.abs(z)))


def _rmsnorm_kernel(x_ref, g_ref, o_ref):
    x = x_ref[...]
    ms = jnp.mean(x * x, axis=-1, keepdims=True)
    o_ref[...] = (x * lax.rsqrt(ms + EPS) * g_ref[...]).astype(o_ref.dtype)


def _rmsnorm(x, g):
    s, d = x.shape
    tm = min(512, s)
    return pl.pallas_call(
        _rmsnorm_kernel,
        out_shape=jax.ShapeDtypeStruct((s, d), BF16),
        grid=(s // tm,),
        in_specs=[pl.BlockSpec((tm, d), lambda i: (i, 0)),
                  pl.BlockSpec((1, d), lambda i: (0, 0))],
        out_specs=pl.BlockSpec((tm, d), lambda i: (i, 0)),
        compiler_params=_params("parallel"),
        name="rmsnorm",
    )(x, g.reshape(1, d))


def _matmul_kernel(x_ref, w_ref, o_ref):
    o_ref[...] = jnp.dot(x_ref[...], w_ref[...], preferred_element_type=F32).astype(o_ref.dtype)


def _matmul(x, w, out_dtype):
    m, k = x.shape
    n = w.shape[1]
    tm = min(1024, m)
    tn = 2048 if n % 2048 == 0 else n
    return pl.pallas_call(
        _matmul_kernel,
        out_shape=jax.ShapeDtypeStruct((m, n), out_dtype),
        grid=(n // tn, m // tm),
        in_specs=[pl.BlockSpec((tm, k), lambda j, i: (i, 0)),
                  pl.BlockSpec((k, tn), lambda j, i: (0, j))],
        out_specs=pl.BlockSpec((tm, tn), lambda j, i: (i, j)),
        compiler_params=_params("parallel", "parallel"),
        name="in_proj",
    )(x, w)


def _matmul_t_kernel(w_ref, x_ref, o_ref):
    o_ref[...] = lax.dot_general(w_ref[...], x_ref[...], _NT,
                                 preferred_element_type=F32).astype(o_ref.dtype)


def _matmul_t(w_t, x, out_dtype):
    n, k = w_t.shape
    m = x.shape[0]
    tm = min(1024, m)
    tn = 1024 if n % 1024 == 0 else n
    return pl.pallas_call(
        _matmul_t_kernel,
        out_shape=jax.ShapeDtypeStruct((n, m), out_dtype),
        grid=(n // tn, m // tm),
        in_specs=[pl.BlockSpec((tn, k), lambda j, i: (j, 0)),
                  pl.BlockSpec((tm, k), lambda j, i: (i, 0))],
        out_specs=pl.BlockSpec((tn, tm), lambda j, i: (j, i)),
        compiler_params=_params("parallel", "parallel"),
        name="in_proj_t",
    )(w_t, x)


def _mixer_a_kernel(q_ref, k_ref, vt_ref, g_ref, b_ref, o_ref, *, tq, band):
    ones = _ones_rows(band)
    for sub in range(q_ref.shape[0] // tq):
        rows = slice(sub * tq, (sub + 1) * tq)
        first = (pl.program_id(1) * (q_ref.shape[0] // tq) + sub) * tq - A_PAD
        start = pl.multiple_of(jnp.maximum(first, 0), tq)
        cut = pl.multiple_of(start - first, tq)
        s = (lax.dot_general(k_ref[pl.ds(start, band), :], q_ref[rows, :], _NT, preferred_element_type=F32)
             + b_ref[0, pl.ds(cut, band), :])
        m = jnp.max(s, axis=0, keepdims=True)
        p = jnp.exp2(s - m).astype(BF16)
        v_aug = jnp.concatenate([vt_ref[:, pl.ds(start, band)], ones], axis=0)
        acc = jnp.dot(v_aug, p, preferred_element_type=F32)
        o = (acc[:HEAD_DIM] / acc[HEAD_DIM:HEAD_DIM + 1]).T
        o_ref[rows, :] = (o * _silu(g_ref[rows, :])).astype(o_ref.dtype)


def _mixer_a(qk, q_col, k_col, vt, v_row, g_arr, g_col, bias_tile, tq):
    s = qk.shape[0]
    assert A_PAD % tq == 0
    band = A_PAD + tq
    tstep = min(4 * tq, s)
    kern = functools.partial(_mixer_a_kernel, tq=tq, band=band)
    return pl.pallas_call(
        kern,
        out_shape=jax.ShapeDtypeStruct((s, BRANCH), BF16),
        grid=(N_HEADS, s // tstep),
        in_specs=[pl.BlockSpec((tstep, HEAD_DIM), lambda h, i: (i, q_col + h)),
                  pl.BlockSpec((s, HEAD_DIM), lambda h, i: (0, k_col + h)),
                  pl.BlockSpec((HEAD_DIM, s), lambda h, i: (v_row + h, 0)),
                  pl.BlockSpec((tstep, HEAD_DIM), lambda h, i: (i, g_col + h)),
                  pl.BlockSpec((1, band + A_PAD, tq), lambda h, i: (h, 0, 0))],
        out_specs=pl.BlockSpec((tstep, HEAD_DIM), lambda h, i: (i, h)),
        compiler_params=_params("parallel", "parallel"),
        name="mixer_a",
    )(qk, qk, vt, g_arr, bias_tile)


def _key_query_index(tk, tq, key_off):
    key = lax.broadcasted_iota(jnp.int32, (tk, tq), 0) + key_off
    query = lax.broadcasted_iota(jnp.int32, (tk, tq), 1)
    return key, query


def _softmax_step(s, masked_out, v_aug, m_sc, acc_sc):
    if masked_out is not None:
        s = jnp.where(masked_out, NEG_INF, s)
    m_old = m_sc[...]
    m_new = jnp.maximum(m_old, jnp.max(s, axis=0, keepdims=True))
    p = jnp.exp2(s - m_new).astype(BF16)
    alpha = jnp.exp2(m_old - m_new)
    acc_sc[...] = alpha * acc_sc[...] + jnp.dot(v_aug, p, preferred_element_type=F32)
    m_sc[...] = m_new


def _init_softmax_state(m_sc, acc_sc):
    m_sc[...] = jnp.full_like(m_sc, NEG_INF)
    acc_sc[...] = jnp.zeros_like(acc_sc)


def _normalised_rows(acc_sc):
    acc = acc_sc[...]
    return (acc[:HEAD_DIM] / acc[HEAD_DIM:HEAD_DIM + 1]).T


def _ones_rows(width):
    return jnp.ones((BF16_SUBLANES, width), BF16)


def _pipelined_sweep(head, count, rest_block, scores_into, consume, s_bufs,
                     accumulate=None, w_bufs=(None, None), count_is_even=False, steps_per_body=4):
    n_head = len(head)
    last_head = head[-1][0]
    defer_accumulate = accumulate is not None

    def step(par, block, mask, prev_block, next_block):
        if next_block is not None:
            scores_into(s_bufs[1 - par], next_block)
        consume(s_bufs[par], w_bufs[par], block, mask)
        if defer_accumulate and prev_block is not None:
            accumulate(w_bufs[1 - par], prev_block)

    def rest_prev(n):
        return jnp.where(n == 0, last_head, rest_block(n - 1))

    scores_into(s_bufs[0], head[0][0])
    for g, (block, mask) in enumerate(head):
        step(g % 2, block, mask, head[g - 1][0] if g else None,
             head[g + 1][0] if g + 1 < n_head else rest_block(0))

    def body_of(steps):
        def body(_, first):
            for u in range(steps):
                n = first + u
                step((n_head + u) % 2, rest_block(n), None,
                     rest_block(n - 1) if u else rest_prev(n), rest_block(n + 1))
            return first + steps
        return body

    done = 0
    steps = steps_per_body
    while steps >= 2:
        done = lax.fori_loop(0, (count - done) // steps, body_of(steps), done)
        steps //= 2
    last_block = rest_prev(count)
    if count_is_even:
        if defer_accumulate:
            accumulate(w_bufs[(n_head - 1) % 2], last_block)
        return

    @pl.when(count % 2 == 1)
    def _():
        n = count - 1
        step(n_head % 2, rest_block(n), None, rest_prev(n), None)
        if defer_accumulate:
            accumulate(w_bufs[n_head % 2], last_block)

    if defer_accumulate:
        @pl.when(count % 2 == 0)
        def _():
            accumulate(w_bufs[(n_head - 1) % 2], last_block)


def _mixer_b_kernel(q_ref, k_ref, vt_ref, g_ref, o_ref, acc_sc, run_sc, s0_sc, s1_sc,
                    w0_sc, c0_sc, w1_sc, c1_sc, *, tq, tk):
    per_q = tq // tk
    assert tq == per_q * tk and per_q % 2 == 0
    i = pl.program_id(1)
    q = q_ref[...]
    kk = lax.broadcasted_iota(jnp.int32, (tk + BF16_SUBLANES, tk), 0)
    jj = lax.broadcasted_iota(jnp.int32, (tk + BF16_SUBLANES, tk), 1)
    suffix = jnp.where((jj > kk) | (kk >= tk), 1.0, 0.0).astype(BF16)
    acc_sc[...] = jnp.zeros_like(acc_sc)
    run_sc[...] = jnp.zeros_like(run_sc)

    def scores_into(buf, j):
        off = pl.multiple_of(j * tk, tk)
        buf[...] = lax.dot_general(k_ref[pl.ds(off, tk), :], q, _NT, preferred_element_type=F32)

    def weights_from(buf, w_buf, j, key_off):
        w_ref, carried_ref = w_buf
        z = buf[...]
        neg_abs = lax.bitcast_convert_type(
            lax.bitcast_convert_type(z, jnp.uint32) | jnp.uint32(0x80000000), F32)
        log_beta = jnp.minimum(z, 0.0) - jnp.log(1.0 + jnp.exp2(neg_abs)) * LOG2E
        log_keep = log_beta - z
        if key_off is not None:
            key, query = _key_query_index(tk, tq, key_off)
            strict = key < query
            log_keep = jnp.where(strict, log_keep, 0.0)
        later = jnp.dot(suffix, log_keep.astype(BF16), preferred_element_type=F32)
        w = jnp.exp2(log_beta + later[:tk])
        if key_off is not None:
            w = jnp.where(strict, w, 0.0)
        w_ref[...] = w.astype(BF16)
        carried_ref[...] = jnp.exp2(run_sc[...])
        run_sc[...] += later[tk:tk + 1]

    def accumulate(w_buf, j):
        w_ref, carried_ref = w_buf
        off = pl.multiple_of(j * tk, tk)
        acc_sc[...] += carried_ref[...] * jnp.dot(vt_ref[:, pl.ds(off, tk)], w_ref[...],
                                                  preferred_element_type=F32)

    def below(n):
        return jnp.clip(per_q * i - 1 - n, 0, per_q * i)

    on_diagonal = [(per_q * i + d, d * tk) for d in reversed(range(per_q))]
    _pipelined_sweep(on_diagonal, per_q * i, below, scores_into, weights_from,
                     (s0_sc, s1_sc), accumulate, ((w0_sc, c0_sc), (w1_sc, c1_sc)), count_is_even=True,
                     steps_per_body=16)
    o_ref[...] = (acc_sc[...].T * _silu(g_ref[...])).astype(o_ref.dtype)


def _mixer_b(qk, q_col, k_col, vt, v_row, g_arr, g_col, tq, tk):
    s = qk.shape[0]
    kern = functools.partial(_mixer_b_kernel, tq=tq, tk=tk)
    return pl.pallas_call(
        kern,
        out_shape=jax.ShapeDtypeStruct((s, BRANCH), BF16),
        grid=(N_HEADS, s // tq),
        in_specs=[pl.BlockSpec((tq, HEAD_DIM), lambda h, i: (i, q_col + h)),
                  pl.BlockSpec((s, HEAD_DIM), lambda h, i: (0, k_col + h)),
                  pl.BlockSpec((HEAD_DIM, s), lambda h, i: (v_row + h, 0)),
                  pl.BlockSpec((tq, HEAD_DIM), lambda h, i: (i, g_col + h))],
        out_specs=pl.BlockSpec((tq, HEAD_DIM), lambda h, i: (i, h)),
        scratch_shapes=[pltpu.VMEM((HEAD_DIM, tq), F32), pltpu.VMEM((1, tq), F32),
                        pltpu.VMEM((tk, tq), F32), pltpu.VMEM((tk, tq), F32),
                        pltpu.VMEM((tk, tq), BF16), pltpu.VMEM((1, tq), F32),
                        pltpu.VMEM((tk, tq), BF16), pltpu.VMEM((1, tq), F32)],
        compiler_params=_params("parallel", "parallel"),
        name="mixer_b",
    )(qk, qk, vt, g_arr)


def _decay_features_kernel(u_ref, wf_ref, b_ref, place_ref, ones_ref, qx_ref, kx_ref, carry_sc):
    tm = u_ref.shape[0]

    @pl.when(pl.program_id(0) == 0)
    def _():
        carry_sc[...] = jnp.zeros_like(carry_sc)

    log_f = _log_sigmoid(jnp.dot(u_ref[...], wf_ref[...], preferred_element_type=F32) + b_ref[...])
    r = lax.broadcasted_iota(jnp.int32, (tm, tm), 0)
    c = lax.broadcasted_iota(jnp.int32, (tm, tm), 1)
    upto = jnp.where(c <= r, 1.0, 0.0).astype(F32)
    cum = jnp.dot(upto, log_f, preferred_element_type=F32,
                  precision=lax.Precision.HIGHEST) + carry_sc[...]
    carry_sc[...] = cum[tm - 1:tm, :]

    c2 = cum * LOG2E
    hi = c2.astype(BF16)
    r1 = c2 - hi.astype(F32)
    mid = r1.astype(BF16)
    lo = (r1 - mid.astype(F32)).astype(BF16)
    placed = jnp.dot(jnp.concatenate([hi, mid, lo], axis=1), place_ref[...],
                     preferred_element_type=F32) + ones_ref[...]
    qx_ref[...] = placed[:, :BRANCH].astype(BF16)
    kx_ref[...] = placed[:, BRANCH:].astype(BF16)


def _decay_placement():
    place = np.zeros((3 * HEAD_DIM, 2 * BRANCH), np.float32)
    ones = np.zeros((1, 2 * BRANCH), np.float32)
    for head in range(N_HEADS):
        for term in range(3):
            place[term * HEAD_DIM + head, head * HEAD_DIM + term] = 1.0
            place[term * HEAD_DIM + head, BRANCH + head * HEAD_DIM + 3 + term] = -1.0
            ones[0, head * HEAD_DIM + 3 + term] = 1.0
            ones[0, BRANCH + head * HEAD_DIM + term] = 1.0
    return jnp.asarray(place, BF16), jnp.asarray(ones, F32)


def _decay_features(u, wf, bias):
    s, d = u.shape
    tm = min(512, s)
    place, ones = _decay_placement()
    out = jax.ShapeDtypeStruct((s, BRANCH), BF16)
    spec = pl.BlockSpec((tm, BRANCH), lambda i: (i, 0))
    return pl.pallas_call(
        _decay_features_kernel,
        out_shape=(out, out),
        grid=(s // tm,),
        in_specs=[pl.BlockSpec((tm, d), lambda i: (i, 0)),
                  pl.BlockSpec((d, HEAD_DIM), lambda i: (0, 0)),
                  pl.BlockSpec((1, HEAD_DIM), lambda i: (0, 0)),
                  pl.BlockSpec(place.shape, lambda i: (0, 0)),
                  pl.BlockSpec(ones.shape, lambda i: (0, 0))],
        out_specs=(spec, spec),
        scratch_shapes=[pltpu.VMEM((1, HEAD_DIM), F32)],
        compiler_params=_params("arbitrary"),
        name="decay_features",
    )(u, wf, bias, place, ones)


def _causal_sweep(i, tq, tk, masked_out_fn, scores_into, consume, s_bufs):
    per_q = tq // tk
    assert tq == per_q * tk and per_q in (1, 2)
    head = [(per_q * i + d, masked_out_fn(*_key_query_index(tk, tq, d * tk))) for d in range(per_q)]
    _pipelined_sweep(head, per_q * i, lambda n: jnp.clip(n, 0, per_q * i), scores_into, consume, s_bufs,
                     count_is_even=per_q == 2, steps_per_body=16 // per_q)


def _mixer_c_kernel(q_ref, qx_ref, k_ref, kx_ref, vt_ref, g_ref, o_ref, m_sc, acc_sc, s0_sc, s1_sc,
                    *, tq, tk):
    q = jnp.concatenate([q_ref[...], qx_ref[...]], axis=1)
    ones = _ones_rows(tk)
    _init_softmax_state(m_sc, acc_sc)

    def scores_into(buf, j):
        off = pl.multiple_of(j * tk, tk)
        k = jnp.concatenate([k_ref[pl.ds(off, tk), :], kx_ref[pl.ds(off, tk), :]], axis=1)
        buf[...] = lax.dot_general(k, q, _NT, preferred_element_type=F32)

    def consume(buf, _, j, masked_out):
        off = pl.multiple_of(j * tk, tk)
        v_aug = jnp.concatenate([vt_ref[:, pl.ds(off, tk)], ones], axis=0)
        _softmax_step(buf[...], masked_out, v_aug, m_sc, acc_sc)

    _causal_sweep(pl.program_id(1), tq, tk, lambda key, query: key > query, scores_into, consume,
                  (s0_sc, s1_sc))
    o_ref[...] = (_normalised_rows(acc_sc) * _silu(g_ref[...])).astype(o_ref.dtype)


def _mixer_c(qk, q_col, k_col, qx, kx, vt, v_row, g_arr, g_col, tq, tk):
    s = qk.shape[0]
    return pl.pallas_call(
        functools.partial(_mixer_c_kernel, tq=tq, tk=tk),
        out_shape=jax.ShapeDtypeStruct((s, BRANCH), BF16),
        grid=(N_HEADS, s // tq),
        in_specs=[pl.BlockSpec((tq, HEAD_DIM), lambda h, i: (i, q_col + h)),
                  pl.BlockSpec((tq, HEAD_DIM), lambda h, i: (i, h)),
                  pl.BlockSpec((s, HEAD_DIM), lambda h, i: (0, k_col + h)),
                  pl.BlockSpec((s, HEAD_DIM), lambda h, i: (0, h)),
                  pl.BlockSpec((HEAD_DIM, s), lambda h, i: (v_row + h, 0)),
                  pl.BlockSpec((tq, HEAD_DIM), lambda h, i: (i, g_col + h))],
        out_specs=pl.BlockSpec((tq, HEAD_DIM), lambda h, i: (i, h)),
        scratch_shapes=[pltpu.VMEM((1, tq), F32),
                        pltpu.VMEM((HEAD_DIM + BF16_SUBLANES, tq), F32),
                        pltpu.VMEM((tk, tq), F32), pltpu.VMEM((tk, tq), F32)],
        compiler_params=_params("parallel", "parallel"),
        name="mixer_c",
    )(qk, qx, qk, kx, vt, g_arr)


def _rope_kernel(x_ref, pos_ref, invf_ref, place_ref, o_ref, *, groups):
    ang = invf_ref[...] * pos_ref[...].astype(F32)
    place = place_ref[...]
    cos = jnp.dot(place, jnp.cos(ang), preferred_element_type=F32, precision=lax.Precision.HIGHEST).T
    sin = jnp.dot(place, jnp.sin(ang), preferred_element_type=F32, precision=lax.Precision.HIGHEST).T
    lane = lax.broadcasted_iota(jnp.int32, cos.shape, 1)
    first_half = (lane & (D_QK - 1)) < ROPE_DIM // 2
    for gidx in range(groups):
        x = x_ref[:, gidx * 128:(gidx + 1) * 128]
        partner = jnp.where(first_half,
                            -pltpu.roll(x, 128 - ROPE_DIM // 2, 1),
                            pltpu.roll(x, ROPE_DIM // 2, 1))
        o_ref[:, gidx * 128:(gidx + 1) * 128] = (x * cos + partner * sin).astype(o_ref.dtype)


def _rope(x, positions):
    s, n = x.shape
    tm = min(512, s)
    n_freq = ROPE_DIM // 2
    inv_freq = ROPE_THETA ** (-jnp.arange(0, ROPE_DIM, 2, dtype=F32) / ROPE_DIM)
    invf = jnp.pad(inv_freq, (0, n_freq)).reshape(2 * n_freq, 1)
    slot = np.arange(128) % D_QK
    place = np.zeros((128, 2 * n_freq), np.float32)
    place[np.arange(128), np.where(slot < ROPE_DIM, slot % n_freq, n_freq)] = 1.0
    return pl.pallas_call(
        functools.partial(_rope_kernel, groups=n // 128),
        out_shape=jax.ShapeDtypeStruct((s, n), BF16),
        grid=(s // tm,),
        in_specs=[pl.BlockSpec((tm, n), lambda i: (i, 0)),
                  pl.BlockSpec((1, tm), lambda i: (0, i)),
                  pl.BlockSpec(invf.shape, lambda i: (0, 0)),
                  pl.BlockSpec(place.shape, lambda i: (0, 0))],
        out_specs=pl.BlockSpec((tm, n), lambda i: (i, 0)),
        compiler_params=_params("parallel"),
        name="rope",
    )(x, positions, invf, jnp.asarray(place))


def _mixer_d_kernel(q_ref, k_ref, vt_ref, g_ref, lam_ref, sub_ref, o_ref,
                    m1_sc, a1_sc, m2_sc, a2_sc, sa1_sc, sa2_sc, sb1_sc, sb2_sc, *, tq, tk, lambda_init):
    q = q_ref[...]
    lane = lax.broadcasted_iota(jnp.int32, q.shape, 1)
    zero = jnp.zeros_like(q)
    q1 = jnp.where(lane < D_QK, q, zero)
    q2 = jnp.where(lane >= D_QK, q, zero)
    ones = _ones_rows(tk)
    _init_softmax_state(m1_sc, a1_sc)
    _init_softmax_state(m2_sc, a2_sc)

    def scores_into(bufs, j):
        off = pl.multiple_of(j * tk, tk)
        k = k_ref[pl.ds(off, tk), :]
        for qm, buf in zip((q1, q2), bufs):
            buf[...] = lax.dot_general(k, qm, _NT, preferred_element_type=F32)

    def consume(bufs, _, j, masked_out):
        off = pl.multiple_of(j * tk, tk)
        v_aug = jnp.concatenate([vt_ref[:, pl.ds(off, tk)], ones], axis=0)
        for buf, m_sc, a_sc in zip(bufs, (m1_sc, m2_sc), (a1_sc, a2_sc)):
            _softmax_step(buf[...], masked_out, v_aug, m_sc, a_sc)

    _causal_sweep(pl.program_id(1), tq, tk,
                  lambda key, query: (key >> CHUNK_SHIFT) > (query >> CHUNK_SHIFT),
                  scores_into, consume, ((sa1_sc, sa2_sc), (sb1_sc, sb2_sc)))

    lv = lam_ref[...]
    lam = (jnp.exp(jnp.sum(lv[0:1] * lv[1:2], axis=1, keepdims=True))
           - jnp.exp(jnp.sum(lv[2:3] * lv[3:4], axis=1, keepdims=True)) + lambda_init)
    o = _normalised_rows(a1_sc) - lam * _normalised_rows(a2_sc)
    ms = jnp.mean(o * o, axis=-1, keepdims=True)
    y = o * lax.rsqrt(ms + EPS) * sub_ref[...] * (1.0 - lambda_init)
    o_ref[...] = (y * _silu(g_ref[...])).astype(o_ref.dtype)


def _mixer_d(qk_arr, q_col, k_col, vt, v_row, g_arr, g_col, lam_vecs, subln, lambda_init, tq, tk):
    s = qk_arr.shape[0]
    kern = functools.partial(_mixer_d_kernel, tq=tq, tk=tk, lambda_init=lambda_init)
    row_stat = pltpu.VMEM((1, tq), F32)
    acc = pltpu.VMEM((HEAD_DIM + BF16_SUBLANES, tq), F32)
    return pl.pallas_call(
        kern,
        out_shape=jax.ShapeDtypeStruct((s, BRANCH), BF16),
        grid=(N_HEADS, s // tq),
        in_specs=[pl.BlockSpec((tq, HEAD_DIM), lambda h, i: (i, q_col + h)),
                  pl.BlockSpec((s, HEAD_DIM), lambda h, i: (0, k_col + h)),
                  pl.BlockSpec((HEAD_DIM, s), lambda h, i: (v_row + h, 0)),
                  pl.BlockSpec((tq, HEAD_DIM), lambda h, i: (i, g_col + h)),
                  pl.BlockSpec((4, D_QK), lambda h, i: (0, 0)),
                  pl.BlockSpec((1, HEAD_DIM), lambda h, i: (0, 0))],
        out_specs=pl.BlockSpec((tq, HEAD_DIM), lambda h, i: (i, h)),
        scratch_shapes=[row_stat, acc, row_stat, acc] + [pltpu.VMEM((tk, tq), F32)] * 4,
        compiler_params=_params("parallel", "parallel"),
        name="mixer_d",
    )(qk_arr, qk_arr, vt, g_arr, lam_vecs, subln.reshape(1, HEAD_DIM))


def _out_proj_kernel(ya_ref, yb_ref, w_ref, h_ref, g_ref, o_ref):
    half = ya_ref.shape[1]
    y = (jnp.dot(ya_ref[...], w_ref[:half, :], preferred_element_type=F32)
         + jnp.dot(yb_ref[...], w_ref[half:, :], preferred_element_type=F32))
    ms = jnp.mean(y * y, axis=-1, keepdims=True)
    o_ref[...] = h_ref[...] + y * lax.rsqrt(ms + EPS) * g_ref[...]


def _out_proj(ya, yb, w, h, g):
    s, d = h.shape
    tm = min(512, s)
    return pl.pallas_call(
        _out_proj_kernel,
        out_shape=jax.ShapeDtypeStruct((s, d), F32),
        grid=(s // tm,),
        in_specs=[pl.BlockSpec((tm, BRANCH), lambda i: (i, 0)),
                  pl.BlockSpec((tm, BRANCH), lambda i: (i, 0)),
                  pl.BlockSpec((2 * BRANCH, d), lambda i: (0, 0)),
                  pl.BlockSpec((tm, d), lambda i: (i, 0)),
                  pl.BlockSpec((1, d), lambda i: (0, 0))],
        out_specs=pl.BlockSpec((tm, d), lambda i: (i, 0)),
        compiler_params=_params("parallel"),
        name="out_proj",
    )(ya, yb, w, h, g.reshape(1, d))


def _rel_bias_tile(rel_bias, tq):
    rows = 2 * A_PAD + tq
    period = rows + tq
    k = np.arange(period)
    c_minus_r = np.where(k < rows, k, k - period)
    line = rel_bias.astype(F32)[:, np.clip(A_PAD - c_minus_r, -REL_CLIP, REL_CLIP) + REL_CLIP] * LOG2E
    flat = jnp.tile(line, (1, tq))[:, :tq * (period - 1)]
    tile = flat.reshape(rel_bias.shape[0], tq, period - 1)[:, :, :rows]
    qc = np.arange(tq)[:, None] >> CHUNK_SHIFT
    kc = np.arange(rows)[None, :] >> CHUNK_SHIFT
    in_band = (kc >= qc) & (kc <= qc + A_LEFT_CHUNKS)
    return jnp.swapaxes(jnp.where(in_band, tile, NEG_INF), 1, 2)


def _even_layer(h, w_in, rel_bias, w_out, norm_pre, norm_post):
    s = h.shape[0]
    u = _rmsnorm(h, norm_pre)
    b = BRANCH
    log2_scale = HEAD_DIM ** -0.5 * LOG2E
    w_rows = jnp.concatenate([w_in[:, :b] * log2_scale, w_in[:, b:2 * b],
                              w_in[:, 4 * b:5 * b] * log2_scale, w_in[:, 5 * b:6 * b]], axis=1).astype(BF16)
    w_g = jnp.concatenate([w_in[:, 3 * b:4 * b], w_in[:, 7 * b:]], axis=1).astype(BF16)
    w_vt = jnp.concatenate([w_in[:, 2 * b:3 * b], w_in[:, 6 * b:7 * b]], axis=1).T.astype(BF16)
    rows = _matmul(u, w_rows, BF16)
    gates = _matmul(u, w_g, F32)
    vt = _matmul_t(w_vt, u, BF16)
    tq_a = min(256, s)
    y_a = _mixer_a(rows, 0, N_HEADS, vt, 0, gates, 0, _rel_bias_tile(rel_bias, tq_a), tq_a)
    y_b = _mixer_b(rows, 2 * N_HEADS, 3 * N_HEADS, vt, N_HEADS, gates, N_HEADS, min(512, s), min(256, s))
    return _out_proj(y_a, y_b, w_out.astype(BF16), h, norm_post)


def _odd_layer(h, positions, w_in, forget_bias, lq1, lk1, lq2, lk2, subln, w_out, norm_pre, norm_post,
               lambda_init):
    s, d = h.shape
    u = _rmsnorm(h, norm_pre)
    b = BRANCH
    f0 = 4 * b
    d0 = f0 + N_HEADS
    v0 = d0 + 4 * N_HEADS * D_QK
    w_rows = jnp.concatenate([w_in[:, :b] * (HEAD_DIM ** -0.5 * LOG2E), w_in[:, b:2 * b]],
                             axis=1).astype(BF16)
    w_g = jnp.concatenate([w_in[:, 3 * b:4 * b], w_in[:, v0 + b:]], axis=1).astype(BF16)
    w_vt = jnp.concatenate([w_in[:, 2 * b:3 * b], w_in[:, v0:v0 + b]], axis=1).T.astype(BF16)
    wd = w_in[:, d0:v0].reshape(d, 4, N_HEADS, D_QK)
    w_qk = jnp.concatenate([
        jnp.concatenate([wd[:, 0], wd[:, 1]], axis=2).reshape(d, b) * (D_QK ** -0.5 * LOG2E),
        jnp.concatenate([wd[:, 2], wd[:, 3]], axis=2).reshape(d, b),
    ], axis=1).astype(BF16)
    w_f = jnp.pad(w_in[:, f0:d0], ((0, 0), (0, HEAD_DIM - N_HEADS))).astype(BF16)
    b_f = jnp.pad(forget_bias.astype(F32), (0, HEAD_DIM - N_HEADS)).reshape(1, HEAD_DIM)

    rows = _matmul(u, w_rows, BF16)
    gates = _matmul(u, w_g, F32)
    vt = _matmul_t(w_vt, u, BF16)
    qk_d = _matmul(u, w_qk, F32)

    qx, kx = _decay_features(u, w_f, b_f)
    tq = tk = min(512, s)
    y_c = _mixer_c(rows, 0, N_HEADS, qx, kx, vt, 0, gates, 0, tq, tk)

    qk_rot = _rope(qk_d, positions)
    lam_vecs = jnp.stack([lq1, lk1, lq2, lk2]).astype(F32)
    y_d = _mixer_d(qk_rot, 0, N_HEADS, vt, N_HEADS, gates, N_HEADS, lam_vecs, subln, lambda_init, tq, tk)
    return _out_proj(y_c, y_d, w_out.astype(BF16), h, norm_post)


def kernel(x, positions, even_w_in, even_rel_bias, even_w_out, even_norm_pre, even_norm_post,
           odd_w_in, odd_forget_bias, odd_lambda_q1, odd_lambda_k1, odd_lambda_q2, odd_lambda_k2,
           odd_subln, odd_w_out, odd_norm_pre, odd_norm_post):
    assert x.shape[0] == 1
    h = x[0]
    depth = even_w_in.shape[0] + odd_w_in.shape[0]
    for layer in range(depth):
        i = layer // 2
        if layer % 2 == 0:
            h = _even_layer(h, even_w_in[i], even_rel_bias[i], even_w_out[i],
                            even_norm_pre[i], even_norm_post[i])
        else:
            lambda_init = 0.8 - 0.6 * math.exp(-0.3 * layer)
            h = _odd_layer(h, positions, odd_w_in[i], odd_forget_bias[i],
                           odd_lambda_q1[i], odd_lambda_k1[i], odd_lambda_q2[i], odd_lambda_k2[i],
                           odd_subln[i], odd_w_out[i], odd_norm_pre[i], odd_norm_post[i], lambda_init)
    return h[None]
```
